```python
import math
import jax, jax.numpy as jnp
from jax import lax
import numpy as np

D_MODEL = 1024
BATCH = 8
SEQ = 2048
DEPTH = 2

N_META = 16
A_HEADS = 8
A_KV_HEADS = 2
A_GROUP = A_HEADS // A_KV_HEADS
HEAD_DIM = 64
WINDOW = 128
BLOCK = 128
A_WIDTH = A_HEADS * HEAD_DIM
KV_WIDTH = A_KV_HEADS * HEAD_DIM
B_WIDTH = D_MODEL // 2
CONV_WIDTH = 3
IN_WIDTH = A_WIDTH + 2 * KV_WIDTH + 3 * B_WIDTH
MIX_WIDTH = A_WIDTH + B_WIDTH
SPLITS = (A_WIDTH, A_WIDTH + KV_WIDTH, A_WIDTH + 2 * KV_WIDTH,
          A_WIDTH + 2 * KV_WIDTH + B_WIDTH, A_WIDTH + 2 * KV_WIDTH + 2 * B_WIDTH)
POOL_SIZES = (2, 4, 8, 16)
N_POOL_GROUPS = 4
POOL_GROUP_DIM = D_MODEL // N_POOL_GROUPS
REL_BUCKETS = 32
REL_MAX_DIST = 128
D_FF = 2816
HALF_STEP = 0.5
RMS_EPS = 1e-6
N_EVEN = (DEPTH + 1) // 2
N_ODD = DEPTH // 2

kernel_name = "hybrid_swa_conv_pool_macaron"


def rmsnorm(x, g):
    xf = x.astype(jnp.float32)
    y = xf * lax.rsqrt(jnp.mean(xf * xf, axis=-1, keepdims=True) + RMS_EPS)
    return (y * g.astype(jnp.float32)).astype(x.dtype)


def swiglu(x, w1, w2):
    gate, up = jnp.split(x @ w1, 2, axis=-1)
    return (jax.nn.silu(gate) * up) @ w2


def t5_bucket(dist):
    n = jnp.maximum(dist, 0)
    max_exact = REL_BUCKETS // 2
    nf = jnp.maximum(n, 1).astype(jnp.float32)
    large = max_exact + (jnp.log(nf / max_exact) / math.log(REL_MAX_DIST / max_exact)
                         * (REL_BUCKETS - max_exact)).astype(jnp.int32)
    large = jnp.minimum(large, REL_BUCKETS - 1)
    return jnp.where(n < max_exact, n, large)


def sliding_window_attention(q, k, v, sinks, rel_bias):
    Bsz, L = q.shape[0], q.shape[1]
    pad = BLOCK - N_META
    nb = (L + pad) // BLOCK
    q = q * (HEAD_DIM ** -0.5)
    padr = lambda t: jnp.pad(t, ((0, 0), (pad, 0), (0, 0)))
    qb = padr(q).reshape(Bsz, nb, BLOCK, A_KV_HEADS, A_GROUP, HEAD_DIM)
    kb = padr(k).reshape(Bsz, nb, BLOCK, A_KV_HEADS, HEAD_DIM)
    vb = padr(v).reshape(Bsz, nb, BLOCK, A_KV_HEADS, HEAD_DIM)

    def band(t):
        prev = jnp.pad(t[:, :-1], ((0, 0), (1, 0), (0, 0), (0, 0), (0, 0)))
        return jnp.concatenate([prev, t], axis=2)

    k_band, v_band = band(kb), band(vb)
    k_meta = k[:, :N_META].reshape(Bsz, N_META, A_KV_HEADS, HEAD_DIM)
    v_meta = v[:, :N_META].reshape(Bsz, N_META, A_KV_HEADS, HEAD_DIM)

    blk = jnp.arange(nb)[:, None, None]
    r = jnp.arange(BLOCK)[None, :, None]
    s = jnp.arange(2 * BLOCK)[None, None, :]
    dist_band = BLOCK + r[0] - s[0]
    k_pos = (blk - 1) * BLOCK + s
    band_ok = (dist_band >= 0) & (dist_band < WINDOW) & (k_pos >= BLOCK)
    bias_band = jnp.moveaxis(rel_bias[t5_bucket(dist_band)].astype(jnp.float32), -1, 0)
    bias_band = bias_band.reshape(A_KV_HEADS, A_GROUP, 1, BLOCK, 2 * BLOCK)
    q_pos = blk * BLOCK + r
    dist_meta = q_pos - (pad + jnp.arange(N_META)[None, None, :])
    meta_ok = dist_meta >= 0
    bias_meta = jnp.moveaxis(rel_bias[t5_bucket(dist_meta)].astype(jnp.float32), -1, 0)
    bias_meta = bias_meta.reshape(A_KV_HEADS, A_GROUP, nb, BLOCK, N_META)

    neg = jnp.float32(-jnp.inf)
    lg_band = jnp.einsum('bnqkgd,bnskd->bkgnqs', qb, k_band).astype(jnp.float32) + bias_band
    lg_band = jnp.where(band_ok, lg_band, neg)
    lg_meta = jnp.einsum('bnqkgd,bmkd->bkgnqm', qb, k_meta).astype(jnp.float32) + bias_meta
    lg_meta = jnp.where(meta_ok, lg_meta, neg)
    sink = jnp.broadcast_to(sinks.astype(jnp.float32).reshape(1, A_KV_HEADS, A_GROUP, 1, 1, 1),
                            lg_band.shape[:-1] + (1,))
    probs = jax.nn.softmax(jnp.concatenate([lg_band, lg_meta, sink], axis=-1), axis=-1)
    p_band = probs[..., :2 * BLOCK].astype(v.dtype)
    p_meta = probs[..., 2 * BLOCK:2 * BLOCK + N_META].astype(v.dtype)
    out = (jnp.einsum('bkgnqs,bnskd->bnqkgd', p_band, v_band)
           + jnp.einsum('bkgnqm,bmkd->bnqkgd', p_meta, v_meta))
    return out.reshape(Bsz, nb * BLOCK, A_WIDTH)[:, pad:]


def short_conv(b_gate, c_gate, x_in, conv_w):
    u = c_gate * x_in
    L = u.shape[1]
    up = jnp.pad(u, ((0, 0), (CONV_WIDTH - 1, 0), (0, 0)))
    y = conv_w[0] * u
    for j in range(1, CONV_WIDTH):
        y = y + conv_w[j] * up[:, CONV_WIDTH - 1 - j:CONV_WIDTH - 1 - j + L]
    return b_gate * y


def attn_conv_mixer(u, w_in, w_out, sinks, conv_w, rel_bias):
    z = u @ w_in
    q, k, v, b_gate, c_gate, x_in = jnp.split(z, SPLITS, axis=-1)
    y_attn = sliding_window_attention(q, k, v, sinks, rel_bias)
    y_conv = short_conv(b_gate, c_gate, x_in, conv_w)
    return jnp.concatenate([y_attn, y_conv], axis=-1) @ w_out


def pool_mixer(u, pool_w, pool_scale):
    Bsz, L, _ = u.shape
    xg = u.astype(jnp.float32).reshape(Bsz, L, N_POOL_GROUPS, POOL_GROUP_DIM)
    cs = jnp.pad(jnp.cumsum(xg, axis=1), ((0, 0), (1, 0), (0, 0), (0, 0)))
    t = jnp.arange(L)
    pooled = []
    for gi, w in enumerate(POOL_SIZES):
        start = jnp.pad(cs[:, :L + 1 - w, gi], ((0, 0), (w - 1, 0), (0, 0)))
        cnt = jnp.minimum(t + 1, w).astype(jnp.float32)[None, :, None]
        pooled.append((cs[:, 1:, gi] - start) / cnt)
    d = jnp.stack(pooled, axis=2) - xg
    y = jnp.einsum('blgc,gcd->blgd', d, pool_w.astype(jnp.float32)).reshape(Bsz, L, D_MODEL)
    return (y * pool_scale.astype(jnp.float32)).astype(u.dtype)


def setup_inputs(seed: int = 0) -> dict:
    key = jax.random.key(seed)
    ks = jax.random.split(key, 12)
    nrm = lambda k, shape, s: jax.random.normal(k, shape, jnp.float32) * s
    return {
        "x": nrm(ks[0], (BATCH, SEQ, D_MODEL), 1.0),
        "meta_tokens": nrm(ks[1], (N_META, D_MODEL), 1.0),
        "rel_bias": nrm(ks[2], (REL_BUCKETS, A_HEADS), 0.5),
        "norm_g": 1.0 + nrm(ks[3], (DEPTH, 6, D_MODEL), 0.1),
        "ffn_w1": nrm(ks[4], (DEPTH, 2, D_MODEL, 2 * D_FF), D_MODEL ** -0.5),
        "ffn_w2": nrm(ks[5], (DEPTH, 2, D_FF, D_MODEL), D_FF ** -0.5),
        "mix_w_in": nrm(ks[6], (N_EVEN, D_MODEL, IN_WIDTH), D_MODEL ** -0.5),
        "mix_w_out": nrm(ks[7], (N_EVEN, MIX_WIDTH, D_MODEL), MIX_WIDTH ** -0.5),
        "attn_sinks": nrm(ks[8], (N_EVEN, A_HEADS), 0.5),
        "conv_w": nrm(ks[9], (N_EVEN, CONV_WIDTH, B_WIDTH), CONV_WIDTH ** -0.5),
        "pool_w": nrm(ks[10], (N_ODD, N_POOL_GROUPS, POOL_GROUP_DIM, POOL_GROUP_DIM), POOL_GROUP_DIM ** -0.5),
        "pool_scale": 1.0 + nrm(ks[11], (N_ODD, D_MODEL), 0.1),
    }


def reference(x, meta_tokens, rel_bias, norm_g, ffn_w1, ffn_w2, mix_w_in, mix_w_out,
              attn_sinks, conv_w, pool_w, pool_scale):
    Bsz = x.shape[0]
    meta = jnp.broadcast_to(meta_tokens.astype(x.dtype)[None], (Bsz, N_META, D_MODEL))
    h = jnp.concatenate([meta, x], axis=1)
    for layer in range(DEPTH):
        g = norm_g[layer]
        h = h + HALF_STEP * rmsnorm(swiglu(rmsnorm(h, g[0]), ffn_w1[layer, 0], ffn_w2[layer, 0]), g[1])
        u = rmsnorm(h, g[2])
        if layer % 2 == 0:
            e = layer // 2
            mix = attn_conv_mixer(u, mix_w_in[e], mix_w_out[e], attn_sinks[e], conv_w[e], rel_bias)
        else:
            o = layer // 2
            mix = pool_mixer(u, pool_w[o], pool_scale[o])
        h = h + rmsnorm(mix, g[3])
        h = h + HALF_STEP * rmsnorm(swiglu(rmsnorm(h, g[4]), ffn_w1[layer, 1], ffn_w2[layer, 1]), g[5])
    return h[:, N_META:]
```

```python
import functools
import math

import jax
import jax.numpy as jnp
from jax.experimental import pallas as pl
from jax.experimental.pallas import tpu as pltpu

D_MODEL = 1024
N_META = 16
A_HEADS = 8
A_KV_HEADS = 2
A_GROUP = A_HEADS // A_KV_HEADS
HEAD_DIM = 64
WINDOW = 128
BLOCK = 128
A_WIDTH = A_HEADS * HEAD_DIM
KV_WIDTH = A_KV_HEADS * HEAD_DIM
B_WIDTH = D_MODEL // 2
CONV_WIDTH = 3
IN_WIDTH = A_WIDTH + 2 * KV_WIDTH + 3 * B_WIDTH
POOL_SIZES = (2, 4, 8, 16)
N_POOL_GROUPS = 4
POOL_GROUP_DIM = D_MODEL // N_POOL_GROUPS
REL_BUCKETS = 32
REL_MAX_DIST = 128
D_FF = 2816
HALF_STEP = 0.5
RMS_EPS = 1e-6

VMEM_LIMIT_BYTES_V7X = 56 * 1024 * 1024
ROW_TILE = 512
FF_CHUNK = 256
N_FF_CHUNKS = D_FF // FF_CHUNK
POOL_TILE = 512
POOL_HALO = 16
CONV_HALO = 8

NEG_INF = float("-inf")


def _rms(x, g):
    return x * jax.lax.rsqrt(jnp.mean(x * x, axis=-1, keepdims=True) + RMS_EPS) * g


def _const_spec(shape):
    zeros = (0,) * len(shape)
    return pl.BlockSpec(shape, lambda *_: zeros, pipeline_mode=pl.Buffered(1))


def _params(n_axes):
    return pltpu.CompilerParams(dimension_semantics=("arbitrary",) * n_axes,
                                vmem_limit_bytes=VMEM_LIMIT_BYTES_V7X)


def _ffn_kernel(h_ref, gpre_ref, w1_ref, w2_ref, gpost_ref, o_ref):
    x = h_ref[...]
    xn = _rms(x, gpre_ref[...]).astype(jnp.bfloat16)
    acc = jnp.zeros(x.shape, jnp.float32)
    for c in range(N_FF_CHUNKS):
        gu = jnp.dot(xn, w1_ref[c], preferred_element_type=jnp.float32)
        gate = gu[:, :FF_CHUNK]
        up = gu[:, FF_CHUNK:]
        act = (gate * (1.0 / (1.0 + jnp.exp(-gate))) * up).astype(jnp.bfloat16)
        acc = acc + jnp.dot(act, w2_ref[c], preferred_element_type=jnp.float32)
    o_ref[...] = x + HALF_STEP * _rms(acc, gpost_ref[...])


def _ffn(h, g_pre, w1c, w2c, g_post):
    n = h.shape[0]
    tm = min(ROW_TILE, n)
    row = pl.BlockSpec((tm, D_MODEL), lambda i: (i, 0))
    return pl.pallas_call(
        _ffn_kernel,
        grid=(n // tm,),
        in_specs=[row, _const_spec((1, D_MODEL)), _const_spec(w1c.shape), _const_spec(w2c.shape),
                  _const_spec((1, D_MODEL))],
        out_specs=row,
        out_shape=jax.ShapeDtypeStruct(h.shape, h.dtype),
        compiler_params=_params(1),
        name="ffn",
    )(h, g_pre, w1c, w2c, g_post)


def _prep_ffn_weights(w1, w2):
    gate = w1[:, :D_FF].reshape(D_MODEL, N_FF_CHUNKS, FF_CHUNK)
    up = w1[:, D_FF:].reshape(D_MODEL, N_FF_CHUNKS, FF_CHUNK)
    w1c = jnp.concatenate([gate, up], axis=-1).transpose(1, 0, 2).astype(jnp.bfloat16)
    w2c = w2.reshape(N_FF_CHUNKS, FF_CHUNK, D_MODEL).astype(jnp.bfloat16)
    return w1c, w2c


def _inproj_kernel(tiles_per_seq, h_ref, g_ref, win_ref, convw_ref, hist_ref,
                   q_ref, k_ref, v_ref, yc_ref, tail_ref, uc_scr):
    i = pl.program_id(0)
    tm = h_ref.shape[0]
    u = _rms(h_ref[...], g_ref[...]).astype(jnp.bfloat16)
    z = jnp.dot(u, win_ref[...], preferred_element_type=jnp.float32)
    q_ref[...] = (z[:, :A_WIDTH] * (HEAD_DIM ** -0.5)).astype(jnp.bfloat16)
    k_ref[...] = z[:, A_WIDTH:A_WIDTH + KV_WIDTH].astype(jnp.bfloat16)
    v_ref[...] = z[:, A_WIDTH + KV_WIDTH:A_WIDTH + 2 * KV_WIDTH].astype(jnp.bfloat16)
    o = A_WIDTH + 2 * KV_WIDTH
    b_gate = z[:, o:o + B_WIDTH]
    c_gate = z[:, o + B_WIDTH:o + 2 * B_WIDTH]
    x_in = z[:, o + 2 * B_WIDTH:o + 3 * B_WIDTH]
    uc = c_gate * x_in

    @pl.when(i % tiles_per_seq == 0)
    def _():
        uc_scr[0:CONV_HALO, :] = hist_ref[...]

    uc_scr[CONV_HALO:CONV_HALO + tm, :] = uc
    y = convw_ref[0:1, :] * uc
    for j in range(1, CONV_WIDTH):
        y = y + convw_ref[j:j + 1, :] * uc_scr[CONV_HALO - j:CONV_HALO - j + tm, :]
    yc_ref[...] = (b_gate * y).astype(jnp.bfloat16)
    tail = uc[tm - CONV_HALO:, :]
    tail_ref[...] = tail
    uc_scr[0:CONV_HALO, :] = tail


def _inproj(h, g, w_in, conv_w, hist, rows_per_seq):
    n = h.shape[0]
    tm = min(ROW_TILE, n)
    row = lambda w: pl.BlockSpec((tm, w), lambda i: (i, 0))
    bf = lambda w: jax.ShapeDtypeStruct((n, w), jnp.bfloat16)
    return pl.pallas_call(
        functools.partial(_inproj_kernel, rows_per_seq // tm),
        grid=(n // tm,),
        in_specs=[row(D_MODEL), _const_spec((1, D_MODEL)), _const_spec(w_in.shape),
                  _const_spec(conv_w.shape), _const_spec((CONV_HALO, B_WIDTH))],
        out_specs=[row(A_WIDTH), row(KV_WIDTH), row(KV_WIDTH), row(B_WIDTH),
                   pl.BlockSpec((CONV_HALO, B_WIDTH), lambda i: (0, 0))],
        out_shape=[bf(A_WIDTH), bf(KV_WIDTH), bf(KV_WIDTH), bf(B_WIDTH),
                   jax.ShapeDtypeStruct((CONV_HALO, B_WIDTH), jnp.float32)],
        scratch_shapes=[pltpu.VMEM((CONV_HALO + tm, B_WIDTH), jnp.float32)],
        compiler_params=_params(1),
        name="inproj_conv",
    )(h, g, w_in, conv_w, hist)


def _attend(q, k_band, v_band, k_meta, v_meta, bias_band, bias_meta, sink_ref, band_mask):
    outs = []
    for h in range(A_HEADS):
        hk = h // A_GROUP
        qh = q[:, h * HEAD_DIM:(h + 1) * HEAD_DIM]
        ks = slice(hk * HEAD_DIM, (hk + 1) * HEAD_DIM)
        dn = (((1,), (1,)), ((), ()))
        lgm = jax.lax.dot_general(qh, k_meta[:, ks], dn, preferred_element_type=jnp.float32)
        lgm = lgm + bias_meta[h]
        sink = sink_ref[h]
        m = jnp.maximum(jnp.max(lgm, axis=-1, keepdims=True), sink)
        if k_band is not None:
            lg = jax.lax.dot_general(qh, k_band[:, ks], dn, preferred_element_type=jnp.float32)
            lg = lg + bias_band[h]
            if band_mask is not None:
                lg = jnp.where(band_mask, lg, NEG_INF)
            m = jnp.maximum(m, jnp.max(lg, axis=-1, keepdims=True))
        pm = jnp.exp(lgm - m)
        den = jnp.sum(pm, axis=-1, keepdims=True) + jnp.exp(sink - m)
        o = jnp.dot(pm.astype(jnp.bfloat16), v_meta[:, ks], preferred_element_type=jnp.float32)
        if k_band is not None:
            pb = jnp.exp(lg - m)
            den = den + jnp.sum(pb, axis=-1, keepdims=True)
            o = o + jnp.dot(pb.astype(jnp.bfloat16), v_band[:, ks], preferred_element_type=jnp.float32)
        outs.append(o / den)
    return jnp.concatenate(outs, axis=-1)


def _attn_kernel(q_ref, kp_ref, kc_ref, vp_ref, vc_ref, km_ref, vm_ref, bb_ref, bm_ref, sink_ref, o_ref):
    n = pl.program_id(1)
    k_band = jnp.concatenate([kp_ref[...], kc_ref[...]], axis=0)
    v_band = jnp.concatenate([vp_ref[...], vc_ref[...]], axis=0)
    col = jax.lax.broadcasted_iota(jnp.int32, (BLOCK, 2 * BLOCK), 1)
    band_mask = col >= jnp.where(n > 0, 0, BLOCK)
    o = _attend(q_ref[...], k_band, v_band, km_ref[...], vm_ref[...], bb_ref, bm_ref, sink_ref, band_mask)
    o_ref[...] = o.astype(o_ref.dtype)


def _attn_meta_kernel(q_ref, km_ref, vm_ref, bm_ref, sink_ref, o_ref):
    o = _attend(q_ref[...], None, None, km_ref[...], vm_ref[...], None, bm_ref, sink_ref, None)
    o_ref[...] = o.astype(o_ref.dtype)


def _attention(q, k, v, k_meta, v_meta, bias_band, bias_meta, sinks, batch, seq):
    nb = seq // BLOCK
    cur = lambda w: pl.BlockSpec((BLOCK, w), lambda b, n: (b * nb + n, 0))
    prev = lambda w: pl.BlockSpec((BLOCK, w), lambda b, n: (b * nb + jnp.maximum(n - 1, 0), 0))
    return pl.pallas_call(
        _attn_kernel,
        grid=(batch, nb),
        in_specs=[cur(A_WIDTH), prev(KV_WIDTH), cur(KV_WIDTH), prev(KV_WIDTH), cur(KV_WIDTH),
                  _const_spec((N_META, KV_WIDTH)), _const_spec((N_META, KV_WIDTH)),
                  _const_spec(bias_band.shape),
                  pl.BlockSpec((A_HEADS, BLOCK, N_META), lambda b, n: (0, n, 0)),
                  pl.BlockSpec(memory_space=pltpu.SMEM)],
        out_specs=cur(A_WIDTH),
        out_shape=jax.ShapeDtypeStruct(q.shape, jnp.bfloat16),
        compiler_params=_params(2),
        name="swa_attention",
    )(q, k, k, v, v, k_meta, v_meta, bias_band, bias_meta, sinks)


def _attention_meta(q, k, v, bias_self, sinks):
    return pl.pallas_call(
        _attn_meta_kernel,
        in_specs=[pl.BlockSpec(memory_space=pltpu.VMEM)] * 4 + [pl.BlockSpec(memory_space=pltpu.SMEM)],
        out_specs=pl.BlockSpec(memory_space=pltpu.VMEM),
        out_shape=jax.ShapeDtypeStruct(q.shape, jnp.bfloat16),
        name="meta_attention",
    )(q, k, v, bias_self, sinks)


def _t5_bucket(dist):
    n = jnp.maximum(dist, 0)
    max_exact = REL_BUCKETS // 2
    nf = jnp.maximum(n, 1).astype(jnp.float32)
    large = max_exact + (jnp.log(nf / max_exact) / math.log(REL_MAX_DIST / max_exact)
                         * (REL_BUCKETS - max_exact)).astype(jnp.int32)
    large = jnp.minimum(large, REL_BUCKETS - 1)
    return jnp.where(n < max_exact, n, large)


def _bias_tables(rel_bias, seq):
    rel = rel_bias.astype(jnp.float32)
    look = lambda dist, ok: jnp.where(ok[None], jnp.moveaxis(rel[_t5_bucket(dist)], -1, 0), NEG_INF)
    r = jnp.arange(BLOCK)[:, None]
    s = jnp.arange(2 * BLOCK)[None, :]
    dist_band = BLOCK + r - s
    band = look(dist_band, (dist_band >= 0) & (dist_band < WINDOW))
    m = jnp.arange(N_META)[None, :]
    dist_meta = N_META + jnp.arange(seq)[:, None] - m
    meta = look(dist_meta, dist_meta >= 0)
    dist_self = jnp.arange(N_META)[:, None] - m
    self_ = look(dist_self, dist_self >= 0)
    return band, meta, self_


def _outproj_kernel(h_ref, ya_ref, yc_ref, woa_ref, woc_ref, g_ref, o_ref):
    mix = jnp.dot(ya_ref[...], woa_ref[...], preferred_element_type=jnp.float32)
    mix = mix + jnp.dot(yc_ref[...], woc_ref[...], preferred_element_type=jnp.float32)
    o_ref[...] = h_ref[...] + _rms(mix, g_ref[...])


def _outproj(h, ya, yc, wo_a, wo_c, g):
    n = h.shape[0]
    tm = min(ROW_TILE, n)
    row = lambda w: pl.BlockSpec((tm, w), lambda i: (i, 0))
    return pl.pallas_call(
        _outproj_kernel,
        grid=(n // tm,),
        in_specs=[row(D_MODEL), row(A_WIDTH), row(B_WIDTH), _const_spec(wo_a.shape),
                  _const_spec(wo_c.shape), _const_spec((1, D_MODEL))],
        out_specs=row(D_MODEL),
        out_shape=jax.ShapeDtypeStruct(h.shape, h.dtype),
        compiler_params=_params(1),
        name="outproj",
    )(h, ya, yc, wo_a, wo_c, g)


def _pool_kernel(tiles_per_seq, pos0, h_ref, prev_ref, first_ref, gpre_ref, pw_ref, ps_ref, gpost_ref,
                 o_ref, u_scr):
    j = pl.program_id(0) % tiles_per_seq
    tm = h_ref.shape[0]
    x = h_ref[...]
    gpre = gpre_ref[...]
    hist = jnp.where(j == 0, first_ref[...], prev_ref[...])
    u = _rms(x, gpre)
    u_scr[0:POOL_HALO, :] = _rms(hist, gpre)
    u_scr[POOL_HALO:POOL_HALO + tm, :] = u
    t = pos0 + j * tm + jax.lax.broadcasted_iota(jnp.int32, (tm, 1), 0)
    ys = []
    for gi, w in enumerate(POOL_SIZES):
        cols = slice(gi * POOL_GROUP_DIM, (gi + 1) * POOL_GROUP_DIM)
        ug = u[:, cols]
        s = ug
        for back in range(1, w):
            s = s + u_scr[POOL_HALO - back:POOL_HALO - back + tm, cols]
        cnt = jnp.minimum(t + 1, w).astype(jnp.float32)
        d = (s / cnt - ug).astype(jnp.bfloat16)
        ys.append(jnp.dot(d, pw_ref[gi], preferred_element_type=jnp.float32))
    y = jnp.concatenate(ys, axis=-1) * ps_ref[...]
    o_ref[...] = x + _rms(y, gpost_ref[...])


def _pool(h, first_hist, g_pre, pool_w, pool_scale, g_post, rows_per_seq, pos0):
    n = h.shape[0]
    tm = min(POOL_TILE, n)
    halo_blocks = tm // POOL_HALO
    row = pl.BlockSpec((tm, D_MODEL), lambda i: (i, 0))
    prev = pl.BlockSpec((POOL_HALO, D_MODEL), lambda i: (jnp.maximum(i * halo_blocks - 1, 0), 0))
    return pl.pallas_call(
        functools.partial(_pool_kernel, rows_per_seq // tm, pos0),
        grid=(n // tm,),
        in_specs=[row, prev, _const_spec((POOL_HALO, D_MODEL)), _const_spec((1, D_MODEL)),
                  _const_spec(pool_w.shape), _const_spec((1, D_MODEL)), _const_spec((1, D_MODEL))],
        out_specs=row,
        out_shape=jax.ShapeDtypeStruct(h.shape, h.dtype),
        scratch_shapes=[pltpu.VMEM((POOL_HALO + tm, D_MODEL), jnp.float32)],
        compiler_params=_params(1),
        name="pool_mixer",
    )(h, h, first_hist, g_pre, pool_w, pool_scale, g_post)


def kernel(x, meta_tokens, rel_bias, norm_g, ffn_w1, ffn_w2, mix_w_in, mix_w_out, attn_sinks,
           conv_w, pool_w, pool_scale):
    batch, seq, _ = x.shape
    depth = norm_g.shape[0]
    h = x.reshape(batch * seq, D_MODEL)
    hm = meta_tokens.astype(x.dtype)
    band_b, meta_b, self_b = _bias_tables(rel_bias, seq)
    for layer in range(depth):
        g = norm_g[layer].astype(jnp.float32).reshape(6, 1, D_MODEL)
        last = layer == depth - 1
        mix_is_attn = layer % 2 == 0
        w1c, w2c = _prep_ffn_weights(ffn_w1[layer, 0], ffn_w2[layer, 0])
        h = _ffn(h, g[0], w1c, w2c, g[1])
        hm = _ffn(hm, g[0], w1c, w2c, g[1])
        if mix_is_attn:
            e = layer // 2
            w_in = mix_w_in[e].astype(jnp.bfloat16)
            wo_a = mix_w_out[e, :A_WIDTH].astype(jnp.bfloat16)
            wo_c = mix_w_out[e, A_WIDTH:].astype(jnp.bfloat16)
            cw = conv_w[e].astype(jnp.float32)
            sinks = attn_sinks[e].astype(jnp.float32)
            no_hist = jnp.zeros((CONV_HALO, B_WIDTH), jnp.float32)
            qm, km, vm, ycm, tail_m = _inproj(hm, g[2], w_in, cw, no_hist, N_META)
            q, k, v, yc, _ = _inproj(h, g[2], w_in, cw, tail_m, seq)
            ya = _attention(q, k, v, km, vm, band_b, meta_b, sinks, batch, seq)
            h = _outproj(h, ya, yc, wo_a, wo_c, g[3])
            if not last:
                yam = _attention_meta(qm, km, vm, self_b, sinks)
                hm = _outproj(hm, yam, ycm, wo_a, wo_c, g[3])
        else:
            o = layer // 2
            pw = pool_w[o].astype(jnp.bfloat16)
            ps = pool_scale[o].astype(jnp.float32).reshape(1, D_MODEL)
            h_new = _pool(h, hm, g[2], pw, ps, g[3], seq, N_META)
            if not last:
                hm = _pool(hm, jnp.zeros((POOL_HALO, D_MODEL), hm.dtype), g[2], pw, ps, g[3], N_META, 0)
            h = h_new
        w1c, w2c = _prep_ffn_weights(ffn_w1[layer, 1], ffn_w2[layer, 1])
        h = _ffn(h, g[4], w1c, w2c, g[5])
        if not last:
            hm = _ffn(hm, g[4], w1c, w2c, g[5])
    return h.reshape(batch, seq, D_MODEL)
```

```python
import functools
import math

import jax
import jax.numpy as jnp
from jax.experimental import pallas as pl
from jax.experimental.pallas import tpu as pltpu

D_MODEL = 1024
N_META = 16
A_HEADS = 8
A_KV_HEADS = 2
A_GROUP = A_HEADS // A_KV_HEADS
HEAD_DIM = 64
WINDOW = 128
BLOCK = 128
A_WIDTH = A_HEADS * HEAD_DIM
KV_WIDTH = A_KV_HEADS * HEAD_DIM
B_WIDTH = D_MODEL // 2
CONV_WIDTH = 3
IN_WIDTH = A_WIDTH + 2 * KV_WIDTH + 3 * B_WIDTH
POOL_SIZES = (2, 4, 8, 16)
N_POOL_GROUPS = 4
POOL_GROUP_DIM = D_MODEL // N_POOL_GROUPS
REL_BUCKETS = 32
REL_MAX_DIST = 128
D_FF = 2816
HALF_STEP = 0.5
RMS_EPS = 1e-6

VMEM_LIMIT_BYTES_V7X = 56 * 1024 * 1024
ROW_TILE = 512
FF_CHUNK = 256
N_FF_CHUNKS = D_FF // FF_CHUNK
POOL_TILE = 512
POOL_HALO = 16
CONV_HALO = 8
ATTN_QBLOCKS = 4
GROUP_LANES = A_GROUP * BLOCK

NEG_INF = float("-inf")
_NT = (((1,), (1,)), ((), ()))

assert WINDOW == BLOCK and N_META == POOL_HALO


def _rms(x, g):
    return x * jax.lax.rsqrt(jnp.mean(x * x, axis=-1, keepdims=True) + RMS_EPS) * g


def _const_spec(shape):
    zeros = (0,) * len(shape)
    return pl.BlockSpec(shape, lambda *_: zeros, pipeline_mode=pl.Buffered(1))


def _params(n_axes):
    return pltpu.CompilerParams(dimension_semantics=("arbitrary",) * n_axes,
                                vmem_limit_bytes=VMEM_LIMIT_BYTES_V7X)


def _ffn_kernel(h_ref, gpre_ref, w1_ref, w2_ref, gpost_ref, o_ref):
    x = h_ref[...]
    xn = _rms(x, gpre_ref[...]).astype(jnp.bfloat16)
    acc = jnp.zeros(x.shape, jnp.float32)
    for c in range(N_FF_CHUNKS):
        gu = jnp.dot(xn, w1_ref[c], preferred_element_type=jnp.float32)
        gate = gu[:, :FF_CHUNK]
        up = gu[:, FF_CHUNK:]
        act = (gate * (1.0 / (1.0 + jnp.exp(-gate))) * up).astype(jnp.bfloat16)
        acc = acc + jnp.dot(act, w2_ref[c], preferred_element_type=jnp.float32)
    o_ref[...] = x + HALF_STEP * _rms(acc, gpost_ref[...])


def _ffn(h, g_pre, w1c, w2c, g_post):
    n = h.shape[0]
    tm = min(ROW_TILE, n)
    row = pl.BlockSpec((tm, D_MODEL), lambda i: (i, 0))
    return pl.pallas_call(
        _ffn_kernel,
        grid=(n // tm,),
        in_specs=[row, _const_spec((1, D_MODEL)), _const_spec(w1c.shape), _const_spec(w2c.shape),
                  _const_spec((1, D_MODEL))],
        out_specs=row,
        out_shape=jax.ShapeDtypeStruct(h.shape, h.dtype),
        compiler_params=_params(1),
        name="ffn",
    )(h, g_pre, w1c, w2c, g_post)


def _prep_ffn_weights(w1, w2):
    gate = w1[:, :D_FF].reshape(D_MODEL, N_FF_CHUNKS, FF_CHUNK)
    up = w1[:, D_FF:].reshape(D_MODEL, N_FF_CHUNKS, FF_CHUNK)
    w1c = jnp.concatenate([gate, up], axis=-1).transpose(1, 0, 2).astype(jnp.bfloat16)
    w2c = w2.reshape(N_FF_CHUNKS, FF_CHUNK, D_MODEL).astype(jnp.bfloat16)
    return w1c, w2c


def _inproj_kernel(tiles_per_seq, transpose_v, h_ref, g_ref, win_ref, convw_ref, hist_ref,
                   q_ref, k_ref, v_ref, yc_ref, tail_ref, uc_scr):
    i = pl.program_id(0)
    tm = h_ref.shape[0]
    u = _rms(h_ref[...], g_ref[...]).astype(jnp.bfloat16)
    z = jnp.dot(u, win_ref[...], preferred_element_type=jnp.float32)
    q_ref[...] = (z[:, :A_WIDTH] * (HEAD_DIM ** -0.5)).astype(jnp.bfloat16)
    k_ref[...] = z[:, A_WIDTH:A_WIDTH + KV_WIDTH].astype(jnp.bfloat16)
    v = z[:, A_WIDTH + KV_WIDTH:A_WIDTH + 2 * KV_WIDTH]
    v_ref[...] = (v.T if transpose_v else v).astype(jnp.bfloat16)
    o = A_WIDTH + 2 * KV_WIDTH
    b_gate = z[:, o:o + B_WIDTH]
    c_gate = z[:, o + B_WIDTH:o + 2 * B_WIDTH]
    x_in = z[:, o + 2 * B_WIDTH:o + 3 * B_WIDTH]
    uc = c_gate * x_in

    @pl.when(i % tiles_per_seq == 0)
    def _():
        uc_scr[0:CONV_HALO, :] = hist_ref[...]

    uc_scr[CONV_HALO:CONV_HALO + tm, :] = uc
    y = convw_ref[0:1, :] * uc
    for j in range(1, CONV_WIDTH):
        y = y + convw_ref[j:j + 1, :] * uc_scr[CONV_HALO - j:CONV_HALO - j + tm, :]
    yc_ref[...] = (b_gate * y).astype(jnp.bfloat16)
    tail = uc[tm - CONV_HALO:, :]
    tail_ref[...] = tail
    uc_scr[0:CONV_HALO, :] = tail


def _inproj(h, g, w_in, conv_w, hist, rows_per_seq, transpose_v):
    n = h.shape[0]
    tm = min(ROW_TILE, n)
    row = lambda w: pl.BlockSpec((tm, w), lambda i: (i, 0))
    bf = lambda *shape: jax.ShapeDtypeStruct(shape, jnp.bfloat16)
    if transpose_v:
        v_spec, v_shape = pl.BlockSpec((KV_WIDTH, tm), lambda i: (0, i)), bf(KV_WIDTH, n)
    else:
        v_spec, v_shape = row(KV_WIDTH), bf(n, KV_WIDTH)
    return pl.pallas_call(
        functools.partial(_inproj_kernel, rows_per_seq // tm, transpose_v),
        grid=(n // tm,),
        in_specs=[row(D_MODEL), _const_spec((1, D_MODEL)), _const_spec(w_in.shape),
                  _const_spec(conv_w.shape), _const_spec((CONV_HALO, B_WIDTH))],
        out_specs=[row(A_WIDTH), row(KV_WIDTH), v_spec, row(B_WIDTH),
                   pl.BlockSpec((CONV_HALO, B_WIDTH), lambda i: (0, 0))],
        out_shape=[bf(n, A_WIDTH), bf(n, KV_WIDTH), v_shape, bf(n, B_WIDTH),
                   jax.ShapeDtypeStruct((CONV_HALO, B_WIDTH), jnp.float32)],
        scratch_shapes=[pltpu.VMEM((CONV_HALO + tm, B_WIDTH), jnp.float32)],
        compiler_params=_params(1),
        name="inproj_conv",
    )(h, g, w_in, conv_w, hist)


def _bias_from_buckets(bucket, rel_ref, head):
    body = lambda b, acc: jnp.where(bucket == b, rel_ref[b, head], acc)
    acc = jax.lax.fori_loop(0, REL_BUCKETS, body, jnp.zeros(bucket.shape, jnp.float32))
    return jnp.where(bucket >= 0, acc, NEG_INF)


def _attn_kernel(q_ref, kp_ref, kc_ref, vtp_ref, vtc_ref, km_ref, vtm_ref, bktc_ref, bktm_ref,
                 rel_ref, sink_ref, o_ref, bias_comb_scr, bias_meta_scr, sink_scr):
    j = pl.program_id(1)
    key_idx = jax.lax.broadcasted_iota(jnp.int32, (BLOCK, GROUP_LANES), 0)
    qry_idx = jax.lax.broadcasted_iota(jnp.int32, (BLOCK, GROUP_LANES), 1) % BLOCK
    from_prev = key_idx > qry_idx

    @pl.when(jnp.logical_and(pl.program_id(0) == 0, j == 0))
    def _():
        bktc = bktc_ref[...]
        bktm = bktm_ref[...]
        for h in range(A_HEADS):
            g, i = divmod(h, A_GROUP)
            lanes = slice(i * BLOCK, (i + 1) * BLOCK)
            bias = _bias_from_buckets(bktc, rel_ref, h)
            bias_comb_scr[0, g, :, lanes] = bias
            bias_comb_scr[1, g, :, lanes] = jnp.where(from_prev[:, :BLOCK], NEG_INF, bias)
            bias_meta_scr[:, h * BLOCK:(h + 1) * BLOCK] = _bias_from_buckets(bktm, rel_ref, h)
            sink_scr[g, :, lanes] = jnp.full((8, BLOCK), sink_ref[h], jnp.float32)

    kcat = jnp.concatenate([kp_ref[...], kc_ref[...]], axis=0)
    vtcat = jnp.concatenate([vtp_ref[...], vtc_ref[...]], axis=1)
    km = km_ref[...]
    vtm = vtm_ref[...]
    for qb in range(ATTN_QBLOCKS):
        q = q_ref[qb * BLOCK:(qb + 1) * BLOCK, :]
        variant = jnp.where(j == 0, 1, 0) if qb == 0 else 0
        meta_row = pl.multiple_of((j * ATTN_QBLOCKS + qb) * N_META, N_META)
        outs = []
        for g in range(A_KV_HEADS):
            dims = slice(g * HEAD_DIM, (g + 1) * HEAD_DIM)
            q_stack = jnp.concatenate(
                [q[:, (g * A_GROUP + i) * HEAD_DIM:(g * A_GROUP + i + 1) * HEAD_DIM]
                 for i in range(A_GROUP)], axis=0)
            k_all = jnp.concatenate([kcat[qb * BLOCK:(qb + 2) * BLOCK, dims], km[:, dims]], axis=0)
            st = jax.lax.dot_general(k_all, q_stack, _NT, preferred_element_type=jnp.float32)
            comb = jnp.where(from_prev, st[0:BLOCK], st[BLOCK:2 * BLOCK]) + bias_comb_scr[variant, g]
            meta = st[2 * BLOCK:] + bias_meta_scr[pl.ds(meta_row, N_META),
                                                  g * GROUP_LANES:(g + 1) * GROUP_LANES]
            sink = sink_scr[g, 0:1, :]
            m = jnp.maximum(jnp.maximum(jnp.max(comb, axis=0, keepdims=True),
                                        jnp.max(meta, axis=0, keepdims=True)), sink)
            pc = jnp.exp(comb - m)
            pm = jnp.exp(meta - m)
            den = (jnp.sum(pc, axis=0, keepdims=True) + jnp.sum(pm, axis=0, keepdims=True)
                   + jnp.exp(sink - m))
            p_band = jnp.concatenate([jnp.where(from_prev, pc, 0.0), jnp.where(from_prev, 0.0, pc)],
                                     axis=0).astype(jnp.bfloat16)
            ot = jnp.dot(vtcat[dims, qb * BLOCK:(qb + 2) * BLOCK], p_band,
                         preferred_element_type=jnp.float32)
            ot = ot + jnp.dot(vtm[dims, :], pm.astype(jnp.bfloat16), preferred_element_type=jnp.float32)
            ot = ot * (1.0 / den)
            outs += [ot[:, i * BLOCK:(i + 1) * BLOCK] for i in range(A_GROUP)]
        yt = jnp.concatenate(outs, axis=0)
        o_ref[qb * BLOCK:(qb + 1) * BLOCK, :] = yt.T.astype(o_ref.dtype)


def _attention(q, k, vt, k_meta, vt_meta, bkt_comb, bkt_meta, rel_bias, sinks, batch, seq):
    nb = seq // BLOCK
    qb = ATTN_QBLOCKS
    tiles = nb // qb
    cur_rows = lambda w: pl.BlockSpec((qb * BLOCK, w), lambda b, j: (b * tiles + j, 0))
    prev_blk = lambda b, j: b * nb + jnp.maximum(j * qb - 1, 0)
    smem = pl.BlockSpec(memory_space=pltpu.SMEM)
    return pl.pallas_call(
        _attn_kernel,
        grid=(batch, tiles),
        in_specs=[cur_rows(A_WIDTH),
                  pl.BlockSpec((BLOCK, KV_WIDTH), lambda b, j: (prev_blk(b, j), 0)),
                  cur_rows(KV_WIDTH),
                  pl.BlockSpec((KV_WIDTH, BLOCK), lambda b, j: (0, prev_blk(b, j))),
                  pl.BlockSpec((KV_WIDTH, qb * BLOCK), lambda b, j: (0, b * tiles + j)),
                  _const_spec(k_meta.shape), _const_spec(vt_meta.shape),
                  _const_spec(bkt_comb.shape), _const_spec(bkt_meta.shape), smem, smem],
        out_specs=cur_rows(A_WIDTH),
        out_shape=jax.ShapeDtypeStruct(q.shape, jnp.bfloat16),
        scratch_shapes=[pltpu.VMEM((2, A_KV_HEADS, BLOCK, GROUP_LANES), jnp.float32),
                        pltpu.VMEM((nb * N_META, A_HEADS * BLOCK), jnp.float32),
                        pltpu.VMEM((A_KV_HEADS, 8, GROUP_LANES), jnp.float32)],
        compiler_params=_params(2),
        name="swa_attention",
    )(q, k, k, vt, vt, k_meta, vt_meta, bkt_comb, bkt_meta, rel_bias, sinks)


def _attn_meta_kernel(q_ref, k_ref, v_ref, bkt_ref, rel_ref, sink_ref, o_ref):
    q = q_ref[...]
    k = k_ref[...]
    v = v_ref[...]
    bkt = bkt_ref[...]
    outs = []
    for h in range(A_HEADS):
        dims = slice((h // A_GROUP) * HEAD_DIM, (h // A_GROUP + 1) * HEAD_DIM)
        s = jax.lax.dot_general(q[:, h * HEAD_DIM:(h + 1) * HEAD_DIM], k[:, dims], _NT,
                                preferred_element_type=jnp.float32)
        s = s + _bias_from_buckets(bkt, rel_ref, h)
        sink = sink_ref[h]
        m = jnp.maximum(jnp.max(s, axis=-1, keepdims=True), sink)
        p = jnp.exp(s - m)
        den = jnp.sum(p, axis=-1, keepdims=True) + jnp.exp(sink - m)
        o = jnp.dot(p.astype(jnp.bfloat16), v[:, dims], preferred_element_type=jnp.float32)
        outs.append(o * (1.0 / den))
    o_ref[...] = jnp.concatenate(outs, axis=-1).astype(o_ref.dtype)


def _attention_meta(q, k, v, bkt_self, rel_bias, sinks):
    vmem = pl.BlockSpec(memory_space=pltpu.VMEM)
    smem = pl.BlockSpec(memory_space=pltpu.SMEM)
    return pl.pallas_call(
        _attn_meta_kernel,
        in_specs=[vmem, vmem, vmem, vmem, smem, smem],
        out_specs=vmem,
        out_shape=jax.ShapeDtypeStruct(q.shape, jnp.bfloat16),
        name="meta_attention",
    )(q, k, v, bkt_self, rel_bias, sinks)


def _t5_bucket(dist):
    n = jnp.maximum(dist, 0)
    max_exact = REL_BUCKETS // 2
    nf = jnp.maximum(n, 1).astype(jnp.float32)
    large = max_exact + (jnp.log(nf / max_exact) / math.log(REL_MAX_DIST / max_exact)
                         * (REL_BUCKETS - max_exact)).astype(jnp.int32)
    large = jnp.minimum(large, REL_BUCKETS - 1)
    return jnp.where(n < max_exact, n, large)


def _bucket_tables(seq):
    bucket = lambda dist: jnp.where(dist >= 0, _t5_bucket(dist), -1).astype(jnp.int32)
    idx = jnp.arange(BLOCK)
    comb = bucket((idx[None, :] - idx[:, None]) % BLOCK)
    m = jnp.arange(N_META)
    pos = N_META + jnp.arange(seq).reshape(seq // BLOCK, 1, BLOCK)
    meta = bucket(pos - m[None, :, None]).reshape(seq // BLOCK * N_META, BLOCK)
    self_ = bucket(m[:, None] - m[None, :])
    return comb, meta, self_


def _outproj_kernel(h_ref, ya_ref, yc_ref, woa_ref, woc_ref, g_ref, o_ref):
    mix = jnp.dot(ya_ref[...], woa_ref[...], preferred_element_type=jnp.float32)
    mix = mix + jnp.dot(yc_ref[...], woc_ref[...], preferred_element_type=jnp.float32)
    o_ref[...] = h_ref[...] + _rms(mix, g_ref[...])


def _outproj(h, ya, yc, wo_a, wo_c, g):
    n = h.shape[0]
    tm = min(ROW_TILE, n)
    row = lambda w: pl.BlockSpec((tm, w), lambda i: (i, 0))
    return pl.pallas_call(
        _outproj_kernel,
        grid=(n // tm,),
        in_specs=[row(D_MODEL), row(A_WIDTH), row(B_WIDTH), _const_spec(wo_a.shape),
                  _const_spec(wo_c.shape), _const_spec((1, D_MODEL))],
        out_specs=row(D_MODEL),
        out_shape=jax.ShapeDtypeStruct(h.shape, h.dtype),
        compiler_params=_params(1),
        name="outproj",
    )(h, ya, yc, wo_a, wo_c, g)


def _pool_kernel(tiles_per_seq, pos0, h_ref, prev_ref, first_ref, gpre_ref, pw_ref, ps_ref, gpost_ref,
                 o_ref, u_scr):
    j = pl.program_id(0) % tiles_per_seq
    tm = h_ref.shape[0]
    x = h_ref[...]
    gpre = gpre_ref[...]
    hist = jnp.where(j == 0, first_ref[...], prev_ref[...])
    u = _rms(x, gpre)
    u_scr[0:POOL_HALO, :] = _rms(hist, gpre)
    u_scr[POOL_HALO:POOL_HALO + tm, :] = u
    t = pos0 + j * tm + jax.lax.broadcasted_iota(jnp.int32, (tm, 1), 0)
    ys = []
    for gi, w in enumerate(POOL_SIZES):
        cols = slice(gi * POOL_GROUP_DIM, (gi + 1) * POOL_GROUP_DIM)
        ug = u[:, cols]
        s = ug
        for back in range(1, w):
            s = s + u_scr[POOL_HALO - back:POOL_HALO - back + tm, cols]
        cnt = jnp.minimum(t + 1, w).astype(jnp.float32)
        d = (s / cnt - ug).astype(jnp.bfloat16)
        ys.append(jnp.dot(d, pw_ref[gi], preferred_element_type=jnp.float32))
    y = jnp.concatenate(ys, axis=-1) * ps_ref[...]
    o_ref[...] = x + _rms(y, gpost_ref[...])


def _pool(h, first_hist, g_pre, pool_w, pool_scale, g_post, rows_per_seq, pos0):
    n = h.shape[0]
    tm = min(POOL_TILE, n)
    halo_blocks = tm // POOL_HALO
    row = pl.BlockSpec((tm, D_MODEL), lambda i: (i, 0))
    prev = pl.BlockSpec((POOL_HALO, D_MODEL), lambda i: (jnp.maximum(i * halo_blocks - 1, 0), 0))
    return pl.pallas_call(
        functools.partial(_pool_kernel, rows_per_seq // tm, pos0),
        grid=(n // tm,),
        in_specs=[row, prev, _const_spec((POOL_HALO, D_MODEL)), _const_spec((1, D_MODEL)),
                  _const_spec(pool_w.shape), _const_spec((1, D_MODEL)), _const_spec((1, D_MODEL))],
        out_specs=row,
        out_shape=jax.ShapeDtypeStruct(h.shape, h.dtype),
        scratch_shapes=[pltpu.VMEM((POOL_HALO + tm, D_MODEL), jnp.float32)],
        compiler_params=_params(1),
        name="pool_mixer",
    )(h, h, first_hist, g_pre, pool_w, pool_scale, g_post)


def kernel(x, meta_tokens, rel_bias, norm_g, ffn_w1, ffn_w2, mix_w_in, mix_w_out, attn_sinks,
           conv_w, pool_w, pool_scale):
    batch, seq, _ = x.shape
    depth = norm_g.shape[0]
    h = x.reshape(batch * seq, D_MODEL)
    hm = meta_tokens.astype(x.dtype)
    bkt_comb, bkt_meta, bkt_self = _bucket_tables(seq)
    rel = rel_bias.astype(jnp.float32)
    for layer in range(depth):
        g = norm_g[layer].astype(jnp.float32).reshape(6, 1, D_MODEL)
        last = layer == depth - 1
        mix_is_attn = layer % 2 == 0
        w1c, w2c = _prep_ffn_weights(ffn_w1[layer, 0], ffn_w2[layer, 0])
        h = _ffn(h, g[0], w1c, w2c, g[1])
        hm = _ffn(hm, g[0], w1c, w2c, g[1])
        if mix_is_attn:
            e = layer // 2
            w_in = mix_w_in[e].astype(jnp.bfloat16)
            wo_a = mix_w_out[e, :A_WIDTH].astype(jnp.bfloat16)
            wo_c = mix_w_out[e, A_WIDTH:].astype(jnp.bfloat16)
            cw = conv_w[e].astype(jnp.float32)
            sinks = attn_sinks[e].astype(jnp.float32)
            no_hist = jnp.zeros((CONV_HALO, B_WIDTH), jnp.float32)
            qm, km, vm, ycm, tail_m = _inproj(hm, g[2], w_in, cw, no_hist, N_META, False)
            q, k, vt, yc, _ = _inproj(h, g[2], w_in, cw, tail_m, seq, True)
            ya = _attention(q, k, vt, km, vm.T, bkt_comb, bkt_meta, rel, sinks, batch, seq)
            h = _outproj(h, ya, yc, wo_a, wo_c, g[3])
            if not last:
                yam = _attention_meta(qm, km, vm, bkt_self, rel, sinks)
                hm = _outproj(hm, yam, ycm, wo_a, wo_c, g[3])
        else:
            o = layer // 2
            pw = pool_w[o].astype(jnp.bfloat16)
            ps = pool_scale[o].astype(jnp.float32).reshape(1, D_MODEL)
            h_new = _pool(h, hm, g[2], pw, ps, g[3], seq, N_META)
            if not last:
                hm = _pool(hm, jnp.zeros((POOL_HALO, D_MODEL), hm.dtype), g[2], pw, ps, g[3], N_META, 0)
            h = h_new
        w1c, w2c = _prep_ffn_weights(ffn_w1[layer, 1], ffn_w2[layer, 1])
        h = _ffn(h, g[4], w1c, w2c, g[5])
        if not last:
            hm = _ffn(hm, g[4], w1c, w2c, g[5])
    return h.reshape(batch, seq, D_MODEL)
```

```python
import functools
import math

import jax
import jax.numpy as jnp
from jax.experimental import pallas as pl
from jax.experimental.pallas import tpu as pltpu

D_MODEL = 1024
N_META = 16
A_HEADS = 8
A_KV_HEADS = 2
A_GROUP = A_HEADS // A_KV_HEADS
HEAD_DIM = 64
WINDOW = 128
BLOCK = 128
A_WIDTH = A_HEADS * HEAD_DIM
KV_WIDTH = A_KV_HEADS * HEAD_DIM
B_WIDTH = D_MODEL // 2
CONV_WIDTH = 3
IN_WIDTH = A_WIDTH + 2 * KV_WIDTH + 3 * B_WIDTH
POOL_SIZES = (2, 4, 8, 16)
N_POOL_GROUPS = 4
POOL_GROUP_DIM = D_MODEL // N_POOL_GROUPS
REL_BUCKETS = 32
REL_MAX_DIST = 128
D_FF = 2816
HALF_STEP = 0.5
RMS_EPS = 1e-6

VMEM_LIMIT_BYTES_V7X = 56 * 1024 * 1024
ROW_TILE = 512
FF_CHUNK = 256
N_FF_CHUNKS = D_FF // FF_CHUNK
POOL_TILE = 512
POOL_HALO = 16
CONV_HALO = 8
ATTN_QBLOCKS = 4
GROUP_LANES = A_GROUP * BLOCK

NEG_INF = float("-inf")
_NT = (((1,), (1,)), ((), ()))

assert WINDOW == BLOCK and N_META == POOL_HALO


def _rms(x, g):
    return x * jax.lax.rsqrt(jnp.mean(x * x, axis=-1, keepdims=True) + RMS_EPS) * g


def _const_spec(shape):
    zeros = (0,) * len(shape)
    return pl.BlockSpec(shape, lambda *_: zeros, pipeline_mode=pl.Buffered(1))


def _slab_spec(arr, *lead):
    rest = arr.shape[len(lead):]
    idx = tuple(lead) + (0,) * len(rest)
    return pl.BlockSpec((None,) * len(lead) + rest, lambda *_: idx, pipeline_mode=pl.Buffered(1))


def _mm(a, w):
    return jax.lax.dot_general(a, w, (((1,), (0,)), ((), ())), preferred_element_type=jnp.float32)


def _params(n_axes):
    return pltpu.CompilerParams(dimension_semantics=("arbitrary",) * n_axes,
                                vmem_limit_bytes=VMEM_LIMIT_BYTES_V7X)


def _ffn_kernel(pre, post, h_ref, g_ref, w1_ref, w2_ref, o_ref):
    x = h_ref[...]
    xn = _rms(x, g_ref[pre:pre + 1, :]).astype(jnp.bfloat16)
    acc = jnp.zeros(x.shape, jnp.float32)
    for c in range(N_FF_CHUNKS):
        cols = slice(c * FF_CHUNK, (c + 1) * FF_CHUNK)
        gate = _mm(xn, w1_ref[:, cols])
        up = _mm(xn, w1_ref[:, D_FF + c * FF_CHUNK:D_FF + (c + 1) * FF_CHUNK])
        act = (gate * (1.0 / (1.0 + jnp.exp(-gate))) * up).astype(jnp.bfloat16)
        acc = acc + _mm(act, w2_ref[cols, :])
    o_ref[...] = x + HALF_STEP * _rms(acc, g_ref[post:post + 1, :])


def _ffn(h, norm_g, ffn_w1, ffn_w2, layer, half):
    n = h.shape[0]
    tm = min(ROW_TILE, n)
    row = pl.BlockSpec((tm, D_MODEL), lambda i: (i, 0))
    return pl.pallas_call(
        functools.partial(_ffn_kernel, 4 * half, 4 * half + 1),
        grid=(n // tm,),
        in_specs=[row, _slab_spec(norm_g, layer), _slab_spec(ffn_w1, layer, half),
                  _slab_spec(ffn_w2, layer, half)],
        out_specs=row,
        out_shape=jax.ShapeDtypeStruct(h.shape, h.dtype),
        compiler_params=_params(1),
        name="ffn",
    )(h, norm_g, ffn_w1, ffn_w2)


def _inproj_kernel(tiles_per_seq, transpose_v, h_ref, g_ref, win_ref, convw_ref, hist_ref,
                   q_ref, k_ref, v_ref, yc_ref, tail_ref, uc_scr):
    i = pl.program_id(0)
    tm = h_ref.shape[0]
    u = _rms(h_ref[...], g_ref[2:3, :]).astype(jnp.bfloat16)
    z = _mm(u, win_ref[...])
    q_ref[...] = (z[:, :A_WIDTH] * (HEAD_DIM ** -0.5)).astype(jnp.bfloat16)
    k_ref[...] = z[:, A_WIDTH:A_WIDTH + KV_WIDTH].astype(jnp.bfloat16)
    v = z[:, A_WIDTH + KV_WIDTH:A_WIDTH + 2 * KV_WIDTH]
    v_ref[...] = (v.T if transpose_v else v).astype(jnp.bfloat16)
    o = A_WIDTH + 2 * KV_WIDTH
    b_gate = z[:, o:o + B_WIDTH]
    c_gate = z[:, o + B_WIDTH:o + 2 * B_WIDTH]
    x_in = z[:, o + 2 * B_WIDTH:o + 3 * B_WIDTH]
    uc = c_gate * x_in

    @pl.when(i % tiles_per_seq == 0)
    def _():
        uc_scr[0:CONV_HALO, :] = hist_ref[...]

    uc_scr[CONV_HALO:CONV_HALO + tm, :] = uc
    y = convw_ref[0:1, :] * uc
    for j in range(1, CONV_WIDTH):
        y = y + convw_ref[j:j + 1, :] * uc_scr[CONV_HALO - j:CONV_HALO - j + tm, :]
    yc_ref[...] = (b_gate * y).astype(jnp.bfloat16)
    tail = uc[tm - CONV_HALO:, :]
    tail_ref[...] = tail
    uc_scr[0:CONV_HALO, :] = tail


def _inproj(h, norm_g, mix_w_in, conv_w, hist, layer, e, rows_per_seq, transpose_v):
    n = h.shape[0]
    tm = min(ROW_TILE, n)
    row = lambda w: pl.BlockSpec((tm, w), lambda i: (i, 0))
    bf = lambda *shape: jax.ShapeDtypeStruct(shape, jnp.bfloat16)
    if transpose_v:
        v_spec, v_shape = pl.BlockSpec((KV_WIDTH, tm), lambda i: (0, i)), bf(KV_WIDTH, n)
    else:
        v_spec, v_shape = row(KV_WIDTH), bf(n, KV_WIDTH)
    return pl.pallas_call(
        functools.partial(_inproj_kernel, rows_per_seq // tm, transpose_v),
        grid=(n // tm,),
        in_specs=[row(D_MODEL), _slab_spec(norm_g, layer), _slab_spec(mix_w_in, e),
                  _slab_spec(conv_w, e), _const_spec((CONV_HALO, B_WIDTH))],
        out_specs=[row(A_WIDTH), row(KV_WIDTH), v_spec, row(B_WIDTH),
                   pl.BlockSpec((CONV_HALO, B_WIDTH), lambda i: (0, 0))],
        out_shape=[bf(n, A_WIDTH), bf(n, KV_WIDTH), v_shape, bf(n, B_WIDTH),
                   jax.ShapeDtypeStruct((CONV_HALO, B_WIDTH), jnp.float32)],
        scratch_shapes=[pltpu.VMEM((CONV_HALO + tm, B_WIDTH), jnp.float32)],
        compiler_params=_params(1),
        name="inproj_conv",
    )(h, norm_g, mix_w_in, conv_w, hist)


def _bias_from_buckets(bucket, rel_ref, head):
    body = lambda b, acc: jnp.where(bucket == b, rel_ref[b, head], acc)
    acc = jax.lax.fori_loop(0, REL_BUCKETS, body, jnp.zeros(bucket.shape, jnp.float32))
    return jnp.where(bucket >= 0, acc, NEG_INF)


def _attn_kernel(e, q_ref, kp_ref, kc_ref, vtp_ref, vtc_ref, km_ref, vtm_ref, bktc_ref, bktm_ref,
                 rel_ref, sink_ref, o_ref, bias_comb_scr, bias_meta_scr, sink_scr):
    j = pl.program_id(1)
    key_idx = jax.lax.broadcasted_iota(jnp.int32, (BLOCK, GROUP_LANES), 0)
    qry_idx = jax.lax.broadcasted_iota(jnp.int32, (BLOCK, GROUP_LANES), 1) % BLOCK
    from_prev = key_idx > qry_idx

    @pl.when(jnp.logical_and(pl.program_id(0) == 0, j == 0))
    def _():
        bktc = bktc_ref[...]
        bktm = bktm_ref[...]
        for h in range(A_HEADS):
            g, i = divmod(h, A_GROUP)
            lanes = slice(i * BLOCK, (i + 1) * BLOCK)
            bias = _bias_from_buckets(bktc, rel_ref, h)
            bias_comb_scr[0, g, :, lanes] = bias
            bias_comb_scr[1, g, :, lanes] = jnp.where(from_prev[:, :BLOCK], NEG_INF, bias)
            bias_meta_scr[:, h * BLOCK:(h + 1) * BLOCK] = _bias_from_buckets(bktm, rel_ref, h)
            sink_scr[g, :, lanes] = jnp.full((8, BLOCK), sink_ref[e, h], jnp.float32)

    kcat = jnp.concatenate([kp_ref[...], kc_ref[...]], axis=0)
    vtcat = jnp.concatenate([vtp_ref[...], vtc_ref[...]], axis=1)
    km = km_ref[...]
    vtm = vtm_ref[...]
    for qb in range(ATTN_QBLOCKS):
        q = q_ref[qb * BLOCK:(qb + 1) * BLOCK, :]
        variant = jnp.where(j == 0, 1, 0) if qb == 0 else 0
        meta_row = pl.multiple_of((j * ATTN_QBLOCKS + qb) * N_META, N_META)
        outs = []
        for g in range(A_KV_HEADS):
            dims = slice(g * HEAD_DIM, (g + 1) * HEAD_DIM)
            q_stack = jnp.concatenate(
                [q[:, (g * A_GROUP + i) * HEAD_DIM:(g * A_GROUP + i + 1) * HEAD_DIM]
                 for i in range(A_GROUP)], axis=0)
            k_all = jnp.concatenate([kcat[qb * BLOCK:(qb + 2) * BLOCK, dims], km[:, dims]], axis=0)
            st = jax.lax.dot_general(k_all, q_stack, _NT, preferred_element_type=jnp.float32)
            comb = jnp.where(from_prev, st[0:BLOCK], st[BLOCK:2 * BLOCK]) + bias_comb_scr[variant, g]
            meta = st[2 * BLOCK:] + bias_meta_scr[pl.ds(meta_row, N_META),
                                                  g * GROUP_LANES:(g + 1) * GROUP_LANES]
            sink = sink_scr[g, 0:1, :]
            m = jnp.maximum(jnp.maximum(jnp.max(comb, axis=0, keepdims=True),
                                        jnp.max(meta, axis=0, keepdims=True)), sink)
            pc = jnp.exp(comb - m)
            pm = jnp.exp(meta - m)
            den = (jnp.sum(pc, axis=0, keepdims=True) + jnp.sum(pm, axis=0, keepdims=True)
                   + jnp.exp(sink - m))
            p_band = jnp.concatenate([jnp.where(from_prev, pc, 0.0), jnp.where(from_prev, 0.0, pc)],
                                     axis=0).astype(jnp.bfloat16)
            ot = jnp.dot(vtcat[dims, qb * BLOCK:(qb + 2) * BLOCK], p_band,
                         preferred_element_type=jnp.float32)
            ot = ot + jnp.dot(vtm[dims, :], pm.astype(jnp.bfloat16), preferred_element_type=jnp.float32)
            ot = ot * (1.0 / den)
            outs += [ot[:, i * BLOCK:(i + 1) * BLOCK] for i in range(A_GROUP)]
        yt = jnp.concatenate(outs, axis=0)
        o_ref[qb * BLOCK:(qb + 1) * BLOCK, :] = yt.T.astype(o_ref.dtype)


def _attention(q, k, vt, k_meta, vt_meta, bkt_comb, bkt_meta, rel_bias, sinks, e, batch, seq):
    nb = seq // BLOCK
    qb = ATTN_QBLOCKS
    tiles = nb // qb
    cur_rows = lambda w: pl.BlockSpec((qb * BLOCK, w), lambda b, j: (b * tiles + j, 0))
    prev_blk = lambda b, j: b * nb + jnp.maximum(j * qb - 1, 0)
    smem = pl.BlockSpec(memory_space=pltpu.SMEM)
    return pl.pallas_call(
        functools.partial(_attn_kernel, e),
        grid=(batch, tiles),
        in_specs=[cur_rows(A_WIDTH),
                  pl.BlockSpec((BLOCK, KV_WIDTH), lambda b, j: (prev_blk(b, j), 0)),
                  cur_rows(KV_WIDTH),
                  pl.BlockSpec((KV_WIDTH, BLOCK), lambda b, j: (0, prev_blk(b, j))),
                  pl.BlockSpec((KV_WIDTH, qb * BLOCK), lambda b, j: (0, b * tiles + j)),
                  _const_spec(k_meta.shape), _const_spec(vt_meta.shape),
                  _const_spec(bkt_comb.shape), _const_spec(bkt_meta.shape), smem, smem],
        out_specs=cur_rows(A_WIDTH),
        out_shape=jax.ShapeDtypeStruct(q.shape, jnp.bfloat16),
        scratch_shapes=[pltpu.VMEM((2, A_KV_HEADS, BLOCK, GROUP_LANES), jnp.float32),
                        pltpu.VMEM((nb * N_META, A_HEADS * BLOCK), jnp.float32),
                        pltpu.VMEM((A_KV_HEADS, 8, GROUP_LANES), jnp.float32)],
        compiler_params=_params(2),
        name="swa_attention",
    )(q, k, k, vt, vt, k_meta, vt_meta, bkt_comb, bkt_meta, rel_bias, sinks)


def _attn_meta_kernel(e, q_ref, k_ref, v_ref, bkt_ref, rel_ref, sink_ref, o_ref):
    q = q_ref[...]
    k = k_ref[...]
    v = v_ref[...]
    bkt = bkt_ref[...]
    outs = []
    for h in range(A_HEADS):
        dims = slice((h // A_GROUP) * HEAD_DIM, (h // A_GROUP + 1) * HEAD_DIM)
        s = jax.lax.dot_general(q[:, h * HEAD_DIM:(h + 1) * HEAD_DIM], k[:, dims], _NT,
                                preferred_element_type=jnp.float32)
        s = s + _bias_from_buckets(bkt, rel_ref, h)
        sink = sink_ref[e, h]
        m = jnp.maximum(jnp.max(s, axis=-1, keepdims=True), sink)
        p = jnp.exp(s - m)
        den = jnp.sum(p, axis=-1, keepdims=True) + jnp.exp(sink - m)
        o = jnp.dot(p.astype(jnp.bfloat16), v[:, dims], preferred_element_type=jnp.float32)
        outs.append(o * (1.0 / den))
    o_ref[...] = jnp.concatenate(outs, axis=-1).astype(o_ref.dtype)


def _attention_meta(q, k, v, bkt_self, rel_bias, sinks, e):
    vmem = pl.BlockSpec(memory_space=pltpu.VMEM)
    smem = pl.BlockSpec(memory_space=pltpu.SMEM)
    return pl.pallas_call(
        functools.partial(_attn_meta_kernel, e),
        in_specs=[vmem, vmem, vmem, vmem, smem, smem],
        out_specs=vmem,
        out_shape=jax.ShapeDtypeStruct(q.shape, jnp.bfloat16),
        name="meta_attention",
    )(q, k, v, bkt_self, rel_bias, sinks)


def _t5_bucket(dist):
    n = jnp.maximum(dist, 0)
    max_exact = REL_BUCKETS // 2
    nf = jnp.maximum(n, 1).astype(jnp.float32)
    large = max_exact + (jnp.log(nf / max_exact) / math.log(REL_MAX_DIST / max_exact)
                         * (REL_BUCKETS - max_exact)).astype(jnp.int32)
    large = jnp.minimum(large, REL_BUCKETS - 1)
    return jnp.where(n < max_exact, n, large)


def _bucket_tables(seq):
    bucket = lambda dist: jnp.where(dist >= 0, _t5_bucket(dist), -1).astype(jnp.int32)
    idx = jnp.arange(BLOCK)
    comb = bucket((idx[None, :] - idx[:, None]) % BLOCK)
    m = jnp.arange(N_META)
    pos = N_META + jnp.arange(seq).reshape(seq // BLOCK, 1, BLOCK)
    meta = bucket(pos - m[None, :, None]).reshape(seq // BLOCK * N_META, BLOCK)
    self_ = bucket(m[:, None] - m[None, :])
    return comb, meta, self_


def _outproj_kernel(h_ref, ya_ref, yc_ref, wo_ref, g_ref, o_ref):
    mix = _mm(ya_ref[...], wo_ref[0:A_WIDTH, :]) + _mm(yc_ref[...], wo_ref[A_WIDTH:, :])
    o_ref[...] = h_ref[...] + _rms(mix, g_ref[3:4, :])


def _outproj(h, ya, yc, mix_w_out, norm_g, layer, e):
    n = h.shape[0]
    tm = min(ROW_TILE, n)
    row = lambda w: pl.BlockSpec((tm, w), lambda i: (i, 0))
    return pl.pallas_call(
        _outproj_kernel,
        grid=(n // tm,),
        in_specs=[row(D_MODEL), row(A_WIDTH), row(B_WIDTH), _slab_spec(mix_w_out, e),
                  _slab_spec(norm_g, layer)],
        out_specs=row(D_MODEL),
        out_shape=jax.ShapeDtypeStruct(h.shape, h.dtype),
        compiler_params=_params(1),
        name="outproj",
    )(h, ya, yc, mix_w_out, norm_g)


def _pool_kernel(tiles_per_seq, pos0, h_ref, prev_ref, first_ref, g_ref, pw_ref, ps_ref, o_ref, u_scr):
    j = pl.program_id(0) % tiles_per_seq
    tm = h_ref.shape[0]
    x = h_ref[...]
    gpre = g_ref[2:3, :]
    hist = jnp.where(j == 0, first_ref[...], prev_ref[...])
    u = _rms(x, gpre)
    u_scr[0:POOL_HALO, :] = _rms(hist, gpre)
    u_scr[POOL_HALO:POOL_HALO + tm, :] = u
    t = pos0 + j * tm + jax.lax.broadcasted_iota(jnp.int32, (tm, 1), 0)
    ys = []
    for gi, w in enumerate(POOL_SIZES):
        cols = slice(gi * POOL_GROUP_DIM, (gi + 1) * POOL_GROUP_DIM)
        ug = u[:, cols]
        s = ug
        for back in range(1, w):
            s = s + u_scr[POOL_HALO - back:POOL_HALO - back + tm, cols]
        cnt = jnp.minimum(t + 1, w).astype(jnp.float32)
        d = (s / cnt - ug).astype(jnp.bfloat16)
        ys.append(_mm(d, pw_ref[gi]))
    y = jnp.concatenate(ys, axis=-1) * ps_ref[...]
    o_ref[...] = x + _rms(y, g_ref[3:4, :])


def _pool(h, first_hist, norm_g, pool_w, pool_scale, layer, o, rows_per_seq, pos0):
    n = h.shape[0]
    tm = min(POOL_TILE, n)
    halo_blocks = tm // POOL_HALO
    row = pl.BlockSpec((tm, D_MODEL), lambda i: (i, 0))
    prev = pl.BlockSpec((POOL_HALO, D_MODEL), lambda i: (jnp.maximum(i * halo_blocks - 1, 0), 0))
    return pl.pallas_call(
        functools.partial(_pool_kernel, rows_per_seq // tm, pos0),
        grid=(n // tm,),
        in_specs=[row, prev, _const_spec((POOL_HALO, D_MODEL)), _slab_spec(norm_g, layer),
                  _slab_spec(pool_w, o),
                  pl.BlockSpec((1, D_MODEL), lambda i: (o, 0), pipeline_mode=pl.Buffered(1))],
        out_specs=row,
        out_shape=jax.ShapeDtypeStruct(h.shape, h.dtype),
        scratch_shapes=[pltpu.VMEM((POOL_HALO + tm, D_MODEL), jnp.float32)],
        compiler_params=_params(1),
        name="pool_mixer",
    )(h, h, first_hist, norm_g, pool_w, pool_scale)


def kernel(x, meta_tokens, rel_bias, norm_g, ffn_w1, ffn_w2, mix_w_in, mix_w_out, attn_sinks,
           conv_w, pool_w, pool_scale):
    batch, seq, _ = x.shape
    depth = norm_g.shape[0]
    h = x.reshape(batch * seq, D_MODEL)
    hm = meta_tokens.astype(x.dtype)
    bkt_comb, bkt_meta, bkt_self = _bucket_tables(seq)
    for layer in range(depth):
        last = layer == depth - 1
        h = _ffn(h, norm_g, ffn_w1, ffn_w2, layer, 0)
        hm = _ffn(hm, norm_g, ffn_w1, ffn_w2, layer, 0)
        if layer % 2 == 0:
            e = layer // 2
            no_hist = jnp.zeros((CONV_HALO, B_WIDTH), jnp.float32)
            qm, km, vm, ycm, tail_m = _inproj(hm, norm_g, mix_w_in, conv_w, no_hist, layer, e, N_META, False)
            q, k, vt, yc, _ = _inproj(h, norm_g, mix_w_in, conv_w, tail_m, layer, e, seq, True)
            ya = _attention(q, k, vt, km, vm.T, bkt_comb, bkt_meta, rel_bias, attn_sinks, e, batch, seq)
            h = _outproj(h, ya, yc, mix_w_out, norm_g, layer, e)
            if not last:
                yam = _attention_meta(qm, km, vm, bkt_self, rel_bias, attn_sinks, e)
                hm = _outproj(hm, yam, ycm, mix_w_out, norm_g, layer, e)
        else:
            o = layer // 2
            h_new = _pool(h, hm, norm_g, pool_w, pool_scale, layer, o, seq, N_META)
            if not last:
                hm = _pool(hm, jnp.zeros((POOL_HALO, D_MODEL), hm.dtype), norm_g, pool_w, pool_scale,
                           layer, o, N_META, 0)
            h = h_new
        h = _ffn(h, norm_g, ffn_w1, ffn_w2, layer, 1)
        if not last:
            hm = _ffn(hm, norm_g, ffn_w1, ffn_w2, layer, 1)
    return h.reshape(batch, seq, D_MODEL)
```

```python
import functools
import math

import jax
import jax.numpy as jnp
from jax.experimental import pallas as pl
from jax.experimental.pallas import tpu as pltpu

D_MODEL = 1024
N_META = 16
A_HEADS = 8
A_KV_HEADS = 2
A_GROUP = A_HEADS // A_KV_HEADS
HEAD_DIM = 64
WINDOW = 128
BLOCK = 128
A_WIDTH = A_HEADS * HEAD_DIM
KV_WIDTH = A_KV_HEADS * HEAD_DIM
B_WIDTH = D_MODEL // 2
CONV_WIDTH = 3
IN_WIDTH = A_WIDTH + 2 * KV_WIDTH + 3 * B_WIDTH
POOL_SIZES = (2, 4, 8, 16)
N_POOL_GROUPS = 4
POOL_GROUP_DIM = D_MODEL // N_POOL_GROUPS
REL_BUCKETS = 32
REL_MAX_DIST = 128
D_FF = 2816
HALF_STEP = 0.5
RMS_EPS = 1e-6

VMEM_LIMIT_BYTES_V7X = 56 * 1024 * 1024
ROW_TILE = 512
FF_CHUNK = 256
N_FF_CHUNKS = D_FF // FF_CHUNK
W1_STAGE_ROWS = 128
W2_STAGE_ROWS = 256
POOL_TILE = 512
POOL_HALO = 16
CONV_HALO = 8
ATTN_QBLOCKS = 4
GROUP_LANES = A_GROUP * BLOCK

NEG_INF = float("-inf")
_NT = (((1,), (1,)), ((), ()))

assert WINDOW == BLOCK and N_META == POOL_HALO


def _rms(x, g):
    return x * jax.lax.rsqrt(jnp.mean(x * x, axis=-1, keepdims=True) + RMS_EPS) * g


def _const_spec(shape):
    zeros = (0,) * len(shape)
    return pl.BlockSpec(shape, lambda *_: zeros, pipeline_mode=pl.Buffered(1))


def _slab_spec(arr, *lead):
    rest = arr.shape[len(lead):]
    idx = tuple(lead) + (0,) * len(rest)
    return pl.BlockSpec((None,) * len(lead) + rest, lambda *_: idx, pipeline_mode=pl.Buffered(1))


def _mm(a, w):
    return jax.lax.dot_general(a, w, (((1,), (0,)), ((), ())), preferred_element_type=jnp.float32)


def _params(n_axes):
    return pltpu.CompilerParams(dimension_semantics=("arbitrary",) * n_axes,
                                vmem_limit_bytes=VMEM_LIMIT_BYTES_V7X)


def _load_bf16(src, dst, stage, sem, chunk):
    rows = src.shape[0]
    copy = lambda k: pltpu.make_async_copy(src.at[pl.ds(k * chunk, chunk), :], stage.at[k % 2], sem.at[k % 2])
    n = rows // chunk
    copy(0).start()
    for k in range(n):
        if k + 1 < n:
            copy(k + 1).start()
        copy(k).wait()
        dst[k * chunk:(k + 1) * chunk, :] = stage[k % 2].astype(dst.dtype)


def _ffn_rows(x, g_ref, pre, post, w1, w2):
    xn = _rms(x, g_ref[pre:pre + 1, :]).astype(jnp.bfloat16)
    acc = jnp.zeros(x.shape, jnp.float32)
    for c in range(N_FF_CHUNKS):
        cols = slice(c * FF_CHUNK, (c + 1) * FF_CHUNK)
        gate = _mm(xn, w1[:, cols])
        up = _mm(xn, w1[:, D_FF + c * FF_CHUNK:D_FF + (c + 1) * FF_CHUNK])
        act = (gate * (1.0 / (1.0 + jnp.exp(-gate))) * up).astype(jnp.bfloat16)
        acc = acc + _mm(act, w2[cols, :])
    return x + HALF_STEP * _rms(acc, g_ref[post:post + 1, :])


def _ffn_kernel(layer, half, with_meta, h_ref, hm_ref, g_ref, w1_hbm, w2_hbm, o_ref, om_ref,
                w1, w2, stage1, stage2, sem1, sem2):
    pre, post = 4 * half, 4 * half + 1

    @pl.when(pl.program_id(0) == 0)
    def _():
        _load_bf16(w1_hbm.at[layer, half], w1, stage1, sem1, W1_STAGE_ROWS)
        _load_bf16(w2_hbm.at[layer, half], w2, stage2, sem2, W2_STAGE_ROWS)
        if with_meta:
            om_ref[...] = _ffn_rows(hm_ref[...], g_ref, pre, post, w1, w2)
        else:
            om_ref[...] = hm_ref[...]

    o_ref[...] = _ffn_rows(h_ref[...], g_ref, pre, post, w1, w2)


def _ffn(h, hm, norm_g, ffn_w1, ffn_w2, layer, half, with_meta):
    n = h.shape[0]
    tm = ROW_TILE
    row = pl.BlockSpec((tm, D_MODEL), lambda i: (i, 0))
    any_spec = pl.BlockSpec(memory_space=pl.ANY)
    return pl.pallas_call(
        functools.partial(_ffn_kernel, layer, half, with_meta),
        grid=(n // tm,),
        in_specs=[row, _const_spec(hm.shape), _slab_spec(norm_g, layer), any_spec, any_spec],
        out_specs=[row, pl.BlockSpec(hm.shape, lambda i: (0, 0))],
        out_shape=[jax.ShapeDtypeStruct(h.shape, h.dtype), jax.ShapeDtypeStruct(hm.shape, hm.dtype)],
        scratch_shapes=[pltpu.VMEM((D_MODEL, 2 * D_FF), jnp.bfloat16),
                        pltpu.VMEM((D_FF, D_MODEL), jnp.bfloat16),
                        pltpu.VMEM((2, W1_STAGE_ROWS, 2 * D_FF), jnp.float32),
                        pltpu.VMEM((2, W2_STAGE_ROWS, D_MODEL), jnp.float32),
                        pltpu.SemaphoreType.DMA((2,)), pltpu.SemaphoreType.DMA((2,))],
        compiler_params=_params(1),
        name="ffn",
    )(h, hm, norm_g, ffn_w1, ffn_w2)


def _inproj_kernel(tiles_per_seq, transpose_v, h_ref, g_ref, win_ref, convw_ref, hist_ref,
                   q_ref, k_ref, v_ref, yc_ref, tail_ref, uc_scr):
    i = pl.program_id(0)
    tm = h_ref.shape[0]
    u = _rms(h_ref[...], g_ref[2:3, :]).astype(jnp.bfloat16)
    z = _mm(u, win_ref[...])
    q_ref[...] = (z[:, :A_WIDTH] * (HEAD_DIM ** -0.5)).astype(jnp.bfloat16)
    k_ref[...] = z[:, A_WIDTH:A_WIDTH + KV_WIDTH].astype(jnp.bfloat16)
    v = z[:, A_WIDTH + KV_WIDTH:A_WIDTH + 2 * KV_WIDTH]
    v_ref[...] = (v.T if transpose_v else v).astype(jnp.bfloat16)
    o = A_WIDTH + 2 * KV_WIDTH
    b_gate = z[:, o:o + B_WIDTH]
    c_gate = z[:, o + B_WIDTH:o + 2 * B_WIDTH]
    x_in = z[:, o + 2 * B_WIDTH:o + 3 * B_WIDTH]
    uc = c_gate * x_in

    @pl.when(i % tiles_per_seq == 0)
    def _():
        uc_scr[0:CONV_HALO, :] = hist_ref[...]

    uc_scr[CONV_HALO:CONV_HALO + tm, :] = uc
    y = convw_ref[0:1, :] * uc
    for j in range(1, CONV_WIDTH):
        y = y + convw_ref[j:j + 1, :] * uc_scr[CONV_HALO - j:CONV_HALO - j + tm, :]
    yc_ref[...] = (b_gate * y).astype(jnp.bfloat16)
    tail = uc[tm - CONV_HALO:, :]
    tail_ref[...] = tail
    uc_scr[0:CONV_HALO, :] = tail


def _inproj(h, norm_g, mix_w_in, conv_w, hist, layer, e, rows_per_seq, transpose_v):
    n = h.shape[0]
    tm = min(ROW_TILE, n)
    row = lambda w: pl.BlockSpec((tm, w), lambda i: (i, 0))
    bf = lambda *shape: jax.ShapeDtypeStruct(shape, jnp.bfloat16)
    if transpose_v:
        v_spec, v_shape = pl.BlockSpec((KV_WIDTH, tm), lambda i: (0, i)), bf(KV_WIDTH, n)
    else:
        v_spec, v_shape = row(KV_WIDTH), bf(n, KV_WIDTH)
    return pl.pallas_call(
        functools.partial(_inproj_kernel, rows_per_seq // tm, transpose_v),
        grid=(n // tm,),
        in_specs=[row(D_MODEL), _slab_spec(norm_g, layer), _slab_spec(mix_w_in, e),
                  _slab_spec(conv_w, e), _const_spec((CONV_HALO, B_WIDTH))],
        out_specs=[row(A_WIDTH), row(KV_WIDTH), v_spec, row(B_WIDTH),
                   pl.BlockSpec((CONV_HALO, B_WIDTH), lambda i: (0, 0))],
        out_shape=[bf(n, A_WIDTH), bf(n, KV_WIDTH), v_shape, bf(n, B_WIDTH),
                   jax.ShapeDtypeStruct((CONV_HALO, B_WIDTH), jnp.float32)],
        scratch_shapes=[pltpu.VMEM((CONV_HALO + tm, B_WIDTH), jnp.float32)],
        compiler_params=_params(1),
        name="inproj_conv",
    )(h, norm_g, mix_w_in, conv_w, hist)


def _bias_from_buckets(bucket, rel_ref, head):
    body = lambda b, acc: jnp.where(bucket == b, rel_ref[b, head], acc)
    acc = jax.lax.fori_loop(0, REL_BUCKETS, body, jnp.zeros(bucket.shape, jnp.float32))
    return jnp.where(bucket >= 0, acc, NEG_INF)


def _attn_kernel(e, q_ref, kp_ref, kc_ref, vtp_ref, vtc_ref, km_ref, vtm_ref, bktc_ref, bktm_ref,
                 rel_ref, sink_ref, o_ref, bias_comb_scr, bias_meta_scr, sink_scr):
    j = pl.program_id(1)
    key_idx = jax.lax.broadcasted_iota(jnp.int32, (BLOCK, GROUP_LANES), 0)
    qry_idx = jax.lax.broadcasted_iota(jnp.int32, (BLOCK, GROUP_LANES), 1) % BLOCK
    from_prev = key_idx > qry_idx

    @pl.when(jnp.logical_and(pl.program_id(0) == 0, j == 0))
    def _():
        bktc = bktc_ref[...]
        bktm = bktm_ref[...]
        for h in range(A_HEADS):
            g, i = divmod(h, A_GROUP)
            lanes = slice(i * BLOCK, (i + 1) * BLOCK)
            bias = _bias_from_buckets(bktc, rel_ref, h)
            bias_comb_scr[0, g, :, lanes] = bias
            bias_comb_scr[1, g, :, lanes] = jnp.where(from_prev[:, :BLOCK], NEG_INF, bias)
            bias_meta_scr[:, h * BLOCK:(h + 1) * BLOCK] = _bias_from_buckets(bktm, rel_ref, h)
            sink_scr[g, :, lanes] = jnp.full((8, BLOCK), sink_ref[e, h], jnp.float32)

    kcat = jnp.concatenate([kp_ref[...], kc_ref[...]], axis=0)
    vtcat = jnp.concatenate([vtp_ref[...], vtc_ref[...]], axis=1)
    km = km_ref[...]
    vtm = vtm_ref[...]
    for qb in range(ATTN_QBLOCKS):
        q = q_ref[qb * BLOCK:(qb + 1) * BLOCK, :]
        variant = jnp.where(j == 0, 1, 0) if qb == 0 else 0
        meta_row = pl.multiple_of((j * ATTN_QBLOCKS + qb) * N_META, N_META)
        outs = []
        for g in range(A_KV_HEADS):
            dims = slice(g * HEAD_DIM, (g + 1) * HEAD_DIM)
            q_stack = jnp.concatenate(
                [q[:, (g * A_GROUP + i) * HEAD_DIM:(g * A_GROUP + i + 1) * HEAD_DIM]
                 for i in range(A_GROUP)], axis=0)
            k_all = jnp.concatenate([kcat[qb * BLOCK:(qb + 2) * BLOCK, dims], km[:, dims]], axis=0)
            st = jax.lax.dot_general(k_all, q_stack, _NT, preferred_element_type=jnp.float32)
            comb = jnp.where(from_prev, st[0:BLOCK], st[BLOCK:2 * BLOCK]) + bias_comb_scr[variant, g]
            meta = st[2 * BLOCK:] + bias_meta_scr[pl.ds(meta_row, N_META),
                                                  g * GROUP_LANES:(g + 1) * GROUP_LANES]
            sink = sink_scr[g, 0:1, :]
            m = jnp.maximum(jnp.maximum(jnp.max(comb, axis=0, keepdims=True),
                                        jnp.max(meta, axis=0, keepdims=True)), sink)
            pc = jnp.exp(comb - m)
            pm = jnp.exp(meta - m)
            den = (jnp.sum(pc, axis=0, keepdims=True) + jnp.sum(pm, axis=0, keepdims=True)
                   + jnp.exp(sink - m))
            p_band = jnp.concatenate([jnp.where(from_prev, pc, 0.0), jnp.where(from_prev, 0.0, pc)],
                                     axis=0).astype(jnp.bfloat16)
            ot = jnp.dot(vtcat[dims, qb * BLOCK:(qb + 2) * BLOCK], p_band,
                         preferred_element_type=jnp.float32)
            ot = ot + jnp.dot(vtm[dims, :], pm.astype(jnp.bfloat16), preferred_element_type=jnp.float32)
            ot = ot * (1.0 / den)
            outs += [ot[:, i * BLOCK:(i + 1) * BLOCK] for i in range(A_GROUP)]
        yt = jnp.concatenate(outs, axis=0)
        o_ref[qb * BLOCK:(qb + 1) * BLOCK, :] = yt.T.astype(o_ref.dtype)


def _attention(q, k, vt, k_meta, vt_meta, bkt_comb, bkt_meta, rel_bias, sinks, e, batch, seq):
    nb = seq // BLOCK
    qb = ATTN_QBLOCKS
    tiles = nb // qb
    cur_rows = lambda w: pl.BlockSpec((qb * BLOCK, w), lambda b, j: (b * tiles + j, 0))
    prev_blk = lambda b, j: b * nb + jnp.maximum(j * qb - 1, 0)
    smem = pl.BlockSpec(memory_space=pltpu.SMEM)
    return pl.pallas_call(
        functools.partial(_attn_kernel, e),
        grid=(batch, tiles),
        in_specs=[cur_rows(A_WIDTH),
                  pl.BlockSpec((BLOCK, KV_WIDTH), lambda b, j: (prev_blk(b, j), 0)),
                  cur_rows(KV_WIDTH),
                  pl.BlockSpec((KV_WIDTH, BLOCK), lambda b, j: (0, prev_blk(b, j))),
                  pl.BlockSpec((KV_WIDTH, qb * BLOCK), lambda b, j: (0, b * tiles + j)),
                  _const_spec(k_meta.shape), _const_spec(vt_meta.shape),
                  _const_spec(bkt_comb.shape), _const_spec(bkt_meta.shape), smem, smem],
        out_specs=cur_rows(A_WIDTH),
        out_shape=jax.ShapeDtypeStruct(q.shape, jnp.bfloat16),
        scratch_shapes=[pltpu.VMEM((2, A_KV_HEADS, BLOCK, GROUP_LANES), jnp.float32),
                        pltpu.VMEM((nb * N_META, A_HEADS * BLOCK), jnp.float32),
                        pltpu.VMEM((A_KV_HEADS, 8, GROUP_LANES), jnp.float32)],
        compiler_params=_params(2),
        name="swa_attention",
    )(q, k, k, vt, vt, k_meta, vt_meta, bkt_comb, bkt_meta, rel_bias, sinks)


def _attn_meta_kernel(e, q_ref, k_ref, v_ref, bkt_ref, rel_ref, sink_ref, o_ref):
    q = q_ref[...]
    k = k_ref[...]
    v = v_ref[...]
    bkt = bkt_ref[...]
    outs = []
    for h in range(A_HEADS):
        dims = slice((h // A_GROUP) * HEAD_DIM, (h // A_GROUP + 1) * HEAD_DIM)
        s = jax.lax.dot_general(q[:, h * HEAD_DIM:(h + 1) * HEAD_DIM], k[:, dims], _NT,
                                preferred_element_type=jnp.float32)
        s = s + _bias_from_buckets(bkt, rel_ref, h)
        sink = sink_ref[e, h]
        m = jnp.maximum(jnp.max(s, axis=-1, keepdims=True), sink)
        p = jnp.exp(s - m)
        den = jnp.sum(p, axis=-1, keepdims=True) + jnp.exp(sink - m)
        o = jnp.dot(p.astype(jnp.bfloat16), v[:, dims], preferred_element_type=jnp.float32)
        outs.append(o * (1.0 / den))
    o_ref[...] = jnp.concatenate(outs, axis=-1).astype(o_ref.dtype)


def _attention_meta(q, k, v, bkt_self, rel_bias, sinks, e):
    vmem = pl.BlockSpec(memory_space=pltpu.VMEM)
    smem = pl.BlockSpec(memory_space=pltpu.SMEM)
    return pl.pallas_call(
        functools.partial(_attn_meta_kernel, e),
        in_specs=[vmem, vmem, vmem, vmem, smem, smem],
        out_specs=vmem,
        out_shape=jax.ShapeDtypeStruct(q.shape, jnp.bfloat16),
        name="meta_attention",
    )(q, k, v, bkt_self, rel_bias, sinks)


def _t5_bucket(dist):
    n = jnp.maximum(dist, 0)
    max_exact = REL_BUCKETS // 2
    nf = jnp.maximum(n, 1).astype(jnp.float32)
    large = max_exact + (jnp.log(nf / max_exact) / math.log(REL_MAX_DIST / max_exact)
                         * (REL_BUCKETS - max_exact)).astype(jnp.int32)
    large = jnp.minimum(large, REL_BUCKETS - 1)
    return jnp.where(n < max_exact, n, large)


def _bucket_tables(seq):
    bucket = lambda dist: jnp.where(dist >= 0, _t5_bucket(dist), -1).astype(jnp.int32)
    idx = jnp.arange(BLOCK)
    comb = bucket((idx[None, :] - idx[:, None]) % BLOCK)
    m = jnp.arange(N_META)
    pos = N_META + jnp.arange(seq).reshape(seq // BLOCK, 1, BLOCK)
    meta = bucket(pos - m[None, :, None]).reshape(seq // BLOCK * N_META, BLOCK)
    self_ = bucket(m[:, None] - m[None, :])
    return comb, meta, self_


def _outproj_kernel(h_ref, ya_ref, yc_ref, wo_ref, g_ref, o_ref):
    mix = _mm(ya_ref[...], wo_ref[0:A_WIDTH, :]) + _mm(yc_ref[...], wo_ref[A_WIDTH:, :])
    o_ref[...] = h_ref[...] + _rms(mix, g_ref[3:4, :])


def _outproj(h, ya, yc, mix_w_out, norm_g, layer, e):
    n = h.shape[0]
    tm = min(ROW_TILE, n)
    row = lambda w: pl.BlockSpec((tm, w), lambda i: (i, 0))
    return pl.pallas_call(
        _outproj_kernel,
        grid=(n // tm,),
        in_specs=[row(D_MODEL), row(A_WIDTH), row(B_WIDTH), _slab_spec(mix_w_out, e),
                  _slab_spec(norm_g, layer)],
        out_specs=row(D_MODEL),
        out_shape=jax.ShapeDtypeStruct(h.shape, h.dtype),
        compiler_params=_params(1),
        name="outproj",
    )(h, ya, yc, mix_w_out, norm_g)


def _pool_kernel(tiles_per_seq, pos0, h_ref, prev_ref, first_ref, g_ref, pw_ref, ps_ref, o_ref, u_scr):
    j = pl.program_id(0) % tiles_per_seq
    tm = h_ref.shape[0]
    x = h_ref[...]
    gpre = g_ref[2:3, :]
    hist = jnp.where(j == 0, first_ref[...], prev_ref[...])
    u = _rms(x, gpre)
    u_scr[0:POOL_HALO, :] = _rms(hist, gpre)
    u_scr[POOL_HALO:POOL_HALO + tm, :] = u
    t = pos0 + j * tm + jax.lax.broadcasted_iota(jnp.int32, (tm, 1), 0)
    ys = []
    for gi, w in enumerate(POOL_SIZES):
        cols = slice(gi * POOL_GROUP_DIM, (gi + 1) * POOL_GROUP_DIM)
        ug = u[:, cols]
        s = ug
        for back in range(1, w):
            s = s + u_scr[POOL_HALO - back:POOL_HALO - back + tm, cols]
        cnt = jnp.minimum(t + 1, w).astype(jnp.float32)
        d = (s / cnt - ug).astype(jnp.bfloat16)
        ys.append(_mm(d, pw_ref[gi]))
    y = jnp.concatenate(ys, axis=-1) * ps_ref[...]
    o_ref[...] = x + _rms(y, g_ref[3:4, :])


def _pool(h, first_hist, norm_g, pool_w, pool_scale, layer, o, rows_per_seq, pos0):
    n = h.shape[0]
    tm = min(POOL_TILE, n)
    halo_blocks = tm // POOL_HALO
    row = pl.BlockSpec((tm, D_MODEL), lambda i: (i, 0))
    prev = pl.BlockSpec((POOL_HALO, D_MODEL), lambda i: (jnp.maximum(i * halo_blocks - 1, 0), 0))
    return pl.pallas_call(
        functools.partial(_pool_kernel, rows_per_seq // tm, pos0),
        grid=(n // tm,),
        in_specs=[row, prev, _const_spec((POOL_HALO, D_MODEL)), _slab_spec(norm_g, layer),
                  _slab_spec(pool_w, o),
                  pl.BlockSpec((1, D_MODEL), lambda i: (o, 0), pipeline_mode=pl.Buffered(1))],
        out_specs=row,
        out_shape=jax.ShapeDtypeStruct(h.shape, h.dtype),
        scratch_shapes=[pltpu.VMEM((POOL_HALO + tm, D_MODEL), jnp.float32)],
        compiler_params=_params(1),
        name="pool_mixer",
    )(h, h, first_hist, norm_g, pool_w, pool_scale)


def kernel(x, meta_tokens, rel_bias, norm_g, ffn_w1, ffn_w2, mix_w_in, mix_w_out, attn_sinks,
           conv_w, pool_w, pool_scale):
    batch, seq, _ = x.shape
    depth = norm_g.shape[0]
    h = x.reshape(batch * seq, D_MODEL)
    hm = meta_tokens.astype(x.dtype)
    bkt_comb, bkt_meta, bkt_self = _bucket_tables(seq)
    for layer in range(depth):
        last = layer == depth - 1
        h, hm = _ffn(h, hm, norm_g, ffn_w1, ffn_w2, layer, 0, True)
        if layer % 2 == 0:
            e = layer // 2
            no_hist = jnp.zeros((CONV_HALO, B_WIDTH), jnp.float32)
            qm, km, vm, ycm, tail_m = _inproj(hm, norm_g, mix_w_in, conv_w, no_hist, layer, e, N_META, False)
            q, k, vt, yc, _ = _inproj(h, norm_g, mix_w_in, conv_w, tail_m, layer, e, seq, True)
            ya = _attention(q, k, vt, km, vm.T, bkt_comb, bkt_meta, rel_bias, attn_sinks, e, batch, seq)
            h = _outproj(h, ya, yc, mix_w_out, norm_g, layer, e)
            if not last:
                yam = _attention_meta(qm, km, vm, bkt_self, rel_bias, attn_sinks, e)
                hm = _outproj(hm, yam, ycm, mix_w_out, norm_g, layer, e)
        else:
            o = layer // 2
            h_new = _pool(h, hm, norm_g, pool_w, pool_scale, layer, o, seq, N_META)
            if not last:
                hm = _pool(hm, jnp.zeros((POOL_HALO, D_MODEL), hm.dtype), norm_g, pool_w, pool_scale,
                           layer, o, N_META, 0)
            h = h_new
        h, hm = _ffn(h, hm, norm_g, ffn_w1, ffn_w2, layer, 1, not last)
    return h.reshape(batch, seq, D_MODEL)
```

```python
import functools
import math

import jax
import jax.numpy as jnp
from jax.experimental import pallas as pl
from jax.experimental.pallas import tpu as pltpu

D_MODEL = 1024
N_META = 16
A_HEADS = 8
A_KV_HEADS = 2
A_GROUP = A_HEADS // A_KV_HEADS
HEAD_DIM = 64
WINDOW = 128
BLOCK = 128
A_WIDTH = A_HEADS * HEAD_DIM
KV_WIDTH = A_KV_HEADS * HEAD_DIM
B_WIDTH = D_MODEL // 2
CONV_WIDTH = 3
IN_WIDTH = A_WIDTH + 2 * KV_WIDTH + 3 * B_WIDTH
POOL_SIZES = (2, 4, 8, 16)
N_POOL_GROUPS = 4
POOL_GROUP_DIM = D_MODEL // N_POOL_GROUPS
REL_BUCKETS = 32
REL_MAX_DIST = 128
D_FF = 2816
HALF_STEP = 0.5
RMS_EPS = 1e-6

VMEM_LIMIT_BYTES_V7X = 56 * 1024 * 1024
ROW_TILE = 512
FFN_ROW_TILE = 1024
FFN_SUB_ROWS = 512
FF_CHUNK = 256
N_FF_CHUNKS = D_FF // FF_CHUNK
W1_STAGE_ROWS = 128
W2_STAGE_ROWS = 256
POOL_TILE = 512
POOL_HALO = 16
CONV_HALO = 8
ATTN_QBLOCKS = 4
GROUP_LANES = A_GROUP * BLOCK

NEG_INF = float("-inf")
_NT = (((1,), (1,)), ((), ()))

assert WINDOW == BLOCK and N_META == POOL_HALO


def _rms(x, g):
    return x * jax.lax.rsqrt(jnp.mean(x * x, axis=-1, keepdims=True) + RMS_EPS) * g


def _const_spec(shape):
    zeros = (0,) * len(shape)
    return pl.BlockSpec(shape, lambda *_: zeros, pipeline_mode=pl.Buffered(1))


def _slab_spec(arr, *lead):
    rest = arr.shape[len(lead):]
    idx = tuple(lead) + (0,) * len(rest)
    return pl.BlockSpec((None,) * len(lead) + rest, lambda *_: idx, pipeline_mode=pl.Buffered(1))


def _mm(a, w):
    return jax.lax.dot_general(a, w, (((1,), (0,)), ((), ())), preferred_element_type=jnp.float32)


def _params(n_axes):
    return pltpu.CompilerParams(dimension_semantics=("arbitrary",) * n_axes,
                                vmem_limit_bytes=VMEM_LIMIT_BYTES_V7X)


def _load_bf16(src, dst, stage, sem, chunk):
    rows = src.shape[0]
    copy = lambda k: pltpu.make_async_copy(src.at[pl.ds(k * chunk, chunk), :], stage.at[k % 2], sem.at[k % 2])
    n = rows // chunk
    copy(0).start()
    for k in range(n):
        if k + 1 < n:
            copy(k + 1).start()
        copy(k).wait()
        dst[k * chunk:(k + 1) * chunk, :] = stage[k % 2].astype(dst.dtype)


def _swiglu(xn, w1, w2):
    acc = jnp.zeros((xn.shape[0], D_MODEL), jnp.float32)
    for c in range(N_FF_CHUNKS):
        cols = slice(c * FF_CHUNK, (c + 1) * FF_CHUNK)
        gate = _mm(xn, w1[:, cols])
        up = _mm(xn, w1[:, D_FF + c * FF_CHUNK:D_FF + (c + 1) * FF_CHUNK])
        act = (gate * (1.0 / (1.0 + jnp.exp(-gate))) * up).astype(jnp.bfloat16)
        acc = acc + _mm(act, w2[cols, :])
    return acc


def _ffn_rows(x, g_ref, pre, post, w1, w2):
    xn = _rms(x, g_ref[pre:pre + 1, :]).astype(jnp.bfloat16)
    return x + HALF_STEP * _rms(_swiglu(xn, w1, w2), g_ref[post:post + 1, :])


def _ffn_kernel(layer, half, with_meta, h_ref, hm_ref, g_ref, w1_hbm, w2_hbm, o_ref, om_ref,
                w1, w2, stage1, stage2, sem1, sem2):
    pre, post = 4 * half, 4 * half + 1

    @pl.when(pl.program_id(0) == 0)
    def _():
        _load_bf16(w1_hbm.at[layer, half], w1, stage1, sem1, W1_STAGE_ROWS)
        _load_bf16(w2_hbm.at[layer, half], w2, stage2, sem2, W2_STAGE_ROWS)
        if with_meta:
            om_ref[...] = _ffn_rows(hm_ref[...], g_ref, pre, post, w1, w2)
        else:
            om_ref[...] = hm_ref[...]

    subs = [slice(r, r + FFN_SUB_ROWS) for r in range(0, h_ref.shape[0], FFN_SUB_ROWS)]
    xns = [_rms(h_ref[rows, :], g_ref[pre:pre + 1, :]).astype(jnp.bfloat16) for rows in subs]
    accs = [_swiglu(xn, w1, w2) for xn in xns]
    for rows, acc in zip(subs, accs):
        o_ref[rows, :] = h_ref[rows, :] + HALF_STEP * _rms(acc, g_ref[post:post + 1, :])


def _ffn(h, hm, norm_g, ffn_w1, ffn_w2, layer, half, with_meta):
    n = h.shape[0]
    tm = FFN_ROW_TILE
    row = pl.BlockSpec((tm, D_MODEL), lambda i: (i, 0))
    any_spec = pl.BlockSpec(memory_space=pl.ANY)
    return pl.pallas_call(
        functools.partial(_ffn_kernel, layer, half, with_meta),
        grid=(n // tm,),
        in_specs=[row, _const_spec(hm.shape), _slab_spec(norm_g, layer), any_spec, any_spec],
        out_specs=[row, pl.BlockSpec(hm.shape, lambda i: (0, 0))],
        out_shape=[jax.ShapeDtypeStruct(h.shape, h.dtype), jax.ShapeDtypeStruct(hm.shape, hm.dtype)],
        scratch_shapes=[pltpu.VMEM((D_MODEL, 2 * D_FF), jnp.bfloat16),
                        pltpu.VMEM((D_FF, D_MODEL), jnp.bfloat16),
                        pltpu.VMEM((2, W1_STAGE_ROWS, 2 * D_FF), jnp.float32),
                        pltpu.VMEM((2, W2_STAGE_ROWS, D_MODEL), jnp.float32),
                        pltpu.SemaphoreType.DMA((2,)), pltpu.SemaphoreType.DMA((2,))],
        compiler_params=_params(1),
        name="ffn",
    )(h, hm, norm_g, ffn_w1, ffn_w2)


def _inproj_kernel(tiles_per_seq, transpose_v, h_ref, g_ref, win_ref, convw_ref, hist_ref,
                   q_ref, k_ref, v_ref, yc_ref, tail_ref, uc_scr):
    i = pl.program_id(0)
    tm = h_ref.shape[0]
    u = _rms(h_ref[...], g_ref[2:3, :]).astype(jnp.bfloat16)
    z = _mm(u, win_ref[...])
    q_ref[...] = (z[:, :A_WIDTH] * (HEAD_DIM ** -0.5)).astype(jnp.bfloat16)
    k_ref[...] = z[:, A_WIDTH:A_WIDTH + KV_WIDTH].astype(jnp.bfloat16)
    v = z[:, A_WIDTH + KV_WIDTH:A_WIDTH + 2 * KV_WIDTH]
    v_ref[...] = (v.T if transpose_v else v).astype(jnp.bfloat16)
    o = A_WIDTH + 2 * KV_WIDTH
    b_gate = z[:, o:o + B_WIDTH]
    c_gate = z[:, o + B_WIDTH:o + 2 * B_WIDTH]
    x_in = z[:, o + 2 * B_WIDTH:o + 3 * B_WIDTH]
    uc = c_gate * x_in

    @pl.when(i % tiles_per_seq == 0)
    def _():
        uc_scr[0:CONV_HALO, :] = hist_ref[...]

    uc_scr[CONV_HALO:CONV_HALO + tm, :] = uc
    y = convw_ref[0:1, :] * uc
    for j in range(1, CONV_WIDTH):
        y = y + convw_ref[j:j + 1, :] * uc_scr[CONV_HALO - j:CONV_HALO - j + tm, :]
    yc_ref[...] = (b_gate * y).astype(jnp.bfloat16)
    tail = uc[tm - CONV_HALO:, :]
    tail_ref[...] = tail
    uc_scr[0:CONV_HALO, :] = tail


def _inproj(h, norm_g, mix_w_in, conv_w, hist, layer, e, rows_per_seq, transpose_v):
    n = h.shape[0]
    tm = min(ROW_TILE, n)
    row = lambda w: pl.BlockSpec((tm, w), lambda i: (i, 0))
    bf = lambda *shape: jax.ShapeDtypeStruct(shape, jnp.bfloat16)
    if transpose_v:
        v_spec, v_shape = pl.BlockSpec((KV_WIDTH, tm), lambda i: (0, i)), bf(KV_WIDTH, n)
    else:
        v_spec, v_shape = row(KV_WIDTH), bf(n, KV_WIDTH)
    return pl.pallas_call(
        functools.partial(_inproj_kernel, rows_per_seq // tm, transpose_v),
        grid=(n // tm,),
        in_specs=[row(D_MODEL), _slab_spec(norm_g, layer), _slab_spec(mix_w_in, e),
                  _slab_spec(conv_w, e), _const_spec((CONV_HALO, B_WIDTH))],
        out_specs=[row(A_WIDTH), row(KV_WIDTH), v_spec, row(B_WIDTH),
                   pl.BlockSpec((CONV_HALO, B_WIDTH), lambda i: (0, 0))],
        out_shape=[bf(n, A_WIDTH), bf(n, KV_WIDTH), v_shape, bf(n, B_WIDTH),
                   jax.ShapeDtypeStruct((CONV_HALO, B_WIDTH), jnp.float32)],
        scratch_shapes=[pltpu.VMEM((CONV_HALO + tm, B_WIDTH), jnp.float32)],
        compiler_params=_params(1),
        name="inproj_conv",
    )(h, norm_g, mix_w_in, conv_w, hist)


def _bias_from_buckets(bucket, rel_ref, head):
    body = lambda b, acc: jnp.where(bucket == b, rel_ref[b, head], acc)
    acc = jax.lax.fori_loop(0, REL_BUCKETS, body, jnp.zeros(bucket.shape, jnp.float32))
    return jnp.where(bucket >= 0, acc, NEG_INF)


def _attn_kernel(e, q_ref, kp_ref, kc_ref, vtp_ref, vtc_ref, km_ref, vtm_ref, bktc_ref, bktm_ref,
                 rel_ref, sink_ref, o_ref, bias_comb_scr, bias_meta_scr, sink_scr):
    j = pl.program_id(1)
    key_idx = jax.lax.broadcasted_iota(jnp.int32, (BLOCK, GROUP_LANES), 0)
    qry_idx = jax.lax.broadcasted_iota(jnp.int32, (BLOCK, GROUP_LANES), 1) % BLOCK
    from_prev = key_idx > qry_idx

    @pl.when(jnp.logical_and(pl.program_id(0) == 0, j == 0))
    def _():
        bktc = bktc_ref[...]
        bktm = bktm_ref[...]
        for h in range(A_HEADS):
            g, i = divmod(h, A_GROUP)
            lanes = slice(i * BLOCK, (i + 1) * BLOCK)
            bias = _bias_from_buckets(bktc, rel_ref, h)
            bias_comb_scr[0, g, :, lanes] = bias
            bias_comb_scr[1, g, :, lanes] = jnp.where(from_prev[:, :BLOCK], NEG_INF, bias)
            bias_meta_scr[:, h * BLOCK:(h + 1) * BLOCK] = _bias_from_buckets(bktm, rel_ref, h)
            sink_scr[g, :, lanes] = jnp.full((8, BLOCK), sink_ref[e, h], jnp.float32)

    kcat = jnp.concatenate([kp_ref[...], kc_ref[...]], axis=0)
    vtcat = jnp.concatenate([vtp_ref[...], vtc_ref[...]], axis=1)
    km = km_ref[...]
    vtm = vtm_ref[...]
    for qb in range(ATTN_QBLOCKS):
        q = q_ref[qb * BLOCK:(qb + 1) * BLOCK, :]
        variant = jnp.where(j == 0, 1, 0) if qb == 0 else 0
        meta_row = pl.multiple_of((j * ATTN_QBLOCKS + qb) * N_META, N_META)
        outs = []
        for g in range(A_KV_HEADS):
            dims = slice(g * HEAD_DIM, (g + 1) * HEAD_DIM)
            q_stack = jnp.concatenate(
                [q[:, (g * A_GROUP + i) * HEAD_DIM:(g * A_GROUP + i + 1) * HEAD_DIM]
                 for i in range(A_GROUP)], axis=0)
            k_all = jnp.concatenate([kcat[qb * BLOCK:(qb + 2) * BLOCK, dims], km[:, dims]], axis=0)
            st = jax.lax.dot_general(k_all, q_stack, _NT, preferred_element_type=jnp.float32)
            comb = jnp.where(from_prev, st[0:BLOCK], st[BLOCK:2 * BLOCK]) + bias_comb_scr[variant, g]
            meta = st[2 * BLOCK:] + bias_meta_scr[pl.ds(meta_row, N_META),
                                                  g * GROUP_LANES:(g + 1) * GROUP_LANES]
            sink = sink_scr[g, 0:1, :]
            m = jnp.maximum(jnp.maximum(jnp.max(comb, axis=0, keepdims=True),
                                        jnp.max(meta, axis=0, keepdims=True)), sink)
            pc = jnp.exp(comb - m)
            pm = jnp.exp(meta - m)
            den = (jnp.sum(pc, axis=0, keepdims=True) + jnp.sum(pm, axis=0, keepdims=True)
                   + jnp.exp(sink - m))
            p_band = jnp.concatenate([jnp.where(from_prev, pc, 0.0), jnp.where(from_prev, 0.0, pc)],
                                     axis=0).astype(jnp.bfloat16)
            ot = jnp.dot(vtcat[dims, qb * BLOCK:(qb + 2) * BLOCK], p_band,
                         preferred_element_type=jnp.float32)
            ot = ot + jnp.dot(vtm[dims, :], pm.astype(jnp.bfloat16), preferred_element_type=jnp.float32)
            ot = ot * (1.0 / den)
            outs += [ot[:, i * BLOCK:(i + 1) * BLOCK] for i in range(A_GROUP)]
        yt = jnp.concatenate(outs, axis=0)
        o_ref[qb * BLOCK:(qb + 1) * BLOCK, :] = yt.T.astype(o_ref.dtype)


def _attention(q, k, vt, k_meta, vt_meta, bkt_comb, bkt_meta, rel_bias, sinks, e, batch, seq):
    nb = seq // BLOCK
    qb = ATTN_QBLOCKS
    tiles = nb // qb
    cur_rows = lambda w: pl.BlockSpec((qb * BLOCK, w), lambda b, j: (b * tiles + j, 0))
    prev_blk = lambda b, j: b * nb + jnp.maximum(j * qb - 1, 0)
    smem = pl.BlockSpec(memory_space=pltpu.SMEM)
    return pl.pallas_call(
        functools.partial(_attn_kernel, e),
        grid=(batch, tiles),
        in_specs=[cur_rows(A_WIDTH),
                  pl.BlockSpec((BLOCK, KV_WIDTH), lambda b, j: (prev_blk(b, j), 0)),
                  cur_rows(KV_WIDTH),
                  pl.BlockSpec((KV_WIDTH, BLOCK), lambda b, j: (0, prev_blk(b, j))),
                  pl.BlockSpec((KV_WIDTH, qb * BLOCK), lambda b, j: (0, b * tiles + j)),
                  _const_spec(k_meta.shape), _const_spec(vt_meta.shape),
                  _const_spec(bkt_comb.shape), _const_spec(bkt_meta.shape), smem, smem],
        out_specs=cur_rows(A_WIDTH),
        out_shape=jax.ShapeDtypeStruct(q.shape, jnp.bfloat16),
        scratch_shapes=[pltpu.VMEM((2, A_KV_HEADS, BLOCK, GROUP_LANES), jnp.float32),
                        pltpu.VMEM((nb * N_META, A_HEADS * BLOCK), jnp.float32),
                        pltpu.VMEM((A_KV_HEADS, 8, GROUP_LANES), jnp.float32)],
        compiler_params=_params(2),
        name="swa_attention",
    )(q, k, k, vt, vt, k_meta, vt_meta, bkt_comb, bkt_meta, rel_bias, sinks)


def _attn_meta_kernel(e, q_ref, k_ref, v_ref, bkt_ref, rel_ref, sink_ref, o_ref):
    q = q_ref[...]
    k = k_ref[...]
    v = v_ref[...]
    bkt = bkt_ref[...]
    outs = []
    for h in range(A_HEADS):
        dims = slice((h // A_GROUP) * HEAD_DIM, (h // A_GROUP + 1) * HEAD_DIM)
        s = jax.lax.dot_general(q[:, h * HEAD_DIM:(h + 1) * HEAD_DIM], k[:, dims], _NT,
                                preferred_element_type=jnp.float32)
        s = s + _bias_from_buckets(bkt, rel_ref, h)
        sink = sink_ref[e, h]
        m = jnp.maximum(jnp.max(s, axis=-1, keepdims=True), sink)
        p = jnp.exp(s - m)
        den = jnp.sum(p, axis=-1, keepdims=True) + jnp.exp(sink - m)
        o = jnp.dot(p.astype(jnp.bfloat16), v[:, dims], preferred_element_type=jnp.float32)
        outs.append(o * (1.0 / den))
    o_ref[...] = jnp.concatenate(outs, axis=-1).astype(o_ref.dtype)


def _attention_meta(q, k, v, bkt_self, rel_bias, sinks, e):
    vmem = pl.BlockSpec(memory_space=pltpu.VMEM)
    smem = pl.BlockSpec(memory_space=pltpu.SMEM)
    return pl.pallas_call(
        functools.partial(_attn_meta_kernel, e),
        in_specs=[vmem, vmem, vmem, vmem, smem, smem],
        out_specs=vmem,
        out_shape=jax.ShapeDtypeStruct(q.shape, jnp.bfloat16),
        name="meta_attention",
    )(q, k, v, bkt_self, rel_bias, sinks)


def _t5_bucket(dist):
    n = jnp.maximum(dist, 0)
    max_exact = REL_BUCKETS // 2
    nf = jnp.maximum(n, 1).astype(jnp.float32)
    large = max_exact + (jnp.log(nf / max_exact) / math.log(REL_MAX_DIST / max_exact)
                         * (REL_BUCKETS - max_exact)).astype(jnp.int32)
    large = jnp.minimum(large, REL_BUCKETS - 1)
    return jnp.where(n < max_exact, n, large)


def _bucket_tables(seq):
    bucket = lambda dist: jnp.where(dist >= 0, _t5_bucket(dist), -1).astype(jnp.int32)
    idx = jnp.arange(BLOCK)
    comb = bucket((idx[None, :] - idx[:, None]) % BLOCK)
    m = jnp.arange(N_META)
    pos = N_META + jnp.arange(seq).reshape(seq // BLOCK, 1, BLOCK)
    meta = bucket(pos - m[None, :, None]).reshape(seq // BLOCK * N_META, BLOCK)
    self_ = bucket(m[:, None] - m[None, :])
    return comb, meta, self_


def _outproj_kernel(h_ref, ya_ref, yc_ref, wo_ref, g_ref, o_ref):
    mix = _mm(ya_ref[...], wo_ref[0:A_WIDTH, :]) + _mm(yc_ref[...], wo_ref[A_WIDTH:, :])
    o_ref[...] = h_ref[...] + _rms(mix, g_ref[3:4, :])


def _outproj(h, ya, yc, mix_w_out, norm_g, layer, e):
    n = h.shape[0]
    tm = min(ROW_TILE, n)
    row = lambda w: pl.BlockSpec((tm, w), lambda i: (i, 0))
    return pl.pallas_call(
        _outproj_kernel,
        grid=(n // tm,),
        in_specs=[row(D_MODEL), row(A_WIDTH), row(B_WIDTH), _slab_spec(mix_w_out, e),
                  _slab_spec(norm_g, layer)],
        out_specs=row(D_MODEL),
        out_shape=jax.ShapeDtypeStruct(h.shape, h.dtype),
        compiler_params=_params(1),
        name="outproj",
    )(h, ya, yc, mix_w_out, norm_g)


def _pool_kernel(tiles_per_seq, pos0, h_ref, prev_ref, first_ref, g_ref, pw_ref, ps_ref, o_ref, u_scr):
    j = pl.program_id(0) % tiles_per_seq
    tm = h_ref.shape[0]
    x = h_ref[...]
    gpre = g_ref[2:3, :]
    hist = jnp.where(j == 0, first_ref[...], prev_ref[...])
    u = _rms(x, gpre)
    u_scr[0:POOL_HALO, :] = _rms(hist, gpre)
    u_scr[POOL_HALO:POOL_HALO + tm, :] = u
    t = pos0 + j * tm + jax.lax.broadcasted_iota(jnp.int32, (tm, 1), 0)
    ys = []
    for gi, w in enumerate(POOL_SIZES):
        cols = slice(gi * POOL_GROUP_DIM, (gi + 1) * POOL_GROUP_DIM)
        ug = u[:, cols]
        s = ug
        for back in range(1, w):
            s = s + u_scr[POOL_HALO - back:POOL_HALO - back + tm, cols]
        cnt = jnp.minimum(t + 1, w).astype(jnp.float32)
        d = (s / cnt - ug).astype(jnp.bfloat16)
        ys.append(_mm(d, pw_ref[gi]))
    y = jnp.concatenate(ys, axis=-1) * ps_ref[...]
    o_ref[...] = x + _rms(y, g_ref[3:4, :])


def _pool(h, first_hist, norm_g, pool_w, pool_scale, layer, o, rows_per_seq, pos0):
    n = h.shape[0]
    tm = min(POOL_TILE, n)
    halo_blocks = tm // POOL_HALO
    row = pl.BlockSpec((tm, D_MODEL), lambda i: (i, 0))
    prev = pl.BlockSpec((POOL_HALO, D_MODEL), lambda i: (jnp.maximum(i * halo_blocks - 1, 0), 0))
    return pl.pallas_call(
        functools.partial(_pool_kernel, rows_per_seq // tm, pos0),
        grid=(n // tm,),
        in_specs=[row, prev, _const_spec((POOL_HALO, D_MODEL)), _slab_spec(norm_g, layer),
                  _slab_spec(pool_w, o),
                  pl.BlockSpec((1, D_MODEL), lambda i: (o, 0), pipeline_mode=pl.Buffered(1))],
        out_specs=row,
        out_shape=jax.ShapeDtypeStruct(h.shape, h.dtype),
        scratch_shapes=[pltpu.VMEM((POOL_HALO + tm, D_MODEL), jnp.float32)],
        compiler_params=_params(1),
        name="pool_mixer",
    )(h, h, first_hist, norm_g, pool_w, pool_scale)


def kernel(x, meta_tokens, rel_bias, norm_g, ffn_w1, ffn_w2, mix_w_in, mix_w_out, attn_sinks,
           conv_w, pool_w, pool_scale):
    batch, seq, _ = x.shape
    depth = norm_g.shape[0]
    h = x.reshape(batch * seq, D_MODEL)
    hm = meta_tokens.astype(x.dtype)
    bkt_comb, bkt_meta, bkt_self = _bucket_tables(seq)
    for layer in range(depth):
        last = layer == depth - 1
        h, hm = _ffn(h, hm, norm_g, ffn_w1, ffn_w2, layer, 0, True)
        if layer % 2 == 0:
            e = layer // 2
            no_hist = jnp.zeros((CONV_HALO, B_WIDTH), jnp.float32)
            qm, km, vm, ycm, tail_m = _inproj(hm, norm_g, mix_w_in, conv_w, no_hist, layer, e, N_META, False)
            q, k, vt, yc, _ = _inproj(h, norm_g, mix_w_in, conv_w, tail_m, layer, e, seq, True)
            ya = _attention(q, k, vt, km, vm.T, bkt_comb, bkt_meta, rel_bias, attn_sinks, e, batch, seq)
            h = _outproj(h, ya, yc, mix_w_out, norm_g, layer, e)
            if not last:
                yam = _attention_meta(qm, km, vm, bkt_self, rel_bias, attn_sinks, e)
                hm = _outproj(hm, yam, ycm, mix_w_out, norm_g, layer, e)
        else:
            o = layer // 2
            h_new = _pool(h, hm, norm_g, pool_w, pool_scale, layer, o, seq, N_META)
            if not last:
                hm = _pool(hm, jnp.zeros((POOL_HALO, D_MODEL), hm.dtype), norm_g, pool_w, pool_scale,
                           layer, o, N_META, 0)
            h = h_new
        h, hm = _ffn(h, hm, norm_g, ffn_w1, ffn_w2, layer, 1, not last)
    return h.reshape(batch, seq, D_MODEL)
```

```python
import functools
import math

import jax
import jax.numpy as jnp
from jax.experimental import pallas as pl
from jax.experimental.pallas import tpu as pltpu

D_MODEL = 1024
N_META = 16
A_HEADS = 8
A_KV_HEADS = 2
A_GROUP = A_HEADS // A_KV_HEADS
HEAD_DIM = 64
WINDOW = 128
BLOCK = 128
A_WIDTH = A_HEADS * HEAD_DIM
KV_WIDTH = A_KV_HEADS * HEAD_DIM
B_WIDTH = D_MODEL // 2
CONV_WIDTH = 3
IN_WIDTH = A_WIDTH + 2 * KV_WIDTH + 3 * B_WIDTH
POOL_SIZES = (2, 4, 8, 16)
N_POOL_GROUPS = 4
POOL_GROUP_DIM = D_MODEL // N_POOL_GROUPS
REL_BUCKETS = 32
REL_MAX_DIST = 128
D_FF = 2816
HALF_STEP = 0.5
RMS_EPS = 1e-6

VMEM_LIMIT_BYTES_V7X = 56 * 1024 * 1024
ROW_TILE = 512
FFN_ROW_TILE = 1024
FFN_SUB_ROWS = 512
FF_CHUNK = 256
N_FF_CHUNKS = D_FF // FF_CHUNK
W1_STAGE_ROWS = 64
W2_STAGE_ROWS = 256
POOL_HALO = 16
POOL_PAD = 8
CONV_HALO = 8
ATTN_QBLOCKS = 4
GROUP_LANES = A_GROUP * BLOCK

NEG_INF = float("-inf")
_NT = (((1,), (1,)), ((), ()))

assert WINDOW == BLOCK and N_META == POOL_HALO


def _rms(x, g):
    return x * jax.lax.rsqrt(jnp.mean(x * x, axis=-1, keepdims=True) + RMS_EPS) * g


def _const_spec(shape):
    zeros = (0,) * len(shape)
    return pl.BlockSpec(shape, lambda *_: zeros, pipeline_mode=pl.Buffered(1))


def _slab_spec(arr, *lead):
    rest = arr.shape[len(lead):]
    idx = tuple(lead) + (0,) * len(rest)
    return pl.BlockSpec((None,) * len(lead) + rest, lambda *_: idx, pipeline_mode=pl.Buffered(1))


def _mm(a, w):
    return jax.lax.dot_general(a, w, (((1,), (0,)), ((), ())), preferred_element_type=jnp.float32)


def _params(n_axes):
    return pltpu.CompilerParams(dimension_semantics=("arbitrary",) * n_axes,
                                vmem_limit_bytes=VMEM_LIMIT_BYTES_V7X)


def _load_bf16(src, dst, stage, sem, chunk):
    rows = src.shape[0]
    copy = lambda k: pltpu.make_async_copy(src.at[pl.ds(k * chunk, chunk), :], stage.at[k % 2], sem.at[k % 2])
    n = rows // chunk
    copy(0).start()
    for k in range(n):
        if k + 1 < n:
            copy(k + 1).start()
        copy(k).wait()
        dst[k * chunk:(k + 1) * chunk, :] = stage[k % 2].astype(dst.dtype)


def _swiglu(xn, w1, w2, side=()):
    side = list(side)
    acc = jnp.zeros((xn.shape[0], D_MODEL), jnp.float32)
    for c in range(N_FF_CHUNKS):
        cols = slice(c * FF_CHUNK, (c + 1) * FF_CHUNK)
        gate = _mm(xn, w1[:, cols])
        up = _mm(xn, w1[:, D_FF + c * FF_CHUNK:D_FF + (c + 1) * FF_CHUNK])
        act = (gate * (1.0 / (1.0 + jnp.exp(-gate))) * up).astype(jnp.bfloat16)
        acc = acc + _mm(act, w2[cols, :])
        if side:
            side.pop(0)()
    for thunk in side:
        thunk()
    return acc


def _ffn_rows(x, g_ref, pre, post, w1, w2):
    xn = _rms(x, g_ref[pre:pre + 1, :]).astype(jnp.bfloat16)
    return x + HALF_STEP * _rms(_swiglu(xn, w1, w2), g_ref[post:post + 1, :])


def _ffn_scratch():
    return [pltpu.VMEM((D_MODEL, 2 * D_FF), jnp.bfloat16),
            pltpu.VMEM((D_FF, D_MODEL), jnp.bfloat16),
            pltpu.VMEM((2, W1_STAGE_ROWS, 2 * D_FF), jnp.float32),
            pltpu.VMEM((2, W2_STAGE_ROWS, D_MODEL), jnp.float32),
            pltpu.SemaphoreType.DMA((2,)), pltpu.SemaphoreType.DMA((2,))]


def _load_ffn_weights(w1_hbm, w2_hbm, layer, half, w1, w2, stage1, stage2, sem1, sem2):
    _load_bf16(w1_hbm.at[layer, half], w1, stage1, sem1, W1_STAGE_ROWS)
    _load_bf16(w2_hbm.at[layer, half], w2, stage2, sem2, W2_STAGE_ROWS)


def _sub_blocks(n_rows):
    return [slice(r, r + FFN_SUB_ROWS) for r in range(0, n_rows, FFN_SUB_ROWS)]


def _ffn_tile(read_x, o_ref, g_ref, pre, post, w1, w2, fronts=None):
    subs = _sub_blocks(o_ref.shape[0])
    fronts = fronts or [[] for _ in subs]
    prenorm = lambda k: _rms(read_x(subs[k]), g_ref[pre:pre + 1, :]).astype(jnp.bfloat16)
    xns, accs = {}, {}

    def finish(k):
        o_ref[subs[k], :] = read_x(subs[k]) + HALF_STEP * _rms(accs[k], g_ref[post:post + 1, :])

    for thunk in fronts[0]:
        thunk()
    xns[0] = prenorm(0)
    for k in range(len(subs)):
        side = []
        if k > 0:
            side.append(functools.partial(finish, k - 1))
        if k + 1 < len(subs):
            side += fronts[k + 1]
            side.append(lambda k=k: xns.__setitem__(k + 1, prenorm(k + 1)))
        accs[k] = _swiglu(xns[k], w1, w2, side)
    finish(len(subs) - 1)


def _ffn_kernel(layer, half, with_meta, h_ref, hm_ref, g_ref, w1_hbm, w2_hbm, o_ref, om_ref, *scr):
    pre, post = 4 * half, 4 * half + 1
    w1, w2 = scr[:2]

    @pl.when(pl.program_id(0) == 0)
    def _():
        _load_ffn_weights(w1_hbm, w2_hbm, layer, half, *scr)
        if with_meta:
            om_ref[...] = _ffn_rows(hm_ref[...], g_ref, pre, post, w1, w2)
        else:
            om_ref[...] = hm_ref[...]

    _ffn_tile(lambda rows: h_ref[rows, :], o_ref, g_ref, pre, post, w1, w2)


def _ffn(h, hm, norm_g, ffn_w1, ffn_w2, layer, half, with_meta):
    n = h.shape[0]
    tm = FFN_ROW_TILE
    row = pl.BlockSpec((tm, D_MODEL), lambda i: (i, 0))
    any_spec = pl.BlockSpec(memory_space=pl.ANY)
    return pl.pallas_call(
        functools.partial(_ffn_kernel, layer, half, with_meta),
        grid=(n // tm,),
        in_specs=[row, _const_spec(hm.shape), _slab_spec(norm_g, layer), any_spec, any_spec],
        out_specs=[row, pl.BlockSpec(hm.shape, lambda i: (0, 0))],
        out_shape=[jax.ShapeDtypeStruct(h.shape, h.dtype), jax.ShapeDtypeStruct(hm.shape, hm.dtype)],
        scratch_shapes=_ffn_scratch(),
        compiler_params=_params(1),
        name="ffn",
    )(h, hm, norm_g, ffn_w1, ffn_w2)


def _inproj_kernel(tiles_per_seq, transpose_v, h_ref, g_ref, win_ref, convw_ref, hist_ref,
                   q_ref, k_ref, v_ref, yc_ref, tail_ref, uc_scr):
    i = pl.program_id(0)
    tm = h_ref.shape[0]
    u = _rms(h_ref[...], g_ref[2:3, :]).astype(jnp.bfloat16)
    z = _mm(u, win_ref[...])
    q_ref[...] = (z[:, :A_WIDTH] * (HEAD_DIM ** -0.5)).astype(jnp.bfloat16)
    k_ref[...] = z[:, A_WIDTH:A_WIDTH + KV_WIDTH].astype(jnp.bfloat16)
    v = z[:, A_WIDTH + KV_WIDTH:A_WIDTH + 2 * KV_WIDTH]
    v_ref[...] = (v.T if transpose_v else v).astype(jnp.bfloat16)
    o = A_WIDTH + 2 * KV_WIDTH
    b_gate = z[:, o:o + B_WIDTH]
    c_gate = z[:, o + B_WIDTH:o + 2 * B_WIDTH]
    x_in = z[:, o + 2 * B_WIDTH:o + 3 * B_WIDTH]
    uc = c_gate * x_in

    @pl.when(i % tiles_per_seq == 0)
    def _():
        uc_scr[0:CONV_HALO, :] = hist_ref[...]

    uc_scr[CONV_HALO:CONV_HALO + tm, :] = uc
    y = convw_ref[0:1, :] * uc
    for j in range(1, CONV_WIDTH):
        y = y + convw_ref[j:j + 1, :] * uc_scr[CONV_HALO - j:CONV_HALO - j + tm, :]
    yc_ref[...] = (b_gate * y).astype(jnp.bfloat16)
    tail = uc[tm - CONV_HALO:, :]
    tail_ref[...] = tail
    uc_scr[0:CONV_HALO, :] = tail


def _inproj(h, norm_g, mix_w_in, conv_w, hist, layer, e, rows_per_seq, transpose_v):
    n = h.shape[0]
    tm = min(ROW_TILE, n)
    row = lambda w: pl.BlockSpec((tm, w), lambda i: (i, 0))
    bf = lambda *shape: jax.ShapeDtypeStruct(shape, jnp.bfloat16)
    if transpose_v:
        v_spec, v_shape = pl.BlockSpec((KV_WIDTH, tm), lambda i: (0, i)), bf(KV_WIDTH, n)
    else:
        v_spec, v_shape = row(KV_WIDTH), bf(n, KV_WIDTH)
    return pl.pallas_call(
        functools.partial(_inproj_kernel, rows_per_seq // tm, transpose_v),
        grid=(n // tm,),
        in_specs=[row(D_MODEL), _slab_spec(norm_g, layer), _slab_spec(mix_w_in, e),
                  _slab_spec(conv_w, e), _const_spec((CONV_HALO, B_WIDTH))],
        out_specs=[row(A_WIDTH), row(KV_WIDTH), v_spec, row(B_WIDTH),
                   pl.BlockSpec((CONV_HALO, B_WIDTH), lambda i: (0, 0))],
        out_shape=[bf(n, A_WIDTH), bf(n, KV_WIDTH), v_shape, bf(n, B_WIDTH),
                   jax.ShapeDtypeStruct((CONV_HALO, B_WIDTH), jnp.float32)],
        scratch_shapes=[pltpu.VMEM((CONV_HALO + tm, B_WIDTH), jnp.float32)],
        compiler_params=_params(1),
        name="inproj_conv",
    )(h, norm_g, mix_w_in, conv_w, hist)


def _bias_from_buckets(bucket, rel_ref, head):
    body = lambda b, acc: jnp.where(bucket == b, rel_ref[b, head], acc)
    acc = jax.lax.fori_loop(0, REL_BUCKETS, body, jnp.zeros(bucket.shape, jnp.float32))
    return jnp.where(bucket >= 0, acc, NEG_INF)


def _attn_kernel(e, q_ref, kp_ref, kc_ref, vtp_ref, vtc_ref, km_ref, vtm_ref, bktc_ref, bktm_ref,
                 rel_ref, sink_ref, o_ref, bias_comb_scr, bias_meta_scr, sink_scr):
    j = pl.program_id(1)
    key_idx = jax.lax.broadcasted_iota(jnp.int32, (BLOCK, GROUP_LANES), 0)
    qry_idx = jax.lax.broadcasted_iota(jnp.int32, (BLOCK, GROUP_LANES), 1) % BLOCK
    from_prev = key_idx > qry_idx

    @pl.when(jnp.logical_and(pl.program_id(0) == 0, j == 0))
    def _():
        bktc = bktc_ref[...]
        bktm = bktm_ref[...]
        for h in range(A_HEADS):
            g, i = divmod(h, A_GROUP)
            lanes = slice(i * BLOCK, (i + 1) * BLOCK)
            bias = _bias_from_buckets(bktc, rel_ref, h)
            bias_comb_scr[0, g, :, lanes] = bias
            bias_comb_scr[1, g, :, lanes] = jnp.where(from_prev[:, :BLOCK], NEG_INF, bias)
            bias_meta_scr[:, h * BLOCK:(h + 1) * BLOCK] = _bias_from_buckets(bktm, rel_ref, h)
            sink_scr[g, :, lanes] = jnp.full((8, BLOCK), sink_ref[e, h], jnp.float32)

    kcat = jnp.concatenate([kp_ref[...], kc_ref[...]], axis=0)
    vtcat = jnp.concatenate([vtp_ref[...], vtc_ref[...]], axis=1)
    km = km_ref[...]
    vtm = vtm_ref[...]
    for qb in range(ATTN_QBLOCKS):
        q = q_ref[qb * BLOCK:(qb + 1) * BLOCK, :]
        variant = jnp.where(j == 0, 1, 0) if qb == 0 else 0
        meta_row = pl.multiple_of((j * ATTN_QBLOCKS + qb) * N_META, N_META)
        outs = []
        for g in range(A_KV_HEADS):
            dims = slice(g * HEAD_DIM, (g + 1) * HEAD_DIM)
            q_stack = jnp.concatenate(
                [q[:, (g * A_GROUP + i) * HEAD_DIM:(g * A_GROUP + i + 1) * HEAD_DIM]
                 for i in range(A_GROUP)], axis=0)
            k_all = jnp.concatenate([kcat[qb * BLOCK:(qb + 2) * BLOCK, dims], km[:, dims]], axis=0)
            st = jax.lax.dot_general(k_all, q_stack, _NT, preferred_element_type=jnp.float32)
            comb = jnp.where(from_prev, st[0:BLOCK], st[BLOCK:2 * BLOCK]) + bias_comb_scr[variant, g]
            meta = st[2 * BLOCK:] + bias_meta_scr[pl.ds(meta_row, N_META),
                                                  g * GROUP_LANES:(g + 1) * GROUP_LANES]
            sink = sink_scr[g, 0:1, :]
            m = jnp.maximum(jnp.maximum(jnp.max(comb, axis=0, keepdims=True),
                                        jnp.max(meta, axis=0, keepdims=True)), sink)
            pc = jnp.exp(comb - m)
            pm = jnp.exp(meta - m)
            den = (jnp.sum(pc, axis=0, keepdims=True) + jnp.sum(pm, axis=0, keepdims=True)
                   + jnp.exp(sink - m))
            p_band = jnp.concatenate([jnp.where(from_prev, pc, 0.0), jnp.where(from_prev, 0.0, pc)],
                                     axis=0).astype(jnp.bfloat16)
            ot = jnp.dot(vtcat[dims, qb * BLOCK:(qb + 2) * BLOCK], p_band,
                         preferred_element_type=jnp.float32)
            ot = ot + jnp.dot(vtm[dims, :], pm.astype(jnp.bfloat16), preferred_element_type=jnp.float32)
            ot = ot * (1.0 / den)
            outs += [ot[:, i * BLOCK:(i + 1) * BLOCK] for i in range(A_GROUP)]
        yt = jnp.concatenate(outs, axis=0)
        o_ref[qb * BLOCK:(qb + 1) * BLOCK, :] = yt.T.astype(o_ref.dtype)


def _attention(q, k, vt, k_meta, vt_meta, bkt_comb, bkt_meta, rel_bias, sinks, e, batch, seq):
    nb = seq // BLOCK
    qb = ATTN_QBLOCKS
    tiles = nb // qb
    cur_rows = lambda w: pl.BlockSpec((qb * BLOCK, w), lambda b, j: (b * tiles + j, 0))
    prev_blk = lambda b, j: b * nb + jnp.maximum(j * qb - 1, 0)
    smem = pl.BlockSpec(memory_space=pltpu.SMEM)
    return pl.pallas_call(
        functools.partial(_attn_kernel, e),
        grid=(batch, tiles),
        in_specs=[cur_rows(A_WIDTH),
                  pl.BlockSpec((BLOCK, KV_WIDTH), lambda b, j: (prev_blk(b, j), 0)),
                  cur_rows(KV_WIDTH),
                  pl.BlockSpec((KV_WIDTH, BLOCK), lambda b, j: (0, prev_blk(b, j))),
                  pl.BlockSpec((KV_WIDTH, qb * BLOCK), lambda b, j: (0, b * tiles + j)),
                  _const_spec(k_meta.shape), _const_spec(vt_meta.shape),
                  _const_spec(bkt_comb.shape), _const_spec(bkt_meta.shape), smem, smem],
        out_specs=cur_rows(A_WIDTH),
        out_shape=jax.ShapeDtypeStruct(q.shape, jnp.bfloat16),
        scratch_shapes=[pltpu.VMEM((2, A_KV_HEADS, BLOCK, GROUP_LANES), jnp.float32),
                        pltpu.VMEM((nb * N_META, A_HEADS * BLOCK), jnp.float32),
                        pltpu.VMEM((A_KV_HEADS, 8, GROUP_LANES), jnp.float32)],
        compiler_params=_params(2),
        name="swa_attention",
    )(q, k, k, vt, vt, k_meta, vt_meta, bkt_comb, bkt_meta, rel_bias, sinks)


def _attn_meta_kernel(e, q_ref, k_ref, v_ref, bkt_ref, rel_ref, sink_ref, o_ref):
    q = q_ref[...]
    k = k_ref[...]
    v = v_ref[...]
    bkt = bkt_ref[...]
    outs = []
    for h in range(A_HEADS):
        dims = slice((h // A_GROUP) * HEAD_DIM, (h // A_GROUP + 1) * HEAD_DIM)
        s = jax.lax.dot_general(q[:, h * HEAD_DIM:(h + 1) * HEAD_DIM], k[:, dims], _NT,
                                preferred_element_type=jnp.float32)
        s = s + _bias_from_buckets(bkt, rel_ref, h)
        sink = sink_ref[e, h]
        m = jnp.maximum(jnp.max(s, axis=-1, keepdims=True), sink)
        p = jnp.exp(s - m)
        den = jnp.sum(p, axis=-1, keepdims=True) + jnp.exp(sink - m)
        o = jnp.dot(p.astype(jnp.bfloat16), v[:, dims], preferred_element_type=jnp.float32)
        outs.append(o * (1.0 / den))
    o_ref[...] = jnp.concatenate(outs, axis=-1).astype(o_ref.dtype)


def _attention_meta(q, k, v, bkt_self, rel_bias, sinks, e):
    vmem = pl.BlockSpec(memory_space=pltpu.VMEM)
    smem = pl.BlockSpec(memory_space=pltpu.SMEM)
    return pl.pallas_call(
        functools.partial(_attn_meta_kernel, e),
        in_specs=[vmem, vmem, vmem, vmem, smem, smem],
        out_specs=vmem,
        out_shape=jax.ShapeDtypeStruct(q.shape, jnp.bfloat16),
        name="meta_attention",
    )(q, k, v, bkt_self, rel_bias, sinks)


def _t5_bucket(dist):
    n = jnp.maximum(dist, 0)
    max_exact = REL_BUCKETS // 2
    nf = jnp.maximum(n, 1).astype(jnp.float32)
    large = max_exact + (jnp.log(nf / max_exact) / math.log(REL_MAX_DIST / max_exact)
                         * (REL_BUCKETS - max_exact)).astype(jnp.int32)
    large = jnp.minimum(large, REL_BUCKETS - 1)
    return jnp.where(n < max_exact, n, large)


def _bucket_tables(seq):
    bucket = lambda dist: jnp.where(dist >= 0, _t5_bucket(dist), -1).astype(jnp.int32)
    idx = jnp.arange(BLOCK)
    comb = bucket((idx[None, :] - idx[:, None]) % BLOCK)
    m = jnp.arange(N_META)
    pos = N_META + jnp.arange(seq).reshape(seq // BLOCK, 1, BLOCK)
    meta = bucket(pos - m[None, :, None]).reshape(seq // BLOCK * N_META, BLOCK)
    self_ = bucket(m[:, None] - m[None, :])
    return comb, meta, self_


def _outproj_rows(x, ya, yc, wo_ref, g_ref):
    mix = _mm(ya, wo_ref[0:A_WIDTH, :]) + _mm(yc, wo_ref[A_WIDTH:, :])
    return x + _rms(mix, g_ref[3:4, :])


def _outproj_ffn_kernel(layer, with_meta, h_ref, ya_ref, yc_ref, hm_ref, yam_ref, ycm_ref, g_ref, wo_ref,
                        w1_hbm, w2_hbm, o_ref, om_ref, *scr):
    w1, w2 = scr[:2]

    @pl.when(pl.program_id(0) == 0)
    def _():
        _load_ffn_weights(w1_hbm, w2_hbm, layer, 1, *scr)
        if with_meta:
            xm = _outproj_rows(hm_ref[...], yam_ref[...], ycm_ref[...], wo_ref, g_ref)
            om_ref[...] = _ffn_rows(xm, g_ref, 4, 5, w1, w2)
        else:
            om_ref[...] = hm_ref[...]

    def front(rows):
        o_ref[rows, :] = _outproj_rows(h_ref[rows, :], ya_ref[rows, :], yc_ref[rows, :], wo_ref, g_ref)

    fronts = [[functools.partial(front, rows)] for rows in _sub_blocks(o_ref.shape[0])]
    _ffn_tile(lambda rows: o_ref[rows, :], o_ref, g_ref, 4, 5, w1, w2, fronts)


def _outproj_ffn(h, ya, yc, hm, yam, ycm, mix_w_out, norm_g, ffn_w1, ffn_w2, layer, e, with_meta):
    n = h.shape[0]
    tm = FFN_ROW_TILE
    row = lambda w: pl.BlockSpec((tm, w), lambda i: (i, 0))
    any_spec = pl.BlockSpec(memory_space=pl.ANY)
    return pl.pallas_call(
        functools.partial(_outproj_ffn_kernel, layer, with_meta),
        grid=(n // tm,),
        in_specs=[row(D_MODEL), row(A_WIDTH), row(B_WIDTH), _const_spec(hm.shape), _const_spec(yam.shape),
                  _const_spec(ycm.shape), _slab_spec(norm_g, layer), _slab_spec(mix_w_out, e),
                  any_spec, any_spec],
        out_specs=[row(D_MODEL), pl.BlockSpec(hm.shape, lambda i: (0, 0))],
        out_shape=[jax.ShapeDtypeStruct(h.shape, h.dtype), jax.ShapeDtypeStruct(hm.shape, hm.dtype)],
        scratch_shapes=_ffn_scratch(),
        compiler_params=_params(1),
        name="outproj_ffn",
    )(h, ya, yc, hm, yam, ycm, norm_g, mix_w_out, ffn_w1, ffn_w2)


def _pool_fronts(x_ref, hist, t0, g_ref, pw_ref, ps_ref, u_scr, lvl_scr, out_ref):
    tm = x_ref.shape[0]
    gpre = g_ref[2:3, :]
    top = POOL_PAD + POOL_HALO
    sub = min(tm, FFN_SUB_ROWS)

    def prenorm(r):
        if r == 0:
            zero_pad = jnp.zeros((POOL_PAD, D_MODEL), jnp.float32)
            u_scr[0:POOL_PAD, :] = zero_pad
            u_scr[POOL_PAD:top, :] = _rms(hist, gpre)
            for slot in range(2):
                lvl_scr[slot, 0:POOL_PAD, :] = zero_pad[:, :POOL_GROUP_DIM]
        u_scr[top + r:top + r + sub, :] = _rms(x_ref[r:r + sub, :], gpre)

    def group(r, gi):
        w = POOL_SIZES[gi]
        lo, hi = POOL_PAD + r, top + r + sub
        cols = slice(gi * POOL_GROUP_DIM, (gi + 1) * POOL_GROUP_DIM)
        src, src_cols, d = u_scr, cols, 1
        while d < w:
            s = src[lo:hi, src_cols] + src[lo - d:hi - d, src_cols]
            d *= 2
            if d < w:
                dst = lvl_scr.at[(d.bit_length() - 1) % 2]
                dst[lo:hi, :] = s
                src, src_cols = dst, slice(None)
        ug = u_scr[top + r:top + r + sub, cols]
        t = t0 + r + jax.lax.broadcasted_iota(jnp.int32, (sub, 1), 0)
        cnt = jnp.minimum(t + 1, w).astype(jnp.float32)
        diff = (s[POOL_HALO:, :] / cnt - ug).astype(jnp.bfloat16)
        out_ref[r:r + sub, cols] = _mm(diff, pw_ref[gi]) * ps_ref[:, cols]

    def finish(r):
        rows = slice(r, r + sub)
        out_ref[rows, :] = x_ref[rows, :] + _rms(out_ref[rows, :], g_ref[3:4, :])

    return [[functools.partial(prenorm, r)]
            + [functools.partial(group, r, gi) for gi in range(N_POOL_GROUPS)]
            + [functools.partial(finish, r)] for r in range(0, tm, sub)]


def _pool_ffn_kernel(layer, tiles_per_seq, with_meta, h_ref, prev_ref, hm_ref, g_ref, pw_ref, ps_ref,
                     w1_hbm, w2_hbm, o_ref, om_ref, *scr):
    w1, w2 = scr[:2]
    u_scr, lvl_scr = scr[-2:]

    @pl.when(pl.program_id(0) == 0)
    def _():
        _load_ffn_weights(w1_hbm, w2_hbm, layer, 1, *scr[:-2])
        if with_meta:
            no_hist = jnp.zeros((POOL_HALO, D_MODEL), jnp.float32)
            for thunk in _pool_fronts(hm_ref, no_hist, 0, g_ref, pw_ref, ps_ref, u_scr, lvl_scr, om_ref)[0]:
                thunk()
            om_ref[...] = _ffn_rows(om_ref[...], g_ref, 4, 5, w1, w2)
        else:
            om_ref[...] = hm_ref[...]

    j = pl.program_id(0) % tiles_per_seq
    hist = jnp.where(j == 0, hm_ref[...], prev_ref[...])
    fronts = _pool_fronts(h_ref, hist, N_META + j * h_ref.shape[0], g_ref, pw_ref, ps_ref, u_scr, lvl_scr,
                          o_ref)
    _ffn_tile(lambda rows: o_ref[rows, :], o_ref, g_ref, 4, 5, w1, w2, fronts)


def _pool_ffn(h, hm, norm_g, pool_w, pool_scale, ffn_w1, ffn_w2, layer, o, seq, with_meta):
    n = h.shape[0]
    tm = FFN_ROW_TILE
    halo_blocks = tm // POOL_HALO
    row = pl.BlockSpec((tm, D_MODEL), lambda i: (i, 0))
    prev = pl.BlockSpec((POOL_HALO, D_MODEL), lambda i: (jnp.maximum(i * halo_blocks - 1, 0), 0))
    any_spec = pl.BlockSpec(memory_space=pl.ANY)
    return pl.pallas_call(
        functools.partial(_pool_ffn_kernel, layer, seq // tm, with_meta),
        grid=(n // tm,),
        in_specs=[row, prev, _const_spec(hm.shape), _slab_spec(norm_g, layer), _slab_spec(pool_w, o),
                  pl.BlockSpec((1, D_MODEL), lambda i: (o, 0), pipeline_mode=pl.Buffered(1)),
                  any_spec, any_spec],
        out_specs=[row, pl.BlockSpec(hm.shape, lambda i: (0, 0))],
        out_shape=[jax.ShapeDtypeStruct(h.shape, h.dtype), jax.ShapeDtypeStruct(hm.shape, hm.dtype)],
        scratch_shapes=_ffn_scratch() + [
            pltpu.VMEM((POOL_PAD + POOL_HALO + tm, D_MODEL), jnp.float32),
            pltpu.VMEM((2, POOL_PAD + POOL_HALO + tm, POOL_GROUP_DIM), jnp.float32)],
        compiler_params=_params(1),
        name="pool_ffn",
    )(h, h, hm, norm_g, pool_w, pool_scale, ffn_w1, ffn_w2)


def kernel(x, meta_tokens, rel_bias, norm_g, ffn_w1, ffn_w2, mix_w_in, mix_w_out, attn_sinks,
           conv_w, pool_w, pool_scale):
    batch, seq, _ = x.shape
    depth = norm_g.shape[0]
    h = x.reshape(batch * seq, D_MODEL)
    hm = meta_tokens.astype(x.dtype)
    bkt_comb, bkt_meta, bkt_self = _bucket_tables(seq)
    for layer in range(depth):
        last = layer == depth - 1
        h, hm = _ffn(h, hm, norm_g, ffn_w1, ffn_w2, layer, 0, True)
        if layer % 2 == 0:
            e = layer // 2
            no_hist = jnp.zeros((CONV_HALO, B_WIDTH), jnp.float32)
            qm, km, vm, ycm, tail_m = _inproj(hm, norm_g, mix_w_in, conv_w, no_hist, layer, e, N_META, False)
            q, k, vt, yc, _ = _inproj(h, norm_g, mix_w_in, conv_w, tail_m, layer, e, seq, True)
            ya = _attention(q, k, vt, km, vm.T, bkt_comb, bkt_meta, rel_bias, attn_sinks, e, batch, seq)
            yam = _attention_meta(qm, km, vm, bkt_self, rel_bias, attn_sinks, e) if not last else ycm
            h, hm = _outproj_ffn(h, ya, yc, hm, yam, ycm, mix_w_out, norm_g, ffn_w1, ffn_w2, layer, e,
                                 not last)
        else:
            h, hm = _pool_ffn(h, hm, norm_g, pool_w, pool_scale, ffn_w1, ffn_w2, layer, layer // 2, seq,
                              not last)
    return h.reshape(batch, seq, D_MODEL)
```

```python
import functools
import math

import jax
import jax.numpy as jnp
from jax.experimental import pallas as pl
from jax.experimental.pallas import tpu as pltpu

D_MODEL = 1024
N_META = 16
A_HEADS = 8
A_KV_HEADS = 2
A_GROUP = A_HEADS // A_KV_HEADS
HEAD_DIM = 64
WINDOW = 128
BLOCK = 128
A_WIDTH = A_HEADS * HEAD_DIM
KV_WIDTH = A_KV_HEADS * HEAD_DIM
B_WIDTH = D_MODEL // 2
CONV_WIDTH = 3
IN_WIDTH = A_WIDTH + 2 * KV_WIDTH + 3 * B_WIDTH
POOL_SIZES = (2, 4, 8, 16)
N_POOL_GROUPS = 4
POOL_GROUP_DIM = D_MODEL // N_POOL_GROUPS
REL_BUCKETS = 32
REL_MAX_DIST = 128
D_FF = 2816
HALF_STEP = 0.5
RMS_EPS = 1e-6

VMEM_LIMIT_BYTES_V7X = 56 * 1024 * 1024
ROW_TILE = 512
FFN_ROW_TILE = 512
FFN_SUB_ROWS = 512
FF_CHUNK = 256
N_FF_CHUNKS = D_FF // FF_CHUNK
W1_STAGE_ROWS = 64
W2_STAGE_ROWS = 256
POOL_HALO = 16
POOL_PAD = 8
CONV_HALO = 8
ATTN_QBLOCKS = 4
GROUP_LANES = A_GROUP * BLOCK

NEG_INF = float("-inf")
_NT = (((1,), (1,)), ((), ()))

assert WINDOW == BLOCK and N_META == POOL_HALO


def _rms(x, g):
    return x * jax.lax.rsqrt(jnp.mean(x * x, axis=-1, keepdims=True) + RMS_EPS) * g


def _const_spec(shape):
    zeros = (0,) * len(shape)
    return pl.BlockSpec(shape, lambda *_: zeros, pipeline_mode=pl.Buffered(1))


def _slab_spec(arr, *lead):
    rest = arr.shape[len(lead):]
    idx = tuple(lead) + (0,) * len(rest)
    return pl.BlockSpec((None,) * len(lead) + rest, lambda *_: idx, pipeline_mode=pl.Buffered(1))


def _mm(a, w):
    return jax.lax.dot_general(a, w, (((1,), (0,)), ((), ())), preferred_element_type=jnp.float32)


def _params(n_axes):
    return pltpu.CompilerParams(dimension_semantics=("arbitrary",) * n_axes,
                                vmem_limit_bytes=VMEM_LIMIT_BYTES_V7X)


def _load_bf16(src, dst, stage, sem, chunk):
    rows = src.shape[0]
    copy = lambda k: pltpu.make_async_copy(src.at[pl.ds(k * chunk, chunk), :], stage.at[k % 2], sem.at[k % 2])
    n = rows // chunk
    copy(0).start()
    for k in range(n):
        if k + 1 < n:
            copy(k + 1).start()
        copy(k).wait()
        dst[k * chunk:(k + 1) * chunk, :] = stage[k % 2].astype(dst.dtype)


def _swiglu(xn, w1, w2, side=()):
    side = list(side)
    acc = jnp.zeros((xn.shape[0], D_MODEL), jnp.float32)
    for c in range(N_FF_CHUNKS):
        cols = slice(c * FF_CHUNK, (c + 1) * FF_CHUNK)
        gate = _mm(xn, w1[:, cols])
        up = _mm(xn, w1[:, D_FF + c * FF_CHUNK:D_FF + (c + 1) * FF_CHUNK])
        act = (gate * (1.0 / (1.0 + jnp.exp(-gate))) * up).astype(jnp.bfloat16)
        acc = acc + _mm(act, w2[cols, :])
        if side:
            side.pop(0)()
    for thunk in side:
        thunk()
    return acc


def _ffn_rows(x, g_ref, pre, post, w1, w2):
    xn = _rms(x, g_ref[pre:pre + 1, :]).astype(jnp.bfloat16)
    return x + HALF_STEP * _rms(_swiglu(xn, w1, w2), g_ref[post:post + 1, :])


def _ffn_scratch():
    return [pltpu.VMEM((D_MODEL, 2 * D_FF), jnp.bfloat16),
            pltpu.VMEM((D_FF, D_MODEL), jnp.bfloat16),
            pltpu.VMEM((2, W1_STAGE_ROWS, 2 * D_FF), jnp.float32),
            pltpu.VMEM((2, W2_STAGE_ROWS, D_MODEL), jnp.float32),
            pltpu.SemaphoreType.DMA((2,)), pltpu.SemaphoreType.DMA((2,))]


def _load_ffn_weights(w1_hbm, w2_hbm, layer, half, w1, w2, stage1, stage2, sem1, sem2):
    _load_bf16(w1_hbm.at[layer, half], w1, stage1, sem1, W1_STAGE_ROWS)
    _load_bf16(w2_hbm.at[layer, half], w2, stage2, sem2, W2_STAGE_ROWS)


def _sub_blocks(n_rows):
    return [slice(r, r + FFN_SUB_ROWS) for r in range(0, n_rows, FFN_SUB_ROWS)]


def _ffn_tile(read_x, o_ref, g_ref, pre, post, w1, w2, fronts=None):
    subs = _sub_blocks(o_ref.shape[0])
    fronts = fronts or [[] for _ in subs]
    prenorm = lambda k: _rms(read_x(subs[k]), g_ref[pre:pre + 1, :]).astype(jnp.bfloat16)
    xns, accs = {}, {}

    def finish(k):
        o_ref[subs[k], :] = read_x(subs[k]) + HALF_STEP * _rms(accs[k], g_ref[post:post + 1, :])

    for thunk in fronts[0]:
        thunk()
    xns[0] = prenorm(0)
    for k in range(len(subs)):
        side = []
        if k > 0:
            side.append(functools.partial(finish, k - 1))
        if k + 1 < len(subs):
            side += fronts[k + 1]
            side.append(lambda k=k: xns.__setitem__(k + 1, prenorm(k + 1)))
        accs[k] = _swiglu(xns[k], w1, w2, side)
    finish(len(subs) - 1)


def _ffn_kernel(layer, half, with_meta, h_ref, hm_ref, g_ref, w1_hbm, w2_hbm, o_ref, om_ref, *scr):
    pre, post = 4 * half, 4 * half + 1
    w1, w2 = scr[:2]

    @pl.when(pl.program_id(0) == 0)
    def _():
        _load_ffn_weights(w1_hbm, w2_hbm, layer, half, *scr)
        if with_meta:
            om_ref[...] = _ffn_rows(hm_ref[...], g_ref, pre, post, w1, w2)
        else:
            om_ref[...] = hm_ref[...]

    _ffn_tile(lambda rows: h_ref[rows, :], o_ref, g_ref, pre, post, w1, w2)


def _ffn(h, hm, norm_g, ffn_w1, ffn_w2, layer, half, with_meta):
    n = h.shape[0]
    tm = FFN_ROW_TILE
    row = pl.BlockSpec((tm, D_MODEL), lambda i: (i, 0))
    any_spec = pl.BlockSpec(memory_space=pl.ANY)
    return pl.pallas_call(
        functools.partial(_ffn_kernel, layer, half, with_meta),
        grid=(n // tm,),
        in_specs=[row, _const_spec(hm.shape), _slab_spec(norm_g, layer), any_spec, any_spec],
        out_specs=[row, pl.BlockSpec(hm.shape, lambda i: (0, 0))],
        out_shape=[jax.ShapeDtypeStruct(h.shape, h.dtype), jax.ShapeDtypeStruct(hm.shape, hm.dtype)],
        scratch_shapes=_ffn_scratch(),
        compiler_params=_params(1),
        name="ffn",
    )(h, hm, norm_g, ffn_w1, ffn_w2)


def _inproj_kernel(tiles_per_seq, transpose_v, h_ref, g_ref, win_ref, convw_ref, hist_ref,
                   q_ref, k_ref, v_ref, yc_ref, tail_ref, uc_scr):
    i = pl.program_id(0)
    tm = h_ref.shape[0]
    u = _rms(h_ref[...], g_ref[2:3, :]).astype(jnp.bfloat16)
    z = _mm(u, win_ref[...])
    q_ref[...] = (z[:, :A_WIDTH] * (HEAD_DIM ** -0.5)).astype(jnp.bfloat16)
    k_ref[...] = z[:, A_WIDTH:A_WIDTH + KV_WIDTH].astype(jnp.bfloat16)
    v = z[:, A_WIDTH + KV_WIDTH:A_WIDTH + 2 * KV_WIDTH]
    v_ref[...] = (v.T if transpose_v else v).astype(jnp.bfloat16)
    o = A_WIDTH + 2 * KV_WIDTH
    b_gate = z[:, o:o + B_WIDTH]
    c_gate = z[:, o + B_WIDTH:o + 2 * B_WIDTH]
    x_in = z[:, o + 2 * B_WIDTH:o + 3 * B_WIDTH]
    uc = c_gate * x_in

    @pl.when(i % tiles_per_seq == 0)
    def _():
        uc_scr[0:CONV_HALO, :] = hist_ref[...]

    uc_scr[CONV_HALO:CONV_HALO + tm, :] = uc
    y = convw_ref[0:1, :] * uc
    for j in range(1, CONV_WIDTH):
        y = y + convw_ref[j:j + 1, :] * uc_scr[CONV_HALO - j:CONV_HALO - j + tm, :]
    yc_ref[...] = (b_gate * y).astype(jnp.bfloat16)
    tail = uc[tm - CONV_HALO:, :]
    tail_ref[...] = tail
    uc_scr[0:CONV_HALO, :] = tail


def _inproj(h, norm_g, mix_w_in, conv_w, hist, layer, e, rows_per_seq, transpose_v):
    n = h.shape[0]
    tm = min(ROW_TILE, n)
    row = lambda w: pl.BlockSpec((tm, w), lambda i: (i, 0))
    bf = lambda *shape: jax.ShapeDtypeStruct(shape, jnp.bfloat16)
    if transpose_v:
        v_spec, v_shape = pl.BlockSpec((KV_WIDTH, tm), lambda i: (0, i)), bf(KV_WIDTH, n)
    else:
        v_spec, v_shape = row(KV_WIDTH), bf(n, KV_WIDTH)
    return pl.pallas_call(
        functools.partial(_inproj_kernel, rows_per_seq // tm, transpose_v),
        grid=(n // tm,),
        in_specs=[row(D_MODEL), _slab_spec(norm_g, layer), _slab_spec(mix_w_in, e),
                  _slab_spec(conv_w, e), _const_spec((CONV_HALO, B_WIDTH))],
        out_specs=[row(A_WIDTH), row(KV_WIDTH), v_spec, row(B_WIDTH),
                   pl.BlockSpec((CONV_HALO, B_WIDTH), lambda i: (0, 0))],
        out_shape=[bf(n, A_WIDTH), bf(n, KV_WIDTH), v_shape, bf(n, B_WIDTH),
                   jax.ShapeDtypeStruct((CONV_HALO, B_WIDTH), jnp.float32)],
        scratch_shapes=[pltpu.VMEM((CONV_HALO + tm, B_WIDTH), jnp.float32)],
        compiler_params=_params(1),
        name="inproj_conv",
    )(h, norm_g, mix_w_in, conv_w, hist)


def _bias_from_buckets(bucket, rel_ref, head):
    body = lambda b, acc: jnp.where(bucket == b, rel_ref[b, head], acc)
    acc = jax.lax.fori_loop(0, REL_BUCKETS, body, jnp.zeros(bucket.shape, jnp.float32))
    return jnp.where(bucket >= 0, acc, NEG_INF)


def _attn_kernel(e, q_ref, kp_ref, kc_ref, vtp_ref, vtc_ref, km_ref, vtm_ref, bktc_ref, bktm_ref,
                 rel_ref, sink_ref, o_ref, bias_comb_scr, bias_meta_scr, sink_scr):
    j = pl.program_id(1)
    key_idx = jax.lax.broadcasted_iota(jnp.int32, (BLOCK, GROUP_LANES), 0)
    qry_idx = jax.lax.broadcasted_iota(jnp.int32, (BLOCK, GROUP_LANES), 1) % BLOCK
    from_prev = key_idx > qry_idx

    @pl.when(jnp.logical_and(pl.program_id(0) == 0, j == 0))
    def _():
        bktc = bktc_ref[...]
        bktm = bktm_ref[...]
        for h in range(A_HEADS):
            g, i = divmod(h, A_GROUP)
            lanes = slice(i * BLOCK, (i + 1) * BLOCK)
            bias = _bias_from_buckets(bktc, rel_ref, h)
            bias_comb_scr[0, g, :, lanes] = bias
            bias_comb_scr[1, g, :, lanes] = jnp.where(from_prev[:, :BLOCK], NEG_INF, bias)
            bias_meta_scr[:, h * BLOCK:(h + 1) * BLOCK] = _bias_from_buckets(bktm, rel_ref, h)
            sink_scr[g, :, lanes] = jnp.full((8, BLOCK), sink_ref[e, h], jnp.float32)

    kcat = jnp.concatenate([kp_ref[...], kc_ref[...]], axis=0)
    vtcat = jnp.concatenate([vtp_ref[...], vtc_ref[...]], axis=1)
    km = km_ref[...]
    vtm = vtm_ref[...]
    for qb in range(ATTN_QBLOCKS):
        q = q_ref[qb * BLOCK:(qb + 1) * BLOCK, :]
        variant = jnp.where(j == 0, 1, 0) if qb == 0 else 0
        meta_row = pl.multiple_of((j * ATTN_QBLOCKS + qb) * N_META, N_META)
        outs = []
        for g in range(A_KV_HEADS):
            dims = slice(g * HEAD_DIM, (g + 1) * HEAD_DIM)
            q_stack = jnp.concatenate(
                [q[:, (g * A_GROUP + i) * HEAD_DIM:(g * A_GROUP + i + 1) * HEAD_DIM]
                 for i in range(A_GROUP)], axis=0)
            k_all = jnp.concatenate([kcat[qb * BLOCK:(qb + 2) * BLOCK, dims], km[:, dims]], axis=0)
            st = jax.lax.dot_general(k_all, q_stack, _NT, preferred_element_type=jnp.float32)
            comb = jnp.where(from_prev, st[0:BLOCK], st[BLOCK:2 * BLOCK]) + bias_comb_scr[variant, g]
            meta = st[2 * BLOCK:] + bias_meta_scr[pl.ds(meta_row, N_META),
                                                  g * GROUP_LANES:(g + 1) * GROUP_LANES]
            sink = sink_scr[g, 0:1, :]
            m = jnp.maximum(jnp.maximum(jnp.max(comb, axis=0, keepdims=True),
                                        jnp.max(meta, axis=0, keepdims=True)), sink)
            pc = jnp.exp(comb - m)
            pm = jnp.exp(meta - m)
            den = (jnp.sum(pc, axis=0, keepdims=True) + jnp.sum(pm, axis=0, keepdims=True)
                   + jnp.exp(sink - m))
            p_band = jnp.concatenate([jnp.where(from_prev, pc, 0.0), jnp.where(from_prev, 0.0, pc)],
                                     axis=0).astype(jnp.bfloat16)
            ot = jnp.dot(vtcat[dims, qb * BLOCK:(qb + 2) * BLOCK], p_band,
                         preferred_element_type=jnp.float32)
            ot = ot + jnp.dot(vtm[dims, :], pm.astype(jnp.bfloat16), preferred_element_type=jnp.float32)
            ot = ot * (1.0 / den)
            outs += [ot[:, i * BLOCK:(i + 1) * BLOCK] for i in range(A_GROUP)]
        yt = jnp.concatenate(outs, axis=0)
        o_ref[qb * BLOCK:(qb + 1) * BLOCK, :] = yt.T.astype(o_ref.dtype)


def _attention(q, k, vt, k_meta, vt_meta, bkt_comb, bkt_meta, rel_bias, sinks, e, batch, seq):
    nb = seq // BLOCK
    qb = ATTN_QBLOCKS
    tiles = nb // qb
    cur_rows = lambda w: pl.BlockSpec((qb * BLOCK, w), lambda b, j: (b * tiles + j, 0))
    prev_blk = lambda b, j: b * nb + jnp.maximum(j * qb - 1, 0)
    smem = pl.BlockSpec(memory_space=pltpu.SMEM)
    return pl.pallas_call(
        functools.partial(_attn_kernel, e),
        grid=(batch, tiles),
        in_specs=[cur_rows(A_WIDTH),
                  pl.BlockSpec((BLOCK, KV_WIDTH), lambda b, j: (prev_blk(b, j), 0)),
                  cur_rows(KV_WIDTH),
                  pl.BlockSpec((KV_WIDTH, BLOCK), lambda b, j: (0, prev_blk(b, j))),
                  pl.BlockSpec((KV_WIDTH, qb * BLOCK), lambda b, j: (0, b * tiles + j)),
                  _const_spec(k_meta.shape), _const_spec(vt_meta.shape),
                  _const_spec(bkt_comb.shape), _const_spec(bkt_meta.shape), smem, smem],
        out_specs=cur_rows(A_WIDTH),
        out_shape=jax.ShapeDtypeStruct(q.shape, jnp.bfloat16),
        scratch_shapes=[pltpu.VMEM((2, A_KV_HEADS, BLOCK, GROUP_LANES), jnp.float32),
                        pltpu.VMEM((nb * N_META, A_HEADS * BLOCK), jnp.float32),
                        pltpu.VMEM((A_KV_HEADS, 8, GROUP_LANES), jnp.float32)],
        compiler_params=_params(2),
        name="swa_attention",
    )(q, k, k, vt, vt, k_meta, vt_meta, bkt_comb, bkt_meta, rel_bias, sinks)


def _attn_meta_kernel(e, q_ref, k_ref, v_ref, bkt_ref, rel_ref, sink_ref, o_ref):
    q = q_ref[...]
    k = k_ref[...]
    v = v_ref[...]
    bkt = bkt_ref[...]
    outs = []
    for h in range(A_HEADS):
        dims = slice((h // A_GROUP) * HEAD_DIM, (h // A_GROUP + 1) * HEAD_DIM)
        s = jax.lax.dot_general(q[:, h * HEAD_DIM:(h + 1) * HEAD_DIM], k[:, dims], _NT,
                                preferred_element_type=jnp.float32)
        s = s + _bias_from_buckets(bkt, rel_ref, h)
        sink = sink_ref[e, h]
        m = jnp.maximum(jnp.max(s, axis=-1, keepdims=True), sink)
        p = jnp.exp(s - m)
        den = jnp.sum(p, axis=-1, keepdims=True) + jnp.exp(sink - m)
        o = jnp.dot(p.astype(jnp.bfloat16), v[:, dims], preferred_element_type=jnp.float32)
        outs.append(o * (1.0 / den))
    o_ref[...] = jnp.concatenate(outs, axis=-1).astype(o_ref.dtype)


def _attention_meta(q, k, v, bkt_self, rel_bias, sinks, e):
    vmem = pl.BlockSpec(memory_space=pltpu.VMEM)
    smem = pl.BlockSpec(memory_space=pltpu.SMEM)
    return pl.pallas_call(
        functools.partial(_attn_meta_kernel, e),
        in_specs=[vmem, vmem, vmem, vmem, smem, smem],
        out_specs=vmem,
        out_shape=jax.ShapeDtypeStruct(q.shape, jnp.bfloat16),
        name="meta_attention",
    )(q, k, v, bkt_self, rel_bias, sinks)


def _t5_bucket(dist):
    n = jnp.maximum(dist, 0)
    max_exact = REL_BUCKETS // 2
    nf = jnp.maximum(n, 1).astype(jnp.float32)
    large = max_exact + (jnp.log(nf / max_exact) / math.log(REL_MAX_DIST / max_exact)
                         * (REL_BUCKETS - max_exact)).astype(jnp.int32)
    large = jnp.minimum(large, REL_BUCKETS - 1)
    return jnp.where(n < max_exact, n, large)


def _bucket_tables(seq):
    bucket = lambda dist: jnp.where(dist >= 0, _t5_bucket(dist), -1).astype(jnp.int32)
    idx = jnp.arange(BLOCK)
    comb = bucket((idx[None, :] - idx[:, None]) % BLOCK)
    m = jnp.arange(N_META)
    pos = N_META + jnp.arange(seq).reshape(seq // BLOCK, 1, BLOCK)
    meta = bucket(pos - m[None, :, None]).reshape(seq // BLOCK * N_META, BLOCK)
    self_ = bucket(m[:, None] - m[None, :])
    return comb, meta, self_


def _outproj_rows(x, ya, yc, wo_ref, g_ref):
    mix = _mm(ya, wo_ref[0:A_WIDTH, :]) + _mm(yc, wo_ref[A_WIDTH:, :])
    return x + _rms(mix, g_ref[3:4, :])


def _outproj_ffn_kernel(layer, with_meta, h_ref, ya_ref, yc_ref, hm_ref, yam_ref, ycm_ref, g_ref, wo_ref,
                        w1_hbm, w2_hbm, o_ref, om_ref, *scr):
    w1, w2 = scr[:2]

    @pl.when(pl.program_id(0) == 0)
    def _():
        _load_ffn_weights(w1_hbm, w2_hbm, layer, 1, *scr)
        if with_meta:
            xm = _outproj_rows(hm_ref[...], yam_ref[...], ycm_ref[...], wo_ref, g_ref)
            om_ref[...] = _ffn_rows(xm, g_ref, 4, 5, w1, w2)
        else:
            om_ref[...] = hm_ref[...]

    def front(rows):
        o_ref[rows, :] = _outproj_rows(h_ref[rows, :], ya_ref[rows, :], yc_ref[rows, :], wo_ref, g_ref)

    fronts = [[functools.partial(front, rows)] for rows in _sub_blocks(o_ref.shape[0])]
    _ffn_tile(lambda rows: o_ref[rows, :], o_ref, g_ref, 4, 5, w1, w2, fronts)


def _outproj_ffn(h, ya, yc, hm, yam, ycm, mix_w_out, norm_g, ffn_w1, ffn_w2, layer, e, with_meta):
    n = h.shape[0]
    tm = FFN_ROW_TILE
    row = lambda w: pl.BlockSpec((tm, w), lambda i: (i, 0))
    any_spec = pl.BlockSpec(memory_space=pl.ANY)
    return pl.pallas_call(
        functools.partial(_outproj_ffn_kernel, layer, with_meta),
        grid=(n // tm,),
        in_specs=[row(D_MODEL), row(A_WIDTH), row(B_WIDTH), _const_spec(hm.shape), _const_spec(yam.shape),
                  _const_spec(ycm.shape), _slab_spec(norm_g, layer), _slab_spec(mix_w_out, e),
                  any_spec, any_spec],
        out_specs=[row(D_MODEL), pl.BlockSpec(hm.shape, lambda i: (0, 0))],
        out_shape=[jax.ShapeDtypeStruct(h.shape, h.dtype), jax.ShapeDtypeStruct(hm.shape, hm.dtype)],
        scratch_shapes=_ffn_scratch(),
        compiler_params=_params(1),
        name="outproj_ffn",
    )(h, ya, yc, hm, yam, ycm, norm_g, mix_w_out, ffn_w1, ffn_w2)


def _pool_fronts(x_ref, hist, t0, g_ref, pw_ref, ps_ref, u_scr, lvl_scr, out_ref):
    tm = x_ref.shape[0]
    gpre = g_ref[2:3, :]
    top = POOL_PAD + POOL_HALO
    sub = min(tm, FFN_SUB_ROWS)

    def prenorm(r):
        if r == 0:
            zero_pad = jnp.zeros((POOL_PAD, D_MODEL), jnp.float32)
            u_scr[0:POOL_PAD, :] = zero_pad
            u_scr[POOL_PAD:top, :] = _rms(hist, gpre)
            for slot in range(2):
                lvl_scr[slot, 0:POOL_PAD, :] = zero_pad[:, :POOL_GROUP_DIM]
        u_scr[top + r:top + r + sub, :] = _rms(x_ref[r:r + sub, :], gpre)

    def group(r, gi):
        w = POOL_SIZES[gi]
        lo, hi = POOL_PAD + r, top + r + sub
        cols = slice(gi * POOL_GROUP_DIM, (gi + 1) * POOL_GROUP_DIM)
        src, src_cols, d = u_scr, cols, 1
        while d < w:
            s = src[lo:hi, src_cols] + src[lo - d:hi - d, src_cols]
            d *= 2
            if d < w:
                dst = lvl_scr.at[(d.bit_length() - 1) % 2]
                dst[lo:hi, :] = s
                src, src_cols = dst, slice(None)
        ug = u_scr[top + r:top + r + sub, cols]
        t = t0 + r + jax.lax.broadcasted_iota(jnp.int32, (sub, 1), 0)
        cnt = jnp.minimum(t + 1, w).astype(jnp.float32)
        diff = (s[POOL_HALO:, :] / cnt - ug).astype(jnp.bfloat16)
        out_ref[r:r + sub, cols] = _mm(diff, pw_ref[gi]) * ps_ref[:, cols]

    def finish(r):
        rows = slice(r, r + sub)
        out_ref[rows, :] = x_ref[rows, :] + _rms(out_ref[rows, :], g_ref[3:4, :])

    return [[functools.partial(prenorm, r)]
            + [functools.partial(group, r, gi) for gi in range(N_POOL_GROUPS)]
            + [functools.partial(finish, r)] for r in range(0, tm, sub)]


def _pool_ffn_kernel(layer, tiles_per_seq, with_meta, h_ref, prev_ref, hm_ref, g_ref, pw_ref, ps_ref,
                     w1_hbm, w2_hbm, o_ref, om_ref, *scr):
    w1, w2 = scr[:2]
    u_scr, lvl_scr = scr[-2:]

    @pl.when(pl.program_id(0) == 0)
    def _():
        _load_ffn_weights(w1_hbm, w2_hbm, layer, 1, *scr[:-2])
        if with_meta:
            no_hist = jnp.zeros((POOL_HALO, D_MODEL), jnp.float32)
            for thunk in _pool_fronts(hm_ref, no_hist, 0, g_ref, pw_ref, ps_ref, u_scr, lvl_scr, om_ref)[0]:
                thunk()
            om_ref[...] = _ffn_rows(om_ref[...], g_ref, 4, 5, w1, w2)
        else:
            om_ref[...] = hm_ref[...]

    j = pl.program_id(0) % tiles_per_seq
    hist = jnp.where(j == 0, hm_ref[...], prev_ref[...])
    fronts = _pool_fronts(h_ref, hist, N_META + j * h_ref.shape[0], g_ref, pw_ref, ps_ref, u_scr, lvl_scr,
                          o_ref)
    _ffn_tile(lambda rows: o_ref[rows, :], o_ref, g_ref, 4, 5, w1, w2, fronts)


def _pool_ffn(h, hm, norm_g, pool_w, pool_scale, ffn_w1, ffn_w2, layer, o, seq, with_meta):
    n = h.shape[0]
    tm = FFN_ROW_TILE
    halo_blocks = tm // POOL_HALO
    row = pl.BlockSpec((tm, D_MODEL), lambda i: (i, 0))
    prev = pl.BlockSpec((POOL_HALO, D_MODEL), lambda i: (jnp.maximum(i * halo_blocks - 1, 0), 0))
    any_spec = pl.BlockSpec(memory_space=pl.ANY)
    return pl.pallas_call(
        functools.partial(_pool_ffn_kernel, layer, seq // tm, with_meta),
        grid=(n // tm,),
        in_specs=[row, prev, _const_spec(hm.shape), _slab_spec(norm_g, layer), _slab_spec(pool_w, o),
                  pl.BlockSpec((1, D_MODEL), lambda i: (o, 0), pipeline_mode=pl.Buffered(1)),
                  any_spec, any_spec],
        out_specs=[row, pl.BlockSpec(hm.shape, lambda i: (0, 0))],
        out_shape=[jax.ShapeDtypeStruct(h.shape, h.dtype), jax.ShapeDtypeStruct(hm.shape, hm.dtype)],
        scratch_shapes=_ffn_scratch() + [
            pltpu.VMEM((POOL_PAD + POOL_HALO + tm, D_MODEL), jnp.float32),
            pltpu.VMEM((2, POOL_PAD + POOL_HALO + tm, POOL_GROUP_DIM), jnp.float32)],
        compiler_params=_params(1),
        name="pool_ffn",
    )(h, h, hm, norm_g, pool_w, pool_scale, ffn_w1, ffn_w2)


def kernel(x, meta_tokens, rel_bias, norm_g, ffn_w1, ffn_w2, mix_w_in, mix_w_out, attn_sinks,
           conv_w, pool_w, pool_scale):
    batch, seq, _ = x.shape
    depth = norm_g.shape[0]
    h = x.reshape(batch * seq, D_MODEL)
    hm = meta_tokens.astype(x.dtype)
    bkt_comb, bkt_meta, bkt_self = _bucket_tables(seq)
    for layer in range(depth):
        last = layer == depth - 1
        h, hm = _ffn(h, hm, norm_g, ffn_w1, ffn_w2, layer, 0, True)
        if layer % 2 == 0:
            e = layer // 2
            no_hist = jnp.zeros((CONV_HALO, B_WIDTH), jnp.float32)
            qm, km, vm, ycm, tail_m = _inproj(hm, norm_g, mix_w_in, conv_w, no_hist, layer, e, N_META, False)
            q, k, vt, yc, _ = _inproj(h, norm_g, mix_w_in, conv_w, tail_m, layer, e, seq, True)
            ya = _attention(q, k, vt, km, vm.T, bkt_comb, bkt_meta, rel_bias, attn_sinks, e, batch, seq)
            yam = _attention_meta(qm, km, vm, bkt_self, rel_bias, attn_sinks, e) if not last else ycm
            h, hm = _outproj_ffn(h, ya, yc, hm, yam, ycm, mix_w_out, norm_g, ffn_w1, ffn_w2, layer, e,
                                 not last)
        else:
            h, hm = _pool_ffn(h, hm, norm_g, pool_w, pool_scale, ffn_w1, ffn_w2, layer, layer // 2, seq,
                              not last)
    return h.reshape(batch, seq, D_MODEL)
```

```python
import functools
import math

import jax
import jax.numpy as jnp
from jax.experimental import pallas as pl
from jax.experimental.pallas import tpu as pltpu

D_MODEL = 1024
N_META = 16
A_HEADS = 8
A_KV_HEADS = 2
A_GROUP = A_HEADS // A_KV_HEADS
HEAD_DIM = 64
WINDOW = 128
BLOCK = 128
A_WIDTH = A_HEADS * HEAD_DIM
KV_WIDTH = A_KV_HEADS * HEAD_DIM
B_WIDTH = D_MODEL // 2
CONV_WIDTH = 3
IN_WIDTH = A_WIDTH + 2 * KV_WIDTH + 3 * B_WIDTH
POOL_SIZES = (2, 4, 8, 16)
N_POOL_GROUPS = 4
POOL_GROUP_DIM = D_MODEL // N_POOL_GROUPS
REL_BUCKETS = 32
REL_MAX_DIST = 128
D_FF = 2816
HALF_STEP = 0.5
RMS_EPS = 1e-6

VMEM_LIMIT_BYTES_V7X = 56 * 1024 * 1024
ROW_TILE = 512
FFN_ROW_TILE = 1024
FFN_SUB_ROWS = 512
FF_CHUNK = 256
N_FF_CHUNKS = D_FF // FF_CHUNK
W1_STAGE_ROWS = 64
W2_STAGE_ROWS = 256
POOL_HALO = 16
POOL_PAD = 8
CONV_HALO = 8
ATTN_QBLOCKS = 4
GROUP_LANES = A_GROUP * BLOCK

NEG_INF = float("-inf")
_NT = (((1,), (1,)), ((), ()))

assert WINDOW == BLOCK and N_META == POOL_HALO


def _rms(x, g):
    return x * jax.lax.rsqrt(jnp.mean(x * x, axis=-1, keepdims=True) + RMS_EPS) * g


def _const_spec(shape):
    zeros = (0,) * len(shape)
    return pl.BlockSpec(shape, lambda *_: zeros, pipeline_mode=pl.Buffered(1))


def _slab_spec(arr, *lead):
    rest = arr.shape[len(lead):]
    idx = tuple(lead) + (0,) * len(rest)
    return pl.BlockSpec((None,) * len(lead) + rest, lambda *_: idx, pipeline_mode=pl.Buffered(1))


def _mm(a, w):
    return jax.lax.dot_general(a, w, (((1,), (0,)), ((), ())), preferred_element_type=jnp.float32)


def _params(n_axes):
    return pltpu.CompilerParams(dimension_semantics=("arbitrary",) * n_axes,
                                vmem_limit_bytes=VMEM_LIMIT_BYTES_V7X)


def _load_bf16(src, dst, stage, sem, chunk):
    rows = src.shape[0]
    copy = lambda k: pltpu.make_async_copy(src.at[pl.ds(k * chunk, chunk), :], stage.at[k % 2], sem.at[k % 2])
    n = rows // chunk
    copy(0).start()
    for k in range(n):
        if k + 1 < n:
            copy(k + 1).start()
        copy(k).wait()
        dst[k * chunk:(k + 1) * chunk, :] = stage[k % 2].astype(dst.dtype)


def _swiglu(xn, w1, w2, side=()):
    side = list(side)
    acts = []
    for c in range(N_FF_CHUNKS):
        cols = slice(c * FF_CHUNK, (c + 1) * FF_CHUNK)
        gate = _mm(xn, w1[:, cols])
        up = _mm(xn, w1[:, D_FF + c * FF_CHUNK:D_FF + (c + 1) * FF_CHUNK])
        acts.append((gate * (1.0 / (1.0 + jnp.exp(-gate))) * up).astype(jnp.bfloat16))
        if side:
            side.pop(0)()
    for thunk in side:
        thunk()
    return _mm(jnp.concatenate(acts, axis=1), w2[...])


def _ffn_rows(x, g_ref, pre, post, w1, w2):
    xn = _rms(x, g_ref[pre:pre + 1, :]).astype(jnp.bfloat16)
    return x + HALF_STEP * _rms(_swiglu(xn, w1, w2), g_ref[post:post + 1, :])


def _ffn_scratch():
    return [pltpu.VMEM((D_MODEL, 2 * D_FF), jnp.bfloat16),
            pltpu.VMEM((D_FF, D_MODEL), jnp.bfloat16),
            pltpu.VMEM((2, W1_STAGE_ROWS, 2 * D_FF), jnp.float32),
            pltpu.VMEM((2, W2_STAGE_ROWS, D_MODEL), jnp.float32),
            pltpu.SemaphoreType.DMA((2,)), pltpu.SemaphoreType.DMA((2,))]


def _load_ffn_weights(w1_hbm, w2_hbm, layer, half, w1, w2, stage1, stage2, sem1, sem2):
    _load_bf16(w1_hbm.at[layer, half], w1, stage1, sem1, W1_STAGE_ROWS)
    _load_bf16(w2_hbm.at[layer, half], w2, stage2, sem2, W2_STAGE_ROWS)


def _sub_blocks(n_rows):
    return [slice(r, r + FFN_SUB_ROWS) for r in range(0, n_rows, FFN_SUB_ROWS)]


def _ffn_tile(read_x, o_ref, g_ref, pre, post, w1, w2, fronts=None):
    subs = _sub_blocks(o_ref.shape[0])
    fronts = fronts or [[] for _ in subs]
    prenorm = lambda k: _rms(read_x(subs[k]), g_ref[pre:pre + 1, :]).astype(jnp.bfloat16)
    xns, accs = {}, {}

    def finish(k):
        o_ref[subs[k], :] = read_x(subs[k]) + HALF_STEP * _rms(accs[k], g_ref[post:post + 1, :])

    for thunk in fronts[0]:
        thunk()
    xns[0] = prenorm(0)
    for k in range(len(subs)):
        side = []
        if k > 0:
            side.append(functools.partial(finish, k - 1))
        if k + 1 < len(subs):
            side += fronts[k + 1]
            side.append(lambda k=k: xns.__setitem__(k + 1, prenorm(k + 1)))
        accs[k] = _swiglu(xns[k], w1, w2, side)
    finish(len(subs) - 1)


def _ffn_kernel(layer, half, with_meta, h_ref, hm_ref, g_ref, w1_hbm, w2_hbm, o_ref, om_ref, *scr):
    pre, post = 4 * half, 4 * half + 1
    w1, w2 = scr[:2]

    @pl.when(pl.program_id(0) == 0)
    def _():
        _load_ffn_weights(w1_hbm, w2_hbm, layer, half, *scr)
        if with_meta:
            om_ref[...] = _ffn_rows(hm_ref[...], g_ref, pre, post, w1, w2)
        else:
            om_ref[...] = hm_ref[...]

    _ffn_tile(lambda rows: h_ref[rows, :], o_ref, g_ref, pre, post, w1, w2)


def _ffn(h, hm, norm_g, ffn_w1, ffn_w2, layer, half, with_meta):
    n = h.shape[0]
    tm = FFN_ROW_TILE
    row = pl.BlockSpec((tm, D_MODEL), lambda i: (i, 0))
    any_spec = pl.BlockSpec(memory_space=pl.ANY)
    return pl.pallas_call(
        functools.partial(_ffn_kernel, layer, half, with_meta),
        grid=(n // tm,),
        in_specs=[row, _const_spec(hm.shape), _slab_spec(norm_g, layer), any_spec, any_spec],
        out_specs=[row, pl.BlockSpec(hm.shape, lambda i: (0, 0))],
        out_shape=[jax.ShapeDtypeStruct(h.shape, h.dtype), jax.ShapeDtypeStruct(hm.shape, hm.dtype)],
        scratch_shapes=_ffn_scratch(),
        compiler_params=_params(1),
        name="ffn",
    )(h, hm, norm_g, ffn_w1, ffn_w2)


def _inproj_kernel(tiles_per_seq, transpose_v, h_ref, g_ref, win_ref, convw_ref, hist_ref,
                   q_ref, k_ref, v_ref, yc_ref, tail_ref, uc_scr):
    i = pl.program_id(0)
    tm = h_ref.shape[0]
    u = _rms(h_ref[...], g_ref[2:3, :]).astype(jnp.bfloat16)
    z = _mm(u, win_ref[...])
    q_ref[...] = (z[:, :A_WIDTH] * (HEAD_DIM ** -0.5)).astype(jnp.bfloat16)
    k_ref[...] = z[:, A_WIDTH:A_WIDTH + KV_WIDTH].astype(jnp.bfloat16)
    v = z[:, A_WIDTH + KV_WIDTH:A_WIDTH + 2 * KV_WIDTH]
    v_ref[...] = (v.T if transpose_v else v).astype(jnp.bfloat16)
    o = A_WIDTH + 2 * KV_WIDTH
    b_gate = z[:, o:o + B_WIDTH]
    c_gate = z[:, o + B_WIDTH:o + 2 * B_WIDTH]
    x_in = z[:, o + 2 * B_WIDTH:o + 3 * B_WIDTH]
    uc = c_gate * x_in

    @pl.when(i % tiles_per_seq == 0)
    def _():
        uc_scr[0:CONV_HALO, :] = hist_ref[...]

    uc_scr[CONV_HALO:CONV_HALO + tm, :] = uc
    y = convw_ref[0:1, :] * uc
    for j in range(1, CONV_WIDTH):
        y = y + convw_ref[j:j + 1, :] * uc_scr[CONV_HALO - j:CONV_HALO - j + tm, :]
    yc_ref[...] = (b_gate * y).astype(jnp.bfloat16)
    tail = uc[tm - CONV_HALO:, :]
    tail_ref[...] = tail
    uc_scr[0:CONV_HALO, :] = tail


def _inproj(h, norm_g, mix_w_in, conv_w, hist, layer, e, rows_per_seq, transpose_v):
    n = h.shape[0]
    tm = min(ROW_TILE, n)
    row = lambda w: pl.BlockSpec((tm, w), lambda i: (i, 0))
    bf = lambda *shape: jax.ShapeDtypeStruct(shape, jnp.bfloat16)
    if transpose_v:
        v_spec, v_shape = pl.BlockSpec((KV_WIDTH, tm), lambda i: (0, i)), bf(KV_WIDTH, n)
    else:
        v_spec, v_shape = row(KV_WIDTH), bf(n, KV_WIDTH)
    return pl.pallas_call(
        functools.partial(_inproj_kernel, rows_per_seq // tm, transpose_v),
        grid=(n // tm,),
        in_specs=[row(D_MODEL), _slab_spec(norm_g, layer), _slab_spec(mix_w_in, e),
                  _slab_spec(conv_w, e), _const_spec((CONV_HALO, B_WIDTH))],
        out_specs=[row(A_WIDTH), row(KV_WIDTH), v_spec, row(B_WIDTH),
                   pl.BlockSpec((CONV_HALO, B_WIDTH), lambda i: (0, 0))],
        out_shape=[bf(n, A_WIDTH), bf(n, KV_WIDTH), v_shape, bf(n, B_WIDTH),
                   jax.ShapeDtypeStruct((CONV_HALO, B_WIDTH), jnp.float32)],
        scratch_shapes=[pltpu.VMEM((CONV_HALO + tm, B_WIDTH), jnp.float32)],
        compiler_params=_params(1),
        name="inproj_conv",
    )(h, norm_g, mix_w_in, conv_w, hist)


def _bias_from_buckets(bucket, rel_ref, head):
    body = lambda b, acc: jnp.where(bucket == b, rel_ref[b, head], acc)
    acc = jax.lax.fori_loop(0, REL_BUCKETS, body, jnp.zeros(bucket.shape, jnp.float32))
    return jnp.where(bucket >= 0, acc, NEG_INF)


def _attn_kernel(e, q_ref, kp_ref, kc_ref, vtp_ref, vtc_ref, km_ref, vtm_ref, bktc_ref, bktm_ref,
                 rel_ref, sink_ref, o_ref, bias_comb_scr, bias_meta_scr, sink_scr):
    j = pl.program_id(1)
    key_idx = jax.lax.broadcasted_iota(jnp.int32, (BLOCK, GROUP_LANES), 0)
    qry_idx = jax.lax.broadcasted_iota(jnp.int32, (BLOCK, GROUP_LANES), 1) % BLOCK
    from_prev = key_idx > qry_idx

    @pl.when(jnp.logical_and(pl.program_id(0) == 0, j == 0))
    def _():
        bktc = bktc_ref[...]
        bktm = bktm_ref[...]
        for h in range(A_HEADS):
            g, i = divmod(h, A_GROUP)
            lanes = slice(i * BLOCK, (i + 1) * BLOCK)
            bias = _bias_from_buckets(bktc, rel_ref, h)
            bias_comb_scr[0, g, :, lanes] = bias
            bias_comb_scr[1, g, :, lanes] = jnp.where(from_prev[:, :BLOCK], NEG_INF, bias)
            bias_meta_scr[:, h * BLOCK:(h + 1) * BLOCK] = _bias_from_buckets(bktm, rel_ref, h)
            sink_scr[g, :, lanes] = jnp.full((8, BLOCK), sink_ref[e, h], jnp.float32)

    kcat = jnp.concatenate([kp_ref[...], kc_ref[...]], axis=0)
    vtcat = jnp.concatenate([vtp_ref[...], vtc_ref[...]], axis=1)
    km = km_ref[...]
    vtm = vtm_ref[...]
    for qb in range(ATTN_QBLOCKS):
        q = q_ref[qb * BLOCK:(qb + 1) * BLOCK, :]
        variant = jnp.where(j == 0, 1, 0) if qb == 0 else 0
        meta_row = pl.multiple_of((j * ATTN_QBLOCKS + qb) * N_META, N_META)
        outs = []
        for g in range(A_KV_HEADS):
            dims = slice(g * HEAD_DIM, (g + 1) * HEAD_DIM)
            q_stack = jnp.concatenate(
                [q[:, (g * A_GROUP + i) * HEAD_DIM:(g * A_GROUP + i + 1) * HEAD_DIM]
                 for i in range(A_GROUP)], axis=0)
            k_all = jnp.concatenate([kcat[qb * BLOCK:(qb + 2) * BLOCK, dims], km[:, dims]], axis=0)
            st = jax.lax.dot_general(k_all, q_stack, _NT, preferred_element_type=jnp.float32)
            comb = jnp.where(from_prev, st[0:BLOCK], st[BLOCK:2 * BLOCK]) + bias_comb_scr[variant, g]
            meta = st[2 * BLOCK:] + bias_meta_scr[pl.ds(meta_row, N_META),
                                                  g * GROUP_LANES:(g + 1) * GROUP_LANES]
            sink = sink_scr[g, 0:1, :]
            m = jnp.maximum(jnp.maximum(jnp.max(comb, axis=0, keepdims=True),
                                        jnp.max(meta, axis=0, keepdims=True)), sink)
            pc = jnp.exp(comb - m)
            pm = jnp.exp(meta - m)
            den = (jnp.sum(pc, axis=0, keepdims=True) + jnp.sum(pm, axis=0, keepdims=True)
                   + jnp.exp(sink - m))
            p_band = jnp.concatenate([jnp.where(from_prev, pc, 0.0), jnp.where(from_prev, 0.0, pc)],
                                     axis=0).astype(jnp.bfloat16)
            ot = jnp.dot(vtcat[dims, qb * BLOCK:(qb + 2) * BLOCK], p_band,
                         preferred_element_type=jnp.float32)
            ot = ot + jnp.dot(vtm[dims, :], pm.astype(jnp.bfloat16), preferred_element_type=jnp.float32)
            ot = ot * (1.0 / den)
            outs += [ot[:, i * BLOCK:(i + 1) * BLOCK] for i in range(A_GROUP)]
        yt = jnp.concatenate(outs, axis=0)
        o_ref[qb * BLOCK:(qb + 1) * BLOCK, :] = yt.T.astype(o_ref.dtype)


def _attention(q, k, vt, k_meta, vt_meta, bkt_comb, bkt_meta, rel_bias, sinks, e, batch, seq):
    nb = seq // BLOCK
    qb = ATTN_QBLOCKS
    tiles = nb // qb
    cur_rows = lambda w: pl.BlockSpec((qb * BLOCK, w), lambda b, j: (b * tiles + j, 0))
    prev_blk = lambda b, j: b * nb + jnp.maximum(j * qb - 1, 0)
    smem = pl.BlockSpec(memory_space=pltpu.SMEM)
    return pl.pallas_call(
        functools.partial(_attn_kernel, e),
        grid=(batch, tiles),
        in_specs=[cur_rows(A_WIDTH),
                  pl.BlockSpec((BLOCK, KV_WIDTH), lambda b, j: (prev_blk(b, j), 0)),
                  cur_rows(KV_WIDTH),
                  pl.BlockSpec((KV_WIDTH, BLOCK), lambda b, j: (0, prev_blk(b, j))),
                  pl.BlockSpec((KV_WIDTH, qb * BLOCK), lambda b, j: (0, b * tiles + j)),
                  _const_spec(k_meta.shape), _const_spec(vt_meta.shape),
                  _const_spec(bkt_comb.shape), _const_spec(bkt_meta.shape), smem, smem],
        out_specs=cur_rows(A_WIDTH),
        out_shape=jax.ShapeDtypeStruct(q.shape, jnp.bfloat16),
        scratch_shapes=[pltpu.VMEM((2, A_KV_HEADS, BLOCK, GROUP_LANES), jnp.float32),
                        pltpu.VMEM((nb * N_META, A_HEADS * BLOCK), jnp.float32),
                        pltpu.VMEM((A_KV_HEADS, 8, GROUP_LANES), jnp.float32)],
        compiler_params=_params(2),
        name="swa_attention",
    )(q, k, k, vt, vt, k_meta, vt_meta, bkt_comb, bkt_meta, rel_bias, sinks)


def _attn_meta_kernel(e, q_ref, k_ref, v_ref, bkt_ref, rel_ref, sink_ref, o_ref):
    q = q_ref[...]
    k = k_ref[...]
    v = v_ref[...]
    bkt = bkt_ref[...]
    outs = []
    for h in range(A_HEADS):
        dims = slice((h // A_GROUP) * HEAD_DIM, (h // A_GROUP + 1) * HEAD_DIM)
        s = jax.lax.dot_general(q[:, h * HEAD_DIM:(h + 1) * HEAD_DIM], k[:, dims], _NT,
                                preferred_element_type=jnp.float32)
        s = s + _bias_from_buckets(bkt, rel_ref, h)
        sink = sink_ref[e, h]
        m = jnp.maximum(jnp.max(s, axis=-1, keepdims=True), sink)
        p = jnp.exp(s - m)
        den = jnp.sum(p, axis=-1, keepdims=True) + jnp.exp(sink - m)
        o = jnp.dot(p.astype(jnp.bfloat16), v[:, dims], preferred_element_type=jnp.float32)
        outs.append(o * (1.0 / den))
    o_ref[...] = jnp.concatenate(outs, axis=-1).astype(o_ref.dtype)


def _attention_meta(q, k, v, bkt_self, rel_bias, sinks, e):
    vmem = pl.BlockSpec(memory_space=pltpu.VMEM)
    smem = pl.BlockSpec(memory_space=pltpu.SMEM)
    return pl.pallas_call(
        functools.partial(_attn_meta_kernel, e),
        in_specs=[vmem, vmem, vmem, vmem, smem, smem],
        out_specs=vmem,
        out_shape=jax.ShapeDtypeStruct(q.shape, jnp.bfloat16),
        name="meta_attention",
    )(q, k, v, bkt_self, rel_bias, sinks)


def _t5_bucket(dist):
    n = jnp.maximum(dist, 0)
    max_exact = REL_BUCKETS // 2
    nf = jnp.maximum(n, 1).astype(jnp.float32)
    large = max_exact + (jnp.log(nf / max_exact) / math.log(REL_MAX_DIST / max_exact)
                         * (REL_BUCKETS - max_exact)).astype(jnp.int32)
    large = jnp.minimum(large, REL_BUCKETS - 1)
    return jnp.where(n < max_exact, n, large)


def _bucket_tables(seq):
    bucket = lambda dist: jnp.where(dist >= 0, _t5_bucket(dist), -1).astype(jnp.int32)
    idx = jnp.arange(BLOCK)
    comb = bucket((idx[None, :] - idx[:, None]) % BLOCK)
    m = jnp.arange(N_META)
    pos = N_META + jnp.arange(seq).reshape(seq // BLOCK, 1, BLOCK)
    meta = bucket(pos - m[None, :, None]).reshape(seq // BLOCK * N_META, BLOCK)
    self_ = bucket(m[:, None] - m[None, :])
    return comb, meta, self_


def _outproj_rows(x, ya, yc, wo_ref, g_ref):
    mix = _mm(ya, wo_ref[0:A_WIDTH, :]) + _mm(yc, wo_ref[A_WIDTH:, :])
    return x + _rms(mix, g_ref[3:4, :])


def _outproj_ffn_kernel(layer, with_meta, h_ref, ya_ref, yc_ref, hm_ref, yam_ref, ycm_ref, g_ref, wo_ref,
                        w1_hbm, w2_hbm, o_ref, om_ref, *scr):
    w1, w2 = scr[:2]

    @pl.when(pl.program_id(0) == 0)
    def _():
        _load_ffn_weights(w1_hbm, w2_hbm, layer, 1, *scr)
        if with_meta:
            xm = _outproj_rows(hm_ref[...], yam_ref[...], ycm_ref[...], wo_ref, g_ref)
            om_ref[...] = _ffn_rows(xm, g_ref, 4, 5, w1, w2)
        else:
            om_ref[...] = hm_ref[...]

    def front(rows):
        o_ref[rows, :] = _outproj_rows(h_ref[rows, :], ya_ref[rows, :], yc_ref[rows, :], wo_ref, g_ref)

    fronts = [[functools.partial(front, rows)] for rows in _sub_blocks(o_ref.shape[0])]
    _ffn_tile(lambda rows: o_ref[rows, :], o_ref, g_ref, 4, 5, w1, w2, fronts)


def _outproj_ffn(h, ya, yc, hm, yam, ycm, mix_w_out, norm_g, ffn_w1, ffn_w2, layer, e, with_meta):
    n = h.shape[0]
    tm = FFN_ROW_TILE
    row = lambda w: pl.BlockSpec((tm, w), lambda i: (i, 0))
    any_spec = pl.BlockSpec(memory_space=pl.ANY)
    return pl.pallas_call(
        functools.partial(_outproj_ffn_kernel, layer, with_meta),
        grid=(n // tm,),
        in_specs=[row(D_MODEL), row(A_WIDTH), row(B_WIDTH), _const_spec(hm.shape), _const_spec(yam.shape),
                  _const_spec(ycm.shape), _slab_spec(norm_g, layer), _slab_spec(mix_w_out, e),
                  any_spec, any_spec],
        out_specs=[row(D_MODEL), pl.BlockSpec(hm.shape, lambda i: (0, 0))],
        out_shape=[jax.ShapeDtypeStruct(h.shape, h.dtype), jax.ShapeDtypeStruct(hm.shape, hm.dtype)],
        scratch_shapes=_ffn_scratch(),
        compiler_params=_params(1),
        name="outproj_ffn",
    )(h, ya, yc, hm, yam, ycm, norm_g, mix_w_out, ffn_w1, ffn_w2)


def _pool_fronts(x_ref, hist, t0, g_ref, pw_ref, ps_ref, u_scr, lvl_scr, out_ref):
    tm = x_ref.shape[0]
    gpre = g_ref[2:3, :]
    top = POOL_PAD + POOL_HALO
    sub = min(tm, FFN_SUB_ROWS)

    def prenorm(r):
        if r == 0:
            zero_pad = jnp.zeros((POOL_PAD, D_MODEL), jnp.float32)
            u_scr[0:POOL_PAD, :] = zero_pad
            u_scr[POOL_PAD:top, :] = _rms(hist, gpre)
            for slot in range(2):
                lvl_scr[slot, 0:POOL_PAD, :] = zero_pad[:, :POOL_GROUP_DIM]
        u_scr[top + r:top + r + sub, :] = _rms(x_ref[r:r + sub, :], gpre)

    def group(r, gi):
        w = POOL_SIZES[gi]
        lo, hi = POOL_PAD + r, top + r + sub
        cols = slice(gi * POOL_GROUP_DIM, (gi + 1) * POOL_GROUP_DIM)
        src, src_cols, d = u_scr, cols, 1
        while d < w:
            s = src[lo:hi, src_cols] + src[lo - d:hi - d, src_cols]
            d *= 2
            if d < w:
                dst = lvl_scr.at[(d.bit_length() - 1) % 2]
                dst[lo:hi, :] = s
                src, src_cols = dst, slice(None)
        ug = u_scr[top + r:top + r + sub, cols]
        t = t0 + r + jax.lax.broadcasted_iota(jnp.int32, (sub, 1), 0)
        cnt = jnp.minimum(t + 1, w).astype(jnp.float32)
        diff = (s[POOL_HALO:, :] / cnt - ug).astype(jnp.bfloat16)
        out_ref[r:r + sub, cols] = _mm(diff, pw_ref[gi]) * ps_ref[:, cols]

    def finish(r):
        rows = slice(r, r + sub)
        out_ref[rows, :] = x_ref[rows, :] + _rms(out_ref[rows, :], g_ref[3:4, :])

    return [[functools.partial(prenorm, r)]
            + [functools.partial(group, r, gi) for gi in range(N_POOL_GROUPS)]
            + [functools.partial(finish, r)] for r in range(0, tm, sub)]


def _pool_ffn_kernel(layer, tiles_per_seq, with_meta, h_ref, prev_ref, hm_ref, g_ref, pw_ref, ps_ref,
                     w1_hbm, w2_hbm, o_ref, om_ref, *scr):
    w1, w2 = scr[:2]
    u_scr, lvl_scr = scr[-2:]

    @pl.when(pl.program_id(0) == 0)
    def _():
        _load_ffn_weights(w1_hbm, w2_hbm, layer, 1, *scr[:-2])
        if with_meta:
            no_hist = jnp.zeros((POOL_HALO, D_MODEL), jnp.float32)
            for thunk in _pool_fronts(hm_ref, no_hist, 0, g_ref, pw_ref, ps_ref, u_scr, lvl_scr, om_ref)[0]:
                thunk()
            om_ref[...] = _ffn_rows(om_ref[...], g_ref, 4, 5, w1, w2)
        else:
            om_ref[...] = hm_ref[...]

    j = pl.program_id(0) % tiles_per_seq
    hist = jnp.where(j == 0, hm_ref[...], prev_ref[...])
    fronts = _pool_fronts(h_ref, hist, N_META + j * h_ref.shape[0], g_ref, pw_ref, ps_ref, u_scr, lvl_scr,
                          o_ref)
    _ffn_tile(lambda rows: o_ref[rows, :], o_ref, g_ref, 4, 5, w1, w2, fronts)


def _pool_ffn(h, hm, norm_g, pool_w, pool_scale, ffn_w1, ffn_w2, layer, o, seq, with_meta):
    n = h.shape[0]
    tm = FFN_ROW_TILE
    halo_blocks = tm // POOL_HALO
    row = pl.BlockSpec((tm, D_MODEL), lambda i: (i, 0))
    prev = pl.BlockSpec((POOL_HALO, D_MODEL), lambda i: (jnp.maximum(i * halo_blocks - 1, 0), 0))
    any_spec = pl.BlockSpec(memory_space=pl.ANY)
    return pl.pallas_call(
        functools.partial(_pool_ffn_kernel, layer, seq // tm, with_meta),
        grid=(n // tm,),
        in_specs=[row, prev, _const_spec(hm.shape), _slab_spec(norm_g, layer), _slab_spec(pool_w, o),
                  pl.BlockSpec((1, D_MODEL), lambda i: (o, 0), pipeline_mode=pl.Buffered(1)),
                  any_spec, any_spec],
        out_specs=[row, pl.BlockSpec(hm.shape, lambda i: (0, 0))],
        out_shape=[jax.ShapeDtypeStruct(h.shape, h.dtype), jax.ShapeDtypeStruct(hm.shape, hm.dtype)],
        scratch_shapes=_ffn_scratch() + [
            pltpu.VMEM((POOL_PAD + POOL_HALO + tm, D_MODEL), jnp.float32),
            pltpu.VMEM((2, POOL_PAD + POOL_HALO + tm, POOL_GROUP_DIM), jnp.float32)],
        compiler_params=_params(1),
        name="pool_ffn",
    )(h, h, hm, norm_g, pool_w, pool_scale, ffn_w1, ffn_w2)


def kernel(x, meta_tokens, rel_bias, norm_g, ffn_w1, ffn_w2, mix_w_in, mix_w_out, attn_sinks,
           conv_w, pool_w, pool_scale):
    batch, seq, _ = x.shape
    depth = norm_g.shape[0]
    h = x.reshape(batch * seq, D_MODEL)
    hm = meta_tokens.astype(x.dtype)
    bkt_comb, bkt_meta, bkt_self = _bucket_tables(seq)
    for layer in range(depth):
        last = layer == depth - 1
        h, hm = _ffn(h, hm, norm_g, ffn_w1, ffn_w2, layer, 0, True)
        if layer % 2 == 0:
            e = layer // 2
            no_hist = jnp.zeros((CONV_HALO, B_WIDTH), jnp.float32)
            qm, km, vm, ycm, tail_m = _inproj(hm, norm_g, mix_w_in, conv_w, no_hist, layer, e, N_META, False)
            q, k, vt, yc, _ = _inproj(h, norm_g, mix_w_in, conv_w, tail_m, layer, e, seq, True)
            ya = _attention(q, k, vt, km, vm.T, bkt_comb, bkt_meta, rel_bias, attn_sinks, e, batch, seq)
            yam = _attention_meta(qm, km, vm, bkt_self, rel_bias, attn_sinks, e) if not last else ycm
            h, hm = _outproj_ffn(h, ya, yc, hm, yam, ycm, mix_w_out, norm_g, ffn_w1, ffn_w2, layer, e,
                                 not last)
        else:
            h, hm = _pool_ffn(h, hm, norm_g, pool_w, pool_scale, ffn_w1, ffn_w2, layer, layer // 2, seq,
                              not last)
    return h.reshape(batch, seq, D_MODEL)
```

```python
import functools
import math

import jax
import jax.numpy as jnp
from jax.experimental import pallas as pl
from jax.experimental.pallas import tpu as pltpu

D_MODEL = 1024
N_META = 16
A_HEADS = 8
A_KV_HEADS = 2
A_GROUP = A_HEADS // A_KV_HEADS
HEAD_DIM = 64
WINDOW = 128
BLOCK = 128
A_WIDTH = A_HEADS * HEAD_DIM
KV_WIDTH = A_KV_HEADS * HEAD_DIM
B_WIDTH = D_MODEL // 2
CONV_WIDTH = 3
IN_WIDTH = A_WIDTH + 2 * KV_WIDTH + 3 * B_WIDTH
POOL_SIZES = (2, 4, 8, 16)
N_POOL_GROUPS = 4
POOL_GROUP_DIM = D_MODEL // N_POOL_GROUPS
REL_BUCKETS = 32
REL_MAX_DIST = 128
D_FF = 2816
HALF_STEP = 0.5
RMS_EPS = 1e-6

VMEM_LIMIT_BYTES_V7X = 56 * 1024 * 1024
ROW_TILE = 512
FFN_ROW_TILE = 1024
OUTPROJ_FFN_ROW_TILE = 512
FFN_SUB_ROWS = 512
FF_CHUNK = 256
N_FF_CHUNKS = D_FF // FF_CHUNK
W1_STAGE_ROWS = 64
W2_STAGE_ROWS = 256
POOL_HALO = 16
POOL_PAD = 8
CONV_HALO = 8
ATTN_QBLOCKS = 4
GROUP_LANES = A_GROUP * BLOCK

NEG_INF = float("-inf")
_NT = (((1,), (1,)), ((), ()))

assert WINDOW == BLOCK and N_META == POOL_HALO


def _rms(x, g):
    return x * jax.lax.rsqrt(jnp.mean(x * x, axis=-1, keepdims=True) + RMS_EPS) * g


def _const_spec(shape):
    zeros = (0,) * len(shape)
    return pl.BlockSpec(shape, lambda *_: zeros, pipeline_mode=pl.Buffered(1))


def _slab_spec(arr, *lead):
    rest = arr.shape[len(lead):]
    idx = tuple(lead) + (0,) * len(rest)
    return pl.BlockSpec((None,) * len(lead) + rest, lambda *_: idx, pipeline_mode=pl.Buffered(1))


def _mm(a, w):
    return jax.lax.dot_general(a, w, (((1,), (0,)), ((), ())), preferred_element_type=jnp.float32)


def _params(n_axes):
    return pltpu.CompilerParams(dimension_semantics=("arbitrary",) * n_axes,
                                vmem_limit_bytes=VMEM_LIMIT_BYTES_V7X)


def _load_bf16(src, dst, stage, sem, chunk):
    rows = src.shape[0]
    copy = lambda k: pltpu.make_async_copy(src.at[pl.ds(k * chunk, chunk), :], stage.at[k % 2], sem.at[k % 2])
    n = rows // chunk
    copy(0).start()
    for k in range(n):
        if k + 1 < n:
            copy(k + 1).start()
        copy(k).wait()
        dst[k * chunk:(k + 1) * chunk, :] = stage[k % 2].astype(dst.dtype)


def _swiglu(xn, w1, w2, side=()):
    side = list(side)
    acts = []
    for c in range(N_FF_CHUNKS):
        cols = slice(c * FF_CHUNK, (c + 1) * FF_CHUNK)
        gate = _mm(xn, w1[:, cols])
        up = _mm(xn, w1[:, D_FF + c * FF_CHUNK:D_FF + (c + 1) * FF_CHUNK])
        acts.append((gate * (1.0 / (1.0 + jnp.exp(-gate))) * up).astype(jnp.bfloat16))
        if side:
            side.pop(0)()
    for thunk in side:
        thunk()
    return _mm(jnp.concatenate(acts, axis=1), w2[...])


def _ffn_rows(x, g_ref, pre, post, w1, w2):
    xn = _rms(x, g_ref[pre:pre + 1, :]).astype(jnp.bfloat16)
    return x + HALF_STEP * _rms(_swiglu(xn, w1, w2), g_ref[post:post + 1, :])


def _ffn_scratch():
    return [pltpu.VMEM((D_MODEL, 2 * D_FF), jnp.bfloat16),
            pltpu.VMEM((D_FF, D_MODEL), jnp.bfloat16),
            pltpu.VMEM((2, W1_STAGE_ROWS, 2 * D_FF), jnp.float32),
            pltpu.VMEM((2, W2_STAGE_ROWS, D_MODEL), jnp.float32),
            pltpu.SemaphoreType.DMA((2,)), pltpu.SemaphoreType.DMA((2,))]


def _load_ffn_weights(w1_hbm, w2_hbm, layer, half, w1, w2, stage1, stage2, sem1, sem2):
    _load_bf16(w1_hbm.at[layer, half], w1, stage1, sem1, W1_STAGE_ROWS)
    _load_bf16(w2_hbm.at[layer, half], w2, stage2, sem2, W2_STAGE_ROWS)


def _sub_blocks(n_rows):
    return [slice(r, r + FFN_SUB_ROWS) for r in range(0, n_rows, FFN_SUB_ROWS)]


def _ffn_tile(read_x, o_ref, g_ref, pre, post, w1, w2, fronts=None):
    subs = _sub_blocks(o_ref.shape[0])
    fronts = fronts or [[] for _ in subs]
    prenorm = lambda k: _rms(read_x(subs[k]), g_ref[pre:pre + 1, :]).astype(jnp.bfloat16)
    xns, accs = {}, {}

    def finish(k):
        o_ref[subs[k], :] = read_x(subs[k]) + HALF_STEP * _rms(accs[k], g_ref[post:post + 1, :])

    for thunk in fronts[0]:
        thunk()
    xns[0] = prenorm(0)
    for k in range(len(subs)):
        side = []
        if k > 0:
            side.append(functools.partial(finish, k - 1))
        if k + 1 < len(subs):
            side += fronts[k + 1]
            side.append(lambda k=k: xns.__setitem__(k + 1, prenorm(k + 1)))
        accs[k] = _swiglu(xns[k], w1, w2, side)
    finish(len(subs) - 1)


def _ffn_kernel(layer, half, with_meta, h_ref, hm_ref, g_ref, w1_hbm, w2_hbm, o_ref, om_ref, *scr):
    pre, post = 4 * half, 4 * half + 1
    w1, w2 = scr[:2]

    @pl.when(pl.program_id(0) == 0)
    def _():
        _load_ffn_weights(w1_hbm, w2_hbm, layer, half, *scr)
        if with_meta:
            om_ref[...] = _ffn_rows(hm_ref[...], g_ref, pre, post, w1, w2)
        else:
            om_ref[...] = hm_ref[...]

    _ffn_tile(lambda rows: h_ref[rows, :], o_ref, g_ref, pre, post, w1, w2)


def _ffn(h, hm, norm_g, ffn_w1, ffn_w2, layer, half, with_meta):
    n = h.shape[0]
    tm = FFN_ROW_TILE
    row = pl.BlockSpec((tm, D_MODEL), lambda i: (i, 0))
    any_spec = pl.BlockSpec(memory_space=pl.ANY)
    return pl.pallas_call(
        functools.partial(_ffn_kernel, layer, half, with_meta),
        grid=(n // tm,),
        in_specs=[row, _const_spec(hm.shape), _slab_spec(norm_g, layer), any_spec, any_spec],
        out_specs=[row, pl.BlockSpec(hm.shape, lambda i: (0, 0))],
        out_shape=[jax.ShapeDtypeStruct(h.shape, h.dtype), jax.ShapeDtypeStruct(hm.shape, hm.dtype)],
        scratch_shapes=_ffn_scratch(),
        compiler_params=_params(1),
        name="ffn",
    )(h, hm, norm_g, ffn_w1, ffn_w2)


def _inproj_kernel(tiles_per_seq, transpose_v, h_ref, g_ref, win_ref, convw_ref, hist_ref,
                   q_ref, k_ref, v_ref, yc_ref, tail_ref, uc_scr):
    i = pl.program_id(0)
    tm = h_ref.shape[0]
    u = _rms(h_ref[...], g_ref[2:3, :]).astype(jnp.bfloat16)
    z = _mm(u, win_ref[...])
    q_ref[...] = (z[:, :A_WIDTH] * (HEAD_DIM ** -0.5)).astype(jnp.bfloat16)
    k_ref[...] = z[:, A_WIDTH:A_WIDTH + KV_WIDTH].astype(jnp.bfloat16)
    v = z[:, A_WIDTH + KV_WIDTH:A_WIDTH + 2 * KV_WIDTH]
    v_ref[...] = (v.T if transpose_v else v).astype(jnp.bfloat16)
    o = A_WIDTH + 2 * KV_WIDTH
    b_gate = z[:, o:o + B_WIDTH]
    c_gate = z[:, o + B_WIDTH:o + 2 * B_WIDTH]
    x_in = z[:, o + 2 * B_WIDTH:o + 3 * B_WIDTH]
    uc = c_gate * x_in

    @pl.when(i % tiles_per_seq == 0)
    def _():
        uc_scr[0:CONV_HALO, :] = hist_ref[...]

    uc_scr[CONV_HALO:CONV_HALO + tm, :] = uc
    y = convw_ref[0:1, :] * uc
    for j in range(1, CONV_WIDTH):
        y = y + convw_ref[j:j + 1, :] * uc_scr[CONV_HALO - j:CONV_HALO - j + tm, :]
    yc_ref[...] = (b_gate * y).astype(jnp.bfloat16)
    tail = uc[tm - CONV_HALO:, :]
    tail_ref[...] = tail
    uc_scr[0:CONV_HALO, :] = tail


def _inproj(h, norm_g, mix_w_in, conv_w, hist, layer, e, rows_per_seq, transpose_v):
    n = h.shape[0]
    tm = min(ROW_TILE, n)
    row = lambda w: pl.BlockSpec((tm, w), lambda i: (i, 0))
    bf = lambda *shape: jax.ShapeDtypeStruct(shape, jnp.bfloat16)
    if transpose_v:
        v_spec, v_shape = pl.BlockSpec((KV_WIDTH, tm), lambda i: (0, i)), bf(KV_WIDTH, n)
    else:
        v_spec, v_shape = row(KV_WIDTH), bf(n, KV_WIDTH)
    return pl.pallas_call(
        functools.partial(_inproj_kernel, rows_per_seq // tm, transpose_v),
        grid=(n // tm,),
        in_specs=[row(D_MODEL), _slab_spec(norm_g, layer), _slab_spec(mix_w_in, e),
                  _slab_spec(conv_w, e), _const_spec((CONV_HALO, B_WIDTH))],
        out_specs=[row(A_WIDTH), row(KV_WIDTH), v_spec, row(B_WIDTH),
                   pl.BlockSpec((CONV_HALO, B_WIDTH), lambda i: (0, 0))],
        out_shape=[bf(n, A_WIDTH), bf(n, KV_WIDTH), v_shape, bf(n, B_WIDTH),
                   jax.ShapeDtypeStruct((CONV_HALO, B_WIDTH), jnp.float32)],
        scratch_shapes=[pltpu.VMEM((CONV_HALO + tm, B_WIDTH), jnp.float32)],
        compiler_params=_params(1),
        name="inproj_conv",
    )(h, norm_g, mix_w_in, conv_w, hist)


def _bias_from_buckets(bucket, rel_ref, head):
    body = lambda b, acc: jnp.where(bucket == b, rel_ref[b, head], acc)
    acc = jax.lax.fori_loop(0, REL_BUCKETS, body, jnp.zeros(bucket.shape, jnp.float32))
    return jnp.where(bucket >= 0, acc, NEG_INF)


def _attn_kernel(e, q_ref, kp_ref, kc_ref, vtp_ref, vtc_ref, km_ref, vtm_ref, bktc_ref, bktm_ref,
                 rel_ref, sink_ref, o_ref, bias_comb_scr, bias_meta_scr, sink_scr):
    j = pl.program_id(1)
    key_idx = jax.lax.broadcasted_iota(jnp.int32, (BLOCK, GROUP_LANES), 0)
    qry_idx = jax.lax.broadcasted_iota(jnp.int32, (BLOCK, GROUP_LANES), 1) % BLOCK
    from_prev = key_idx > qry_idx

    @pl.when(jnp.logical_and(pl.program_id(0) == 0, j == 0))
    def _():
        bktc = bktc_ref[...]
        bktm = bktm_ref[...]
        for h in range(A_HEADS):
            g, i = divmod(h, A_GROUP)
            lanes = slice(i * BLOCK, (i + 1) * BLOCK)
            bias = _bias_from_buckets(bktc, rel_ref, h)
            bias_comb_scr[0, g, :, lanes] = bias
            bias_comb_scr[1, g, :, lanes] = jnp.where(from_prev[:, :BLOCK], NEG_INF, bias)
            bias_meta_scr[:, h * BLOCK:(h + 1) * BLOCK] = _bias_from_buckets(bktm, rel_ref, h)
            sink_scr[g, :, lanes] = jnp.full((8, BLOCK), sink_ref[e, h], jnp.float32)

    kcat = jnp.concatenate([kp_ref[...], kc_ref[...]], axis=0)
    vtcat = jnp.concatenate([vtp_ref[...], vtc_ref[...]], axis=1)
    km = km_ref[...]
    vtm = vtm_ref[...]
    for qb in range(ATTN_QBLOCKS):
        q = q_ref[qb * BLOCK:(qb + 1) * BLOCK, :]
        variant = jnp.where(j == 0, 1, 0) if qb == 0 else 0
        meta_row = pl.multiple_of((j * ATTN_QBLOCKS + qb) * N_META, N_META)
        outs = []
        for g in range(A_KV_HEADS):
            dims = slice(g * HEAD_DIM, (g + 1) * HEAD_DIM)
            q_stack = jnp.concatenate(
                [q[:, (g * A_GROUP + i) * HEAD_DIM:(g * A_GROUP + i + 1) * HEAD_DIM]
                 for i in range(A_GROUP)], axis=0)
            k_all = jnp.concatenate([kcat[qb * BLOCK:(qb + 2) * BLOCK, dims], km[:, dims]], axis=0)
            st = jax.lax.dot_general(k_all, q_stack, _NT, preferred_element_type=jnp.float32)
            comb = jnp.where(from_prev, st[0:BLOCK], st[BLOCK:2 * BLOCK]) + bias_comb_scr[variant, g]
            meta = st[2 * BLOCK:] + bias_meta_scr[pl.ds(meta_row, N_META),
                                                  g * GROUP_LANES:(g + 1) * GROUP_LANES]
            sink = sink_scr[g, 0:1, :]
            m = jnp.maximum(jnp.maximum(jnp.max(comb, axis=0, keepdims=True),
                                        jnp.max(meta, axis=0, keepdims=True)), sink)
            pc = jnp.exp(comb - m)
            pm = jnp.exp(meta - m)
            den = (jnp.sum(pc, axis=0, keepdims=True) + jnp.sum(pm, axis=0, keepdims=True)
                   + jnp.exp(sink - m))
            p_band = jnp.concatenate([jnp.where(from_prev, pc, 0.0), jnp.where(from_prev, 0.0, pc)],
                                     axis=0).astype(jnp.bfloat16)
            ot = jnp.dot(vtcat[dims, qb * BLOCK:(qb + 2) * BLOCK], p_band,
                         preferred_element_type=jnp.float32)
            ot = ot + jnp.dot(vtm[dims, :], pm.astype(jnp.bfloat16), preferred_element_type=jnp.float32)
            ot = ot * (1.0 / den)
            outs += [ot[:, i * BLOCK:(i + 1) * BLOCK] for i in range(A_GROUP)]
        yt = jnp.concatenate(outs, axis=0)
        o_ref[qb * BLOCK:(qb + 1) * BLOCK, :] = yt.T.astype(o_ref.dtype)


def _attention(q, k, vt, k_meta, vt_meta, bkt_comb, bkt_meta, rel_bias, sinks, e, batch, seq):
    nb = seq // BLOCK
    qb = ATTN_QBLOCKS
    tiles = nb // qb
    cur_rows = lambda w: pl.BlockSpec((qb * BLOCK, w), lambda b, j: (b * tiles + j, 0))
    prev_blk = lambda b, j: b * nb + jnp.maximum(j * qb - 1, 0)
    smem = pl.BlockSpec(memory_space=pltpu.SMEM)
    return pl.pallas_call(
        functools.partial(_attn_kernel, e),
        grid=(batch, tiles),
        in_specs=[cur_rows(A_WIDTH),
                  pl.BlockSpec((BLOCK, KV_WIDTH), lambda b, j: (prev_blk(b, j), 0)),
                  cur_rows(KV_WIDTH),
                  pl.BlockSpec((KV_WIDTH, BLOCK), lambda b, j: (0, prev_blk(b, j))),
                  pl.BlockSpec((KV_WIDTH, qb * BLOCK), lambda b, j: (0, b * tiles + j)),
                  _const_spec(k_meta.shape), _const_spec(vt_meta.shape),
                  _const_spec(bkt_comb.shape), _const_spec(bkt_meta.shape), smem, smem],
        out_specs=cur_rows(A_WIDTH),
        out_shape=jax.ShapeDtypeStruct(q.shape, jnp.bfloat16),
        scratch_shapes=[pltpu.VMEM((2, A_KV_HEADS, BLOCK, GROUP_LANES), jnp.float32),
                        pltpu.VMEM((nb * N_META, A_HEADS * BLOCK), jnp.float32),
                        pltpu.VMEM((A_KV_HEADS, 8, GROUP_LANES), jnp.float32)],
        compiler_params=_params(2),
        name="swa_attention",
    )(q, k, k, vt, vt, k_meta, vt_meta, bkt_comb, bkt_meta, rel_bias, sinks)


def _attn_meta_kernel(e, q_ref, k_ref, v_ref, bkt_ref, rel_ref, sink_ref, o_ref):
    q = q_ref[...]
    k = k_ref[...]
    v = v_ref[...]
    bkt = bkt_ref[...]
    outs = []
    for h in range(A_HEADS):
        dims = slice((h // A_GROUP) * HEAD_DIM, (h // A_GROUP + 1) * HEAD_DIM)
        s = jax.lax.dot_general(q[:, h * HEAD_DIM:(h + 1) * HEAD_DIM], k[:, dims], _NT,
                                preferred_element_type=jnp.float32)
        s = s + _bias_from_buckets(bkt, rel_ref, h)
        sink = sink_ref[e, h]
        m = jnp.maximum(jnp.max(s, axis=-1, keepdims=True), sink)
        p = jnp.exp(s - m)
        den = jnp.sum(p, axis=-1, keepdims=True) + jnp.exp(sink - m)
        o = jnp.dot(p.astype(jnp.bfloat16), v[:, dims], preferred_element_type=jnp.float32)
        outs.append(o * (1.0 / den))
    o_ref[...] = jnp.concatenate(outs, axis=-1).astype(o_ref.dtype)


def _attention_meta(q, k, v, bkt_self, rel_bias, sinks, e):
    vmem = pl.BlockSpec(memory_space=pltpu.VMEM)
    smem = pl.BlockSpec(memory_space=pltpu.SMEM)
    return pl.pallas_call(
        functools.partial(_attn_meta_kernel, e),
        in_specs=[vmem, vmem, vmem, vmem, smem, smem],
        out_specs=vmem,
        out_shape=jax.ShapeDtypeStruct(q.shape, jnp.bfloat16),
        name="meta_attention",
    )(q, k, v, bkt_self, rel_bias, sinks)


def _t5_bucket(dist):
    n = jnp.maximum(dist, 0)
    max_exact = REL_BUCKETS // 2
    nf = jnp.maximum(n, 1).astype(jnp.float32)
    large = max_exact + (jnp.log(nf / max_exact) / math.log(REL_MAX_DIST / max_exact)
                         * (REL_BUCKETS - max_exact)).astype(jnp.int32)
    large = jnp.minimum(large, REL_BUCKETS - 1)
    return jnp.where(n < max_exact, n, large)


def _bucket_tables(seq):
    bucket = lambda dist: jnp.where(dist >= 0, _t5_bucket(dist), -1).astype(jnp.int32)
    idx = jnp.arange(BLOCK)
    comb = bucket((idx[None, :] - idx[:, None]) % BLOCK)
    m = jnp.arange(N_META)
    pos = N_META + jnp.arange(seq).reshape(seq // BLOCK, 1, BLOCK)
    meta = bucket(pos - m[None, :, None]).reshape(seq // BLOCK * N_META, BLOCK)
    self_ = bucket(m[:, None] - m[None, :])
    return comb, meta, self_


def _outproj_rows(x, ya, yc, wo_ref, g_ref):
    mix = _mm(ya, wo_ref[0:A_WIDTH, :]) + _mm(yc, wo_ref[A_WIDTH:, :])
    return x + _rms(mix, g_ref[3:4, :])


def _outproj_ffn_kernel(layer, with_meta, h_ref, ya_ref, yc_ref, hm_ref, yam_ref, ycm_ref, g_ref, wo_ref,
                        w1_hbm, w2_hbm, o_ref, om_ref, *scr):
    w1, w2 = scr[:2]

    @pl.when(pl.program_id(0) == 0)
    def _():
        _load_ffn_weights(w1_hbm, w2_hbm, layer, 1, *scr)
        if with_meta:
            xm = _outproj_rows(hm_ref[...], yam_ref[...], ycm_ref[...], wo_ref, g_ref)
            om_ref[...] = _ffn_rows(xm, g_ref, 4, 5, w1, w2)
        else:
            om_ref[...] = hm_ref[...]

    def front(rows):
        o_ref[rows, :] = _outproj_rows(h_ref[rows, :], ya_ref[rows, :], yc_ref[rows, :], wo_ref, g_ref)

    fronts = [[functools.partial(front, rows)] for rows in _sub_blocks(o_ref.shape[0])]
    _ffn_tile(lambda rows: o_ref[rows, :], o_ref, g_ref, 4, 5, w1, w2, fronts)


def _outproj_ffn(h, ya, yc, hm, yam, ycm, mix_w_out, norm_g, ffn_w1, ffn_w2, layer, e, with_meta):
    n = h.shape[0]
    tm = OUTPROJ_FFN_ROW_TILE
    row = lambda w: pl.BlockSpec((tm, w), lambda i: (i, 0))
    any_spec = pl.BlockSpec(memory_space=pl.ANY)
    return pl.pallas_call(
        functools.partial(_outproj_ffn_kernel, layer, with_meta),
        grid=(n // tm,),
        in_specs=[row(D_MODEL), row(A_WIDTH), row(B_WIDTH), _const_spec(hm.shape), _const_spec(yam.shape),
                  _const_spec(ycm.shape), _slab_spec(norm_g, layer), _slab_spec(mix_w_out, e),
                  any_spec, any_spec],
        out_specs=[row(D_MODEL), pl.BlockSpec(hm.shape, lambda i: (0, 0))],
        out_shape=[jax.ShapeDtypeStruct(h.shape, h.dtype), jax.ShapeDtypeStruct(hm.shape, hm.dtype)],
        scratch_shapes=_ffn_scratch(),
        compiler_params=_params(1),
        name="outproj_ffn",
    )(h, ya, yc, hm, yam, ycm, norm_g, mix_w_out, ffn_w1, ffn_w2)


def _pool_fronts(x_ref, hist, t0, g_ref, pw_ref, ps_ref, u_scr, lvl_scr, out_ref):
    tm = x_ref.shape[0]
    gpre = g_ref[2:3, :]
    top = POOL_PAD + POOL_HALO
    sub = min(tm, FFN_SUB_ROWS)

    def prenorm(r):
        if r == 0:
            zero_pad = jnp.zeros((POOL_PAD, D_MODEL), jnp.float32)
            u_scr[0:POOL_PAD, :] = zero_pad
            u_scr[POOL_PAD:top, :] = _rms(hist, gpre)
            for slot in range(2):
                lvl_scr[slot, 0:POOL_PAD, :] = zero_pad[:, :POOL_GROUP_DIM]
        u_scr[top + r:top + r + sub, :] = _rms(x_ref[r:r + sub, :], gpre)

    def group(r, gi):
        w = POOL_SIZES[gi]
        lo, hi = POOL_PAD + r, top + r + sub
        cols = slice(gi * POOL_GROUP_DIM, (gi + 1) * POOL_GROUP_DIM)
        src, src_cols, d = u_scr, cols, 1
        while d < w:
            s = src[lo:hi, src_cols] + src[lo - d:hi - d, src_cols]
            d *= 2
            if d < w:
                dst = lvl_scr.at[(d.bit_length() - 1) % 2]
                dst[lo:hi, :] = s
                src, src_cols = dst, slice(None)
        ug = u_scr[top + r:top + r + sub, cols]
        t = t0 + r + jax.lax.broadcasted_iota(jnp.int32, (sub, 1), 0)
        cnt = jnp.minimum(t + 1, w).astype(jnp.float32)
        diff = (s[POOL_HALO:, :] / cnt - ug).astype(jnp.bfloat16)
        out_ref[r:r + sub, cols] = _mm(diff, pw_ref[gi]) * ps_ref[:, cols]

    def finish(r):
        rows = slice(r, r + sub)
        out_ref[rows, :] = x_ref[rows, :] + _rms(out_ref[rows, :], g_ref[3:4, :])

    return [[functools.partial(prenorm, r)]
            + [functools.partial(group, r, gi) for gi in range(N_POOL_GROUPS)]
            + [functools.partial(finish, r)] for r in range(0, tm, sub)]


def _pool_ffn_kernel(layer, tiles_per_seq, with_meta, h_ref, prev_ref, hm_ref, g_ref, pw_ref, ps_ref,
                     w1_hbm, w2_hbm, o_ref, om_ref, *scr):
    w1, w2 = scr[:2]
    u_scr, lvl_scr = scr[-2:]

    @pl.when(pl.program_id(0) == 0)
    def _():
        _load_ffn_weights(w1_hbm, w2_hbm, layer, 1, *scr[:-2])
        if with_meta:
            no_hist = jnp.zeros((POOL_HALO, D_MODEL), jnp.float32)
            for thunk in _pool_fronts(hm_ref, no_hist, 0, g_ref, pw_ref, ps_ref, u_scr, lvl_scr, om_ref)[0]:
                thunk()
            om_ref[...] = _ffn_rows(om_ref[...], g_ref, 4, 5, w1, w2)
        else:
            om_ref[...] = hm_ref[...]

    j = pl.program_id(0) % tiles_per_seq
    hist = jnp.where(j == 0, hm_ref[...], prev_ref[...])
    fronts = _pool_fronts(h_ref, hist, N_META + j * h_ref.shape[0], g_ref, pw_ref, ps_ref, u_scr, lvl_scr,
                          o_ref)
    _ffn_tile(lambda rows: o_ref[rows, :], o_ref, g_ref, 4, 5, w1, w2, fronts)


def _pool_ffn(h, hm, norm_g, pool_w, pool_scale, ffn_w1, ffn_w2, layer, o, seq, with_meta):
    n = h.shape[0]
    tm = FFN_ROW_TILE
    halo_blocks = tm // POOL_HALO
    row = pl.BlockSpec((tm, D_MODEL), lambda i: (i, 0))
    prev = pl.BlockSpec((POOL_HALO, D_MODEL), lambda i: (jnp.maximum(i * halo_blocks - 1, 0), 0))
    any_spec = pl.BlockSpec(memory_space=pl.ANY)
    return pl.pallas_call(
        functools.partial(_pool_ffn_kernel, layer, seq // tm, with_meta),
        grid=(n // tm,),
        in_specs=[row, prev, _const_spec(hm.shape), _slab_spec(norm_g, layer), _slab_spec(pool_w, o),
                  pl.BlockSpec((1, D_MODEL), lambda i: (o, 0), pipeline_mode=pl.Buffered(1)),
                  any_spec, any_spec],
        out_specs=[row, pl.BlockSpec(hm.shape, lambda i: (0, 0))],
        out_shape=[jax.ShapeDtypeStruct(h.shape, h.dtype), jax.ShapeDtypeStruct(hm.shape, hm.dtype)],
        scratch_shapes=_ffn_scratch() + [
            pltpu.VMEM((POOL_PAD + POOL_HALO + tm, D_MODEL), jnp.float32),
            pltpu.VMEM((2, POOL_PAD + POOL_HALO + tm, POOL_GROUP_DIM), jnp.float32)],
        compiler_params=_params(1),
        name="pool_ffn",
    )(h, h, hm, norm_g, pool_w, pool_scale, ffn_w1, ffn_w2)


def kernel(x, meta_tokens, rel_bias, norm_g, ffn_w1, ffn_w2, mix_w_in, mix_w_out, attn_sinks,
           conv_w, pool_w, pool_scale):
    batch, seq, _ = x.shape
    depth = norm_g.shape[0]
    h = x.reshape(batch * seq, D_MODEL)
    hm = meta_tokens.astype(x.dtype)
    bkt_comb, bkt_meta, bkt_self = _bucket_tables(seq)
    for layer in range(depth):
        last = layer == depth - 1
        h, hm = _ffn(h, hm, norm_g, ffn_w1, ffn_w2, layer, 0, True)
        if layer % 2 == 0:
            e = layer // 2
            no_hist = jnp.zeros((CONV_HALO, B_WIDTH), jnp.float32)
            qm, km, vm, ycm, tail_m = _inproj(hm, norm_g, mix_w_in, conv_w, no_hist, layer, e, N_META, False)
            q, k, vt, yc, _ = _inproj(h, norm_g, mix_w_in, conv_w, tail_m, layer, e, seq, True)
            ya = _attention(q, k, vt, km, vm.T, bkt_comb, bkt_meta, rel_bias, attn_sinks, e, batch, seq)
            yam = _attention_meta(qm, km, vm, bkt_self, rel_bias, attn_sinks, e) if not last else ycm
            h, hm = _outproj_ffn(h, ya, yc, hm, yam, ycm, mix_w_out, norm_g, ffn_w1, ffn_w2, layer, e,
                                 not last)
        else:
            h, hm = _pool_ffn(h, hm, norm_g, pool_w, pool_scale, ffn_w1, ffn_w2, layer, layer // 2, seq,
                              not last)
    return h.reshape(batch, seq, D_MODEL)
```

```python
import functools
import math

import jax
import jax.numpy as jnp
from jax.experimental import pallas as pl
from jax.experimental.pallas import tpu as pltpu

D_MODEL = 1024
N_META = 16
A_HEADS = 8
A_KV_HEADS = 2
A_GROUP = A_HEADS // A_KV_HEADS
HEAD_DIM = 64
WINDOW = 128
BLOCK = 128
A_WIDTH = A_HEADS * HEAD_DIM
KV_WIDTH = A_KV_HEADS * HEAD_DIM
B_WIDTH = D_MODEL // 2
CONV_WIDTH = 3
IN_WIDTH = A_WIDTH + 2 * KV_WIDTH + 3 * B_WIDTH
POOL_SIZES = (2, 4, 8, 16)
N_POOL_GROUPS = 4
POOL_GROUP_DIM = D_MODEL // N_POOL_GROUPS
REL_BUCKETS = 32
REL_MAX_DIST = 128
D_FF = 2816
HALF_STEP = 0.5
RMS_EPS = 1e-6

VMEM_LIMIT_BYTES_V7X = 56 * 1024 * 1024
ROW_TILE = 512
FFN_ROW_TILE = 1024
OUTPROJ_FFN_ROW_TILE = 512
FFN_SUB_ROWS = 512
FF_CHUNK = 256
N_FF_CHUNKS = D_FF // FF_CHUNK
N_WSLOTS = 2
POOL_HALO = 16
POOL_PAD = 8
CONV_HALO = 8
ATTN_QBLOCKS = 4
GROUP_LANES = A_GROUP * BLOCK

NEG_INF = float("-inf")
_NT = (((1,), (1,)), ((), ()))

assert WINDOW == BLOCK and N_META == POOL_HALO


def _rms(x, g):
    return x * jax.lax.rsqrt(jnp.mean(x * x, axis=-1, keepdims=True) + RMS_EPS) * g


def _const_spec(shape):
    zeros = (0,) * len(shape)
    return pl.BlockSpec(shape, lambda *_: zeros, pipeline_mode=pl.Buffered(1))


def _slab_spec(arr, *lead):
    rest = arr.shape[len(lead):]
    idx = tuple(lead) + (0,) * len(rest)
    return pl.BlockSpec((None,) * len(lead) + rest, lambda *_: idx, pipeline_mode=pl.Buffered(1))


def _mm(a, w):
    return jax.lax.dot_general(a, w, (((1,), (0,)), ((), ())), preferred_element_type=jnp.float32)


def _params(n_axes):
    return pltpu.CompilerParams(dimension_semantics=("arbitrary",) * n_axes,
                                vmem_limit_bytes=VMEM_LIMIT_BYTES_V7X)


def _chunk_loader(w1_src, w2_src, w1, w2, stage1, stage2, sem):
    def copies(c):
        slot = c % N_WSLOTS
        rows = pl.ds(c * FF_CHUNK, FF_CHUNK)
        return [pltpu.make_async_copy(w1_src.at[:, pl.ds(c * FF_CHUNK, FF_CHUNK)], stage1.at[slot, 0],
                                      sem.at[slot, 0]),
                pltpu.make_async_copy(w1_src.at[:, pl.ds(D_FF + c * FF_CHUNK, FF_CHUNK)], stage1.at[slot, 1],
                                      sem.at[slot, 1]),
                pltpu.make_async_copy(w2_src.at[rows, :], stage2.at[slot], sem.at[slot, 2])]

    def start(c):
        for copy in copies(c):
            copy.start()

    def before_chunk(c):
        if c == 0:
            for k in range(min(N_WSLOTS, N_FF_CHUNKS)):
                start(k)
        for copy in copies(c):
            copy.wait()
        slot = c % N_WSLOTS
        cols = slice(c * FF_CHUNK, (c + 1) * FF_CHUNK)
        w1[:, cols] = stage1[slot, 0].astype(w1.dtype)
        w1[:, D_FF + c * FF_CHUNK:D_FF + (c + 1) * FF_CHUNK] = stage1[slot, 1].astype(w1.dtype)
        w2[cols, :] = stage2[slot].astype(w2.dtype)
        if c + N_WSLOTS < N_FF_CHUNKS:
            start(c + N_WSLOTS)

    return before_chunk


def _swiglu(xn, w1, w2, side=(), before_chunk=None):
    side = list(side)
    acts = []
    for c in range(N_FF_CHUNKS):
        if before_chunk is not None:
            before_chunk(c)
        cols = slice(c * FF_CHUNK, (c + 1) * FF_CHUNK)
        gate = _mm(xn, w1[:, cols])
        up = _mm(xn, w1[:, D_FF + c * FF_CHUNK:D_FF + (c + 1) * FF_CHUNK])
        acts.append((gate * (1.0 / (1.0 + jnp.exp(-gate))) * up).astype(jnp.bfloat16))
        if side:
            side.pop(0)()
    for thunk in side:
        thunk()
    return _mm(jnp.concatenate(acts, axis=1), w2[...])


def _ffn_rows(x, g_ref, pre, post, w1, w2):
    xn = _rms(x, g_ref[pre:pre + 1, :]).astype(jnp.bfloat16)
    return x + HALF_STEP * _rms(_swiglu(xn, w1, w2), g_ref[post:post + 1, :])


def _ffn_scratch():
    return [pltpu.VMEM((D_MODEL, 2 * D_FF), jnp.bfloat16),
            pltpu.VMEM((D_FF, D_MODEL), jnp.bfloat16),
            pltpu.VMEM((N_WSLOTS, 2, D_MODEL, FF_CHUNK), jnp.float32),
            pltpu.VMEM((N_WSLOTS, FF_CHUNK, D_MODEL), jnp.float32),
            pltpu.SemaphoreType.DMA((N_WSLOTS, 3))]


def _sub_blocks(n_rows):
    return [slice(r, r + FFN_SUB_ROWS) for r in range(0, n_rows, FFN_SUB_ROWS)]


def _ffn_tile(read_x, o_ref, g_ref, pre, post, w1, w2, fronts=None, before_chunk=None):
    subs = _sub_blocks(o_ref.shape[0])
    fronts = fronts or [[] for _ in subs]
    prenorm = lambda k: _rms(read_x(subs[k]), g_ref[pre:pre + 1, :]).astype(jnp.bfloat16)
    xns, accs = {}, {}

    def finish(k):
        o_ref[subs[k], :] = read_x(subs[k]) + HALF_STEP * _rms(accs[k], g_ref[post:post + 1, :])

    for thunk in fronts[0]:
        thunk()
    xns[0] = prenorm(0)
    for k in range(len(subs)):
        side = []
        if k > 0:
            side.append(functools.partial(finish, k - 1))
        if k + 1 < len(subs):
            side += fronts[k + 1]
            side.append(lambda k=k: xns.__setitem__(k + 1, prenorm(k + 1)))
        accs[k] = _swiglu(xns[k], w1, w2, side, before_chunk if k == 0 else None)
    finish(len(subs) - 1)


def _ffn_steps(read_x, o_ref, g_ref, pre, post, w1_src, w2_src, scr, make_fronts, first_step_tail):
    w1, w2 = scr[:2]

    @pl.when(pl.program_id(0) == 0)
    def _():
        _ffn_tile(read_x, o_ref, g_ref, pre, post, w1, w2, make_fronts(),
                  _chunk_loader(w1_src, w2_src, *scr[:5]))
        first_step_tail()

    @pl.when(pl.program_id(0) > 0)
    def _():
        _ffn_tile(read_x, o_ref, g_ref, pre, post, w1, w2, make_fronts())


def _ffn_kernel(layer, half, with_meta, h_ref, hm_ref, g_ref, w1_hbm, w2_hbm, o_ref, om_ref, *scr):
    pre, post = 4 * half, 4 * half + 1
    w1, w2 = scr[:2]

    def meta_rows():
        if with_meta:
            om_ref[...] = _ffn_rows(hm_ref[...], g_ref, pre, post, w1, w2)
        else:
            om_ref[...] = hm_ref[...]

    _ffn_steps(lambda rows: h_ref[rows, :], o_ref, g_ref, pre, post, w1_hbm.at[layer, half],
               w2_hbm.at[layer, half], scr, lambda: None, meta_rows)


def _ffn(h, hm, norm_g, ffn_w1, ffn_w2, layer, half, with_meta):
    n = h.shape[0]
    tm = FFN_ROW_TILE
    row = pl.BlockSpec((tm, D_MODEL), lambda i: (i, 0))
    any_spec = pl.BlockSpec(memory_space=pl.ANY)
    return pl.pallas_call(
        functools.partial(_ffn_kernel, layer, half, with_meta),
        grid=(n // tm,),
        in_specs=[row, _const_spec(hm.shape), _slab_spec(norm_g, layer), any_spec, any_spec],
        out_specs=[row, pl.BlockSpec(hm.shape, lambda i: (0, 0))],
        out_shape=[jax.ShapeDtypeStruct(h.shape, h.dtype), jax.ShapeDtypeStruct(hm.shape, hm.dtype)],
        scratch_shapes=_ffn_scratch(),
        compiler_params=_params(1),
        name="ffn",
    )(h, hm, norm_g, ffn_w1, ffn_w2)


def _inproj_kernel(tiles_per_seq, transpose_v, h_ref, g_ref, win_ref, convw_ref, hist_ref,
                   q_ref, k_ref, v_ref, yc_ref, tail_ref, uc_scr):
    i = pl.program_id(0)
    tm = h_ref.shape[0]
    u = _rms(h_ref[...], g_ref[2:3, :]).astype(jnp.bfloat16)
    z = _mm(u, win_ref[...])
    q_ref[...] = (z[:, :A_WIDTH] * (HEAD_DIM ** -0.5)).astype(jnp.bfloat16)
    k_ref[...] = z[:, A_WIDTH:A_WIDTH + KV_WIDTH].astype(jnp.bfloat16)
    v = z[:, A_WIDTH + KV_WIDTH:A_WIDTH + 2 * KV_WIDTH]
    v_ref[...] = (v.T if transpose_v else v).astype(jnp.bfloat16)
    o = A_WIDTH + 2 * KV_WIDTH
    b_gate = z[:, o:o + B_WIDTH]
    c_gate = z[:, o + B_WIDTH:o + 2 * B_WIDTH]
    x_in = z[:, o + 2 * B_WIDTH:o + 3 * B_WIDTH]
    uc = c_gate * x_in

    @pl.when(i % tiles_per_seq == 0)
    def _():
        uc_scr[0:CONV_HALO, :] = hist_ref[...]

    uc_scr[CONV_HALO:CONV_HALO + tm, :] = uc
    y = convw_ref[0:1, :] * uc
    for j in range(1, CONV_WIDTH):
        y = y + convw_ref[j:j + 1, :] * uc_scr[CONV_HALO - j:CONV_HALO - j + tm, :]
    yc_ref[...] = (b_gate * y).astype(jnp.bfloat16)
    tail = uc[tm - CONV_HALO:, :]
    tail_ref[...] = tail
    uc_scr[0:CONV_HALO, :] = tail


def _inproj(h, norm_g, mix_w_in, conv_w, hist, layer, e, rows_per_seq, transpose_v):
    n = h.shape[0]
    tm = min(ROW_TILE, n)
    row = lambda w: pl.BlockSpec((tm, w), lambda i: (i, 0))
    bf = lambda *shape: jax.ShapeDtypeStruct(shape, jnp.bfloat16)
    if transpose_v:
        v_spec, v_shape = pl.BlockSpec((KV_WIDTH, tm), lambda i: (0, i)), bf(KV_WIDTH, n)
    else:
        v_spec, v_shape = row(KV_WIDTH), bf(n, KV_WIDTH)
    return pl.pallas_call(
        functools.partial(_inproj_kernel, rows_per_seq // tm, transpose_v),
        grid=(n // tm,),
        in_specs=[row(D_MODEL), _slab_spec(norm_g, layer), _slab_spec(mix_w_in, e),
                  _slab_spec(conv_w, e), _const_spec((CONV_HALO, B_WIDTH))],
        out_specs=[row(A_WIDTH), row(KV_WIDTH), v_spec, row(B_WIDTH),
                   pl.BlockSpec((CONV_HALO, B_WIDTH), lambda i: (0, 0))],
        out_shape=[bf(n, A_WIDTH), bf(n, KV_WIDTH), v_shape, bf(n, B_WIDTH),
                   jax.ShapeDtypeStruct((CONV_HALO, B_WIDTH), jnp.float32)],
        scratch_shapes=[pltpu.VMEM((CONV_HALO + tm, B_WIDTH), jnp.float32)],
        compiler_params=_params(1),
        name="inproj_conv",
    )(h, norm_g, mix_w_in, conv_w, hist)


def _bias_from_buckets(bucket, rel_ref, head):
    body = lambda b, acc: jnp.where(bucket == b, rel_ref[b, head], acc)
    acc = jax.lax.fori_loop(0, REL_BUCKETS, body, jnp.zeros(bucket.shape, jnp.float32))
    return jnp.where(bucket >= 0, acc, NEG_INF)


def _attn_kernel(e, q_ref, kp_ref, kc_ref, vtp_ref, vtc_ref, km_ref, vtm_ref, bktc_ref, bktm_ref,
                 rel_ref, sink_ref, o_ref, bias_comb_scr, bias_meta_scr, sink_scr):
    j = pl.program_id(1)
    key_idx = jax.lax.broadcasted_iota(jnp.int32, (BLOCK, GROUP_LANES), 0)
    qry_idx = jax.lax.broadcasted_iota(jnp.int32, (BLOCK, GROUP_LANES), 1) % BLOCK
    from_prev = key_idx > qry_idx

    @pl.when(jnp.logical_and(pl.program_id(0) == 0, j == 0))
    def _():
        bktc = bktc_ref[...]
        bktm = bktm_ref[...]
        for h in range(A_HEADS):
            g, i = divmod(h, A_GROUP)
            lanes = slice(i * BLOCK, (i + 1) * BLOCK)
            bias = _bias_from_buckets(bktc, rel_ref, h)
            bias_comb_scr[0, g, :, lanes] = bias
            bias_comb_scr[1, g, :, lanes] = jnp.where(from_prev[:, :BLOCK], NEG_INF, bias)
            bias_meta_scr[:, h * BLOCK:(h + 1) * BLOCK] = _bias_from_buckets(bktm, rel_ref, h)
            sink_scr[g, :, lanes] = jnp.full((8, BLOCK), sink_ref[e, h], jnp.float32)

    kcat = jnp.concatenate([kp_ref[...], kc_ref[...]], axis=0)
    vtcat = jnp.concatenate([vtp_ref[...], vtc_ref[...]], axis=1)
    km = km_ref[...]
    vtm = vtm_ref[...]
    for qb in range(ATTN_QBLOCKS):
        q = q_ref[qb * BLOCK:(qb + 1) * BLOCK, :]
        variant = jnp.where(j == 0, 1, 0) if qb == 0 else 0
        meta_row = pl.multiple_of((j * ATTN_QBLOCKS + qb) * N_META, N_META)
        outs = []
        for g in range(A_KV_HEADS):
            dims = slice(g * HEAD_DIM, (g + 1) * HEAD_DIM)
            q_stack = jnp.concatenate(
                [q[:, (g * A_GROUP + i) * HEAD_DIM:(g * A_GROUP + i + 1) * HEAD_DIM]
                 for i in range(A_GROUP)], axis=0)
            k_all = jnp.concatenate([kcat[qb * BLOCK:(qb + 2) * BLOCK, dims], km[:, dims]], axis=0)
            st = jax.lax.dot_general(k_all, q_stack, _NT, preferred_element_type=jnp.float32)
            comb = jnp.where(from_prev, st[0:BLOCK], st[BLOCK:2 * BLOCK]) + bias_comb_scr[variant, g]
            meta = st[2 * BLOCK:] + bias_meta_scr[pl.ds(meta_row, N_META),
                                                  g * GROUP_LANES:(g + 1) * GROUP_LANES]
            sink = sink_scr[g, 0:1, :]
            m = jnp.maximum(jnp.maximum(jnp.max(comb, axis=0, keepdims=True),
                                        jnp.max(meta, axis=0, keepdims=True)), sink)
            pc = jnp.exp(comb - m)
            pm = jnp.exp(meta - m)
            den = (jnp.sum(pc, axis=0, keepdims=True) + jnp.sum(pm, axis=0, keepdims=True)
                   + jnp.exp(sink - m))
            p_band = jnp.concatenate([jnp.where(from_prev, pc, 0.0), jnp.where(from_prev, 0.0, pc)],
                                     axis=0).astype(jnp.bfloat16)
            ot = jnp.dot(vtcat[dims, qb * BLOCK:(qb + 2) * BLOCK], p_band,
                         preferred_element_type=jnp.float32)
            ot = ot + jnp.dot(vtm[dims, :], pm.astype(jnp.bfloat16), preferred_element_type=jnp.float32)
            ot = ot * (1.0 / den)
            outs += [ot[:, i * BLOCK:(i + 1) * BLOCK] for i in range(A_GROUP)]
        yt = jnp.concatenate(outs, axis=0)
        o_ref[qb * BLOCK:(qb + 1) * BLOCK, :] = yt.T.astype(o_ref.dtype)


def _attention(q, k, vt, k_meta, vt_meta, bkt_comb, bkt_meta, rel_bias, sinks, e, batch, seq):
    nb = seq // BLOCK
    qb = ATTN_QBLOCKS
    tiles = nb // qb
    cur_rows = lambda w: pl.BlockSpec((qb * BLOCK, w), lambda b, j: (b * tiles + j, 0))
    prev_blk = lambda b, j: b * nb + jnp.maximum(j * qb - 1, 0)
    smem = pl.BlockSpec(memory_space=pltpu.SMEM)
    return pl.pallas_call(
        functools.partial(_attn_kernel, e),
        grid=(batch, tiles),
        in_specs=[cur_rows(A_WIDTH),
                  pl.BlockSpec((BLOCK, KV_WIDTH), lambda b, j: (prev_blk(b, j), 0)),
                  cur_rows(KV_WIDTH),
                  pl.BlockSpec((KV_WIDTH, BLOCK), lambda b, j: (0, prev_blk(b, j))),
                  pl.BlockSpec((KV_WIDTH, qb * BLOCK), lambda b, j: (0, b * tiles + j)),
                  _const_spec(k_meta.shape), _const_spec(vt_meta.shape),
                  _const_spec(bkt_comb.shape), _const_spec(bkt_meta.shape), smem, smem],
        out_specs=cur_rows(A_WIDTH),
        out_shape=jax.ShapeDtypeStruct(q.shape, jnp.bfloat16),
        scratch_shapes=[pltpu.VMEM((2, A_KV_HEADS, BLOCK, GROUP_LANES), jnp.float32),
                        pltpu.VMEM((nb * N_META, A_HEADS * BLOCK), jnp.float32),
                        pltpu.VMEM((A_KV_HEADS, 8, GROUP_LANES), jnp.float32)],
        compiler_params=_params(2),
        name="swa_attention",
    )(q, k, k, vt, vt, k_meta, vt_meta, bkt_comb, bkt_meta, rel_bias, sinks)


def _attn_meta_kernel(e, q_ref, k_ref, v_ref, bkt_ref, rel_ref, sink_ref, o_ref):
    q = q_ref[...]
    k = k_ref[...]
    v = v_ref[...]
    bkt = bkt_ref[...]
    outs = []
    for h in range(A_HEADS):
        dims = slice((h // A_GROUP) * HEAD_DIM, (h // A_GROUP + 1) * HEAD_DIM)
        s = jax.lax.dot_general(q[:, h * HEAD_DIM:(h + 1) * HEAD_DIM], k[:, dims], _NT,
                                preferred_element_type=jnp.float32)
        s = s + _bias_from_buckets(bkt, rel_ref, h)
        sink = sink_ref[e, h]
        m = jnp.maximum(jnp.max(s, axis=-1, keepdims=True), sink)
        p = jnp.exp(s - m)
        den = jnp.sum(p, axis=-1, keepdims=True) + jnp.exp(sink - m)
        o = jnp.dot(p.astype(jnp.bfloat16), v[:, dims], preferred_element_type=jnp.float32)
        outs.append(o * (1.0 / den))
    o_ref[...] = jnp.concatenate(outs, axis=-1).astype(o_ref.dtype)


def _attention_meta(q, k, v, bkt_self, rel_bias, sinks, e):
    vmem = pl.BlockSpec(memory_space=pltpu.VMEM)
    smem = pl.BlockSpec(memory_space=pltpu.SMEM)
    return pl.pallas_call(
        functools.partial(_attn_meta_kernel, e),
        in_specs=[vmem, vmem, vmem, vmem, smem, smem],
        out_specs=vmem,
        out_shape=jax.ShapeDtypeStruct(q.shape, jnp.bfloat16),
        name="meta_attention",
    )(q, k, v, bkt_self, rel_bias, sinks)


def _t5_bucket(dist):
    n = jnp.maximum(dist, 0)
    max_exact = REL_BUCKETS // 2
    nf = jnp.maximum(n, 1).astype(jnp.float32)
    large = max_exact + (jnp.log(nf / max_exact) / math.log(REL_MAX_DIST / max_exact)
                         * (REL_BUCKETS - max_exact)).astype(jnp.int32)
    large = jnp.minimum(large, REL_BUCKETS - 1)
    return jnp.where(n < max_exact, n, large)


def _bucket_tables(seq):
    bucket = lambda dist: jnp.where(dist >= 0, _t5_bucket(dist), -1).astype(jnp.int32)
    idx = jnp.arange(BLOCK)
    comb = bucket((idx[None, :] - idx[:, None]) % BLOCK)
    m = jnp.arange(N_META)
    pos = N_META + jnp.arange(seq).reshape(seq // BLOCK, 1, BLOCK)
    meta = bucket(pos - m[None, :, None]).reshape(seq // BLOCK * N_META, BLOCK)
    self_ = bucket(m[:, None] - m[None, :])
    return comb, meta, self_


def _outproj_rows(x, ya, yc, wo_ref, g_ref):
    mix = _mm(ya, wo_ref[0:A_WIDTH, :]) + _mm(yc, wo_ref[A_WIDTH:, :])
    return x + _rms(mix, g_ref[3:4, :])


def _outproj_ffn_kernel(layer, with_meta, h_ref, ya_ref, yc_ref, hm_ref, yam_ref, ycm_ref, g_ref, wo_ref,
                        w1_hbm, w2_hbm, o_ref, om_ref, *scr):
    w1, w2 = scr[:2]

    def meta_rows():
        if with_meta:
            xm = _outproj_rows(hm_ref[...], yam_ref[...], ycm_ref[...], wo_ref, g_ref)
            om_ref[...] = _ffn_rows(xm, g_ref, 4, 5, w1, w2)
        else:
            om_ref[...] = hm_ref[...]

    def front(rows):
        o_ref[rows, :] = _outproj_rows(h_ref[rows, :], ya_ref[rows, :], yc_ref[rows, :], wo_ref, g_ref)

    make_fronts = lambda: [[functools.partial(front, rows)] for rows in _sub_blocks(o_ref.shape[0])]
    _ffn_steps(lambda rows: o_ref[rows, :], o_ref, g_ref, 4, 5, w1_hbm.at[layer, 1], w2_hbm.at[layer, 1],
               scr, make_fronts, meta_rows)


def _outproj_ffn(h, ya, yc, hm, yam, ycm, mix_w_out, norm_g, ffn_w1, ffn_w2, layer, e, with_meta):
    n = h.shape[0]
    tm = OUTPROJ_FFN_ROW_TILE
    row = lambda w: pl.BlockSpec((tm, w), lambda i: (i, 0))
    any_spec = pl.BlockSpec(memory_space=pl.ANY)
    return pl.pallas_call(
        functools.partial(_outproj_ffn_kernel, layer, with_meta),
        grid=(n // tm,),
        in_specs=[row(D_MODEL), row(A_WIDTH), row(B_WIDTH), _const_spec(hm.shape), _const_spec(yam.shape),
                  _const_spec(ycm.shape), _slab_spec(norm_g, layer), _slab_spec(mix_w_out, e),
                  any_spec, any_spec],
        out_specs=[row(D_MODEL), pl.BlockSpec(hm.shape, lambda i: (0, 0))],
        out_shape=[jax.ShapeDtypeStruct(h.shape, h.dtype), jax.ShapeDtypeStruct(hm.shape, hm.dtype)],
        scratch_shapes=_ffn_scratch(),
        compiler_params=_params(1),
        name="outproj_ffn",
    )(h, ya, yc, hm, yam, ycm, norm_g, mix_w_out, ffn_w1, ffn_w2)


def _pool_fronts(x_ref, hist, t0, g_ref, pw_ref, ps_ref, u_scr, lvl_scr, out_ref):
    tm = x_ref.shape[0]
    gpre = g_ref[2:3, :]
    top = POOL_PAD + POOL_HALO
    sub = min(tm, FFN_SUB_ROWS)

    def prenorm(r):
        if r == 0:
            zero_pad = jnp.zeros((POOL_PAD, D_MODEL), jnp.float32)
            u_scr[0:POOL_PAD, :] = zero_pad
            u_scr[POOL_PAD:top, :] = _rms(hist, gpre)
            for slot in range(2):
                lvl_scr[slot, 0:POOL_PAD, :] = zero_pad[:, :POOL_GROUP_DIM]
        u_scr[top + r:top + r + sub, :] = _rms(x_ref[r:r + sub, :], gpre)

    def group(r, gi):
        w = POOL_SIZES[gi]
        lo, hi = POOL_PAD + r, top + r + sub
        cols = slice(gi * POOL_GROUP_DIM, (gi + 1) * POOL_GROUP_DIM)
        src, src_cols, d = u_scr, cols, 1
        while d < w:
            s = src[lo:hi, src_cols] + src[lo - d:hi - d, src_cols]
            d *= 2
            if d < w:
                dst = lvl_scr.at[(d.bit_length() - 1) % 2]
                dst[lo:hi, :] = s
                src, src_cols = dst, slice(None)
        ug = u_scr[top + r:top + r + sub, cols]
        t = t0 + r + jax.lax.broadcasted_iota(jnp.int32, (sub, 1), 0)
        cnt = jnp.minimum(t + 1, w).astype(jnp.float32)
        diff = (s[POOL_HALO:, :] / cnt - ug).astype(jnp.bfloat16)
        out_ref[r:r + sub, cols] = _mm(diff, pw_ref[gi]) * ps_ref[:, cols]

    def finish(r):
        rows = slice(r, r + sub)
        out_ref[rows, :] = x_ref[rows, :] + _rms(out_ref[rows, :], g_ref[3:4, :])

    return [[functools.partial(prenorm, r)]
            + [functools.partial(group, r, gi) for gi in range(N_POOL_GROUPS)]
            + [functools.partial(finish, r)] for r in range(0, tm, sub)]


def _pool_ffn_kernel(layer, tiles_per_seq, with_meta, h_ref, prev_ref, hm_ref, g_ref, pw_ref, ps_ref,
                     w1_hbm, w2_hbm, o_ref, om_ref, *scr):
    w1, w2 = scr[:2]
    u_scr, lvl_scr = scr[-2:]

    def meta_rows():
        if with_meta:
            no_hist = jnp.zeros((POOL_HALO, D_MODEL), jnp.float32)
            for thunk in _pool_fronts(hm_ref, no_hist, 0, g_ref, pw_ref, ps_ref, u_scr, lvl_scr, om_ref)[0]:
                thunk()
            om_ref[...] = _ffn_rows(om_ref[...], g_ref, 4, 5, w1, w2)
        else:
            om_ref[...] = hm_ref[...]

    def make_fronts():
        j = pl.program_id(0) % tiles_per_seq
        hist = jnp.where(j == 0, hm_ref[...], prev_ref[...])
        return _pool_fronts(h_ref, hist, N_META + j * h_ref.shape[0], g_ref, pw_ref, ps_ref, u_scr, lvl_scr,
                            o_ref)

    _ffn_steps(lambda rows: o_ref[rows, :], o_ref, g_ref, 4, 5, w1_hbm.at[layer, 1], w2_hbm.at[layer, 1],
               scr, make_fronts, meta_rows)


def _pool_ffn(h, hm, norm_g, pool_w, pool_scale, ffn_w1, ffn_w2, layer, o, seq, with_meta):
    n = h.shape[0]
    tm = FFN_ROW_TILE
    halo_blocks = tm // POOL_HALO
    row = pl.BlockSpec((tm, D_MODEL), lambda i: (i, 0))
    prev = pl.BlockSpec((POOL_HALO, D_MODEL), lambda i: (jnp.maximum(i * halo_blocks - 1, 0), 0))
    any_spec = pl.BlockSpec(memory_space=pl.ANY)
    return pl.pallas_call(
        functools.partial(_pool_ffn_kernel, layer, seq // tm, with_meta),
        grid=(n // tm,),
        in_specs=[row, prev, _const_spec(hm.shape), _slab_spec(norm_g, layer), _slab_spec(pool_w, o),
                  pl.BlockSpec((1, D_MODEL), lambda i: (o, 0), pipeline_mode=pl.Buffered(1)),
                  any_spec, any_spec],
        out_specs=[row, pl.BlockSpec(hm.shape, lambda i: (0, 0))],
        out_shape=[jax.ShapeDtypeStruct(h.shape, h.dtype), jax.ShapeDtypeStruct(hm.shape, hm.dtype)],
        scratch_shapes=_ffn_scratch() + [
            pltpu.VMEM((POOL_PAD + POOL_HALO + tm, D_MODEL), jnp.float32),
            pltpu.VMEM((2, POOL_PAD + POOL_HALO + tm, POOL_GROUP_DIM), jnp.float32)],
        compiler_params=_params(1),
        name="pool_ffn",
    )(h, h, hm, norm_g, pool_w, pool_scale, ffn_w1, ffn_w2)


def kernel(x, meta_tokens, rel_bias, norm_g, ffn_w1, ffn_w2, mix_w_in, mix_w_out, attn_sinks,
           conv_w, pool_w, pool_scale):
    batch, seq, _ = x.shape
    depth = norm_g.shape[0]
    h = x.reshape(batch * seq, D_MODEL)
    hm = meta_tokens.astype(x.dtype)
    bkt_comb, bkt_meta, bkt_self = _bucket_tables(seq)
    for layer in range(depth):
        last = layer == depth - 1
        h, hm = _ffn(h, hm, norm_g, ffn_w1, ffn_w2, layer, 0, True)
        if layer % 2 == 0:
            e = layer // 2
            no_hist = jnp.zeros((CONV_HALO, B_WIDTH), jnp.float32)
            qm, km, vm, ycm, tail_m = _inproj(hm, norm_g, mix_w_in, conv_w, no_hist, layer, e, N_META, False)
            q, k, vt, yc, _ = _inproj(h, norm_g, mix_w_in, conv_w, tail_m, layer, e, seq, True)
            ya = _attention(q, k, vt, km, vm.T, bkt_comb, bkt_meta, rel_bias, attn_sinks, e, batch, seq)
            yam = _attention_meta(qm, km, vm, bkt_self, rel_bias, attn_sinks, e) if not last else ycm
            h, hm = _outproj_ffn(h, ya, yc, hm, yam, ycm, mix_w_out, norm_g, ffn_w1, ffn_w2, layer, e,
                                 not last)
        else:
            h, hm = _pool_ffn(h, hm, norm_g, pool_w, pool_scale, ffn_w1, ffn_w2, layer, layer // 2, seq,
                              not last)
    return h.reshape(batch, seq, D_MODEL)
```

```python
import functools
import math

import jax
import jax.numpy as jnp
from jax.experimental import pallas as pl
from jax.experimental.pallas import tpu as pltpu

D_MODEL = 1024
N_META = 16
A_HEADS = 8
A_KV_HEADS = 2
A_GROUP = A_HEADS // A_KV_HEADS
HEAD_DIM = 64
WINDOW = 128
BLOCK = 128
A_WIDTH = A_HEADS * HEAD_DIM
KV_WIDTH = A_KV_HEADS * HEAD_DIM
B_WIDTH = D_MODEL // 2
CONV_WIDTH = 3
IN_WIDTH = A_WIDTH + 2 * KV_WIDTH + 3 * B_WIDTH
POOL_SIZES = (2, 4, 8, 16)
N_POOL_GROUPS = 4
POOL_GROUP_DIM = D_MODEL // N_POOL_GROUPS
REL_BUCKETS = 32
REL_MAX_DIST = 128
D_FF = 2816
HALF_STEP = 0.5
RMS_EPS = 1e-6

VMEM_LIMIT_BYTES_V7X = 56 * 1024 * 1024
ROW_TILE = 1024
ROW_SUB = 512
FFN_ROW_TILE = 1024
OUTPROJ_FFN_ROW_TILE = 512
FFN_SUB_ROWS = 512
FF_CHUNK = 256
N_FF_CHUNKS = D_FF // FF_CHUNK
N_WSLOTS = 2
POOL_HALO = 16
POOL_PAD = 8
CONV_HALO = 8
ATTN_QBLOCKS = 4
GROUP_LANES = A_GROUP * BLOCK

NEG_INF = float("-inf")
_NT = (((1,), (1,)), ((), ()))

assert WINDOW == BLOCK and N_META == POOL_HALO


def _rms(x, g):
    return x * jax.lax.rsqrt(jnp.mean(x * x, axis=-1, keepdims=True) + RMS_EPS) * g


def _const_spec(shape):
    zeros = (0,) * len(shape)
    return pl.BlockSpec(shape, lambda *_: zeros, pipeline_mode=pl.Buffered(1))


def _slab_spec(arr, *lead):
    rest = arr.shape[len(lead):]
    idx = tuple(lead) + (0,) * len(rest)
    return pl.BlockSpec((None,) * len(lead) + rest, lambda *_: idx, pipeline_mode=pl.Buffered(1))


def _mm(a, w):
    return jax.lax.dot_general(a, w, (((1,), (0,)), ((), ())), preferred_element_type=jnp.float32)


def _params(n_axes):
    return pltpu.CompilerParams(dimension_semantics=("arbitrary",) * n_axes,
                                vmem_limit_bytes=VMEM_LIMIT_BYTES_V7X)


def _chunk_loader(w1_src, w2_src, w1, w2, stage1, stage2, sem):
    def copies(c):
        slot = c % N_WSLOTS
        rows = pl.ds(c * FF_CHUNK, FF_CHUNK)
        return [pltpu.make_async_copy(w1_src.at[:, pl.ds(c * FF_CHUNK, FF_CHUNK)], stage1.at[slot, 0],
                                      sem.at[slot, 0]),
                pltpu.make_async_copy(w1_src.at[:, pl.ds(D_FF + c * FF_CHUNK, FF_CHUNK)], stage1.at[slot, 1],
                                      sem.at[slot, 1]),
                pltpu.make_async_copy(w2_src.at[rows, :], stage2.at[slot], sem.at[slot, 2])]

    def start(c):
        for copy in copies(c):
            copy.start()

    def before_chunk(c):
        if c == 0:
            for k in range(min(N_WSLOTS, N_FF_CHUNKS)):
                start(k)
        for copy in copies(c):
            copy.wait()
        slot = c % N_WSLOTS
        cols = slice(c * FF_CHUNK, (c + 1) * FF_CHUNK)
        w1[:, cols] = stage1[slot, 0].astype(w1.dtype)
        w1[:, D_FF + c * FF_CHUNK:D_FF + (c + 1) * FF_CHUNK] = stage1[slot, 1].astype(w1.dtype)
        w2[cols, :] = stage2[slot].astype(w2.dtype)
        if c + N_WSLOTS < N_FF_CHUNKS:
            start(c + N_WSLOTS)

    return before_chunk


def _swiglu(xn, w1, w2, side=(), before_chunk=None):
    side = list(side)
    acts = []
    for c in range(N_FF_CHUNKS):
        if before_chunk is not None:
            before_chunk(c)
        cols = slice(c * FF_CHUNK, (c + 1) * FF_CHUNK)
        gate = _mm(xn, w1[:, cols])
        up = _mm(xn, w1[:, D_FF + c * FF_CHUNK:D_FF + (c + 1) * FF_CHUNK])
        acts.append((gate * (1.0 / (1.0 + jnp.exp(-gate))) * up).astype(jnp.bfloat16))
        if side:
            side.pop(0)()
    for thunk in side:
        thunk()
    return _mm(jnp.concatenate(acts, axis=1), w2[...])


def _ffn_rows(x, g_ref, pre, post, w1, w2):
    xn = _rms(x, g_ref[pre:pre + 1, :]).astype(jnp.bfloat16)
    return x + HALF_STEP * _rms(_swiglu(xn, w1, w2), g_ref[post:post + 1, :])


def _ffn_scratch():
    return [pltpu.VMEM((D_MODEL, 2 * D_FF), jnp.bfloat16),
            pltpu.VMEM((D_FF, D_MODEL), jnp.bfloat16),
            pltpu.VMEM((N_WSLOTS, 2, D_MODEL, FF_CHUNK), jnp.float32),
            pltpu.VMEM((N_WSLOTS, FF_CHUNK, D_MODEL), jnp.float32),
            pltpu.SemaphoreType.DMA((N_WSLOTS, 3))]


def _sub_blocks(n_rows):
    return [slice(r, r + FFN_SUB_ROWS) for r in range(0, n_rows, FFN_SUB_ROWS)]


def _ffn_tile(read_x, o_ref, g_ref, pre, post, w1, w2, fronts=None, before_chunk=None):
    subs = _sub_blocks(o_ref.shape[0])
    fronts = fronts or [[] for _ in subs]
    prenorm = lambda k: _rms(read_x(subs[k]), g_ref[pre:pre + 1, :]).astype(jnp.bfloat16)
    xns, accs = {}, {}

    def finish(k):
        o_ref[subs[k], :] = read_x(subs[k]) + HALF_STEP * _rms(accs[k], g_ref[post:post + 1, :])

    for thunk in fronts[0]:
        thunk()
    xns[0] = prenorm(0)
    for k in range(len(subs)):
        side = []
        if k > 0:
            side.append(functools.partial(finish, k - 1))
        if k + 1 < len(subs):
            side += fronts[k + 1]
            side.append(lambda k=k: xns.__setitem__(k + 1, prenorm(k + 1)))
        accs[k] = _swiglu(xns[k], w1, w2, side, before_chunk if k == 0 else None)
    finish(len(subs) - 1)


def _ffn_steps(read_x, o_ref, g_ref, pre, post, w1_src, w2_src, scr, make_fronts, first_step_tail):
    w1, w2 = scr[:2]

    @pl.when(pl.program_id(0) == 0)
    def _():
        _ffn_tile(read_x, o_ref, g_ref, pre, post, w1, w2, make_fronts(),
                  _chunk_loader(w1_src, w2_src, *scr[:5]))
        first_step_tail()

    @pl.when(pl.program_id(0) > 0)
    def _():
        _ffn_tile(read_x, o_ref, g_ref, pre, post, w1, w2, make_fronts())


def _ffn_kernel(layer, half, with_meta, h_ref, hm_ref, g_ref, w1_hbm, w2_hbm, o_ref, om_ref, *scr):
    pre, post = 4 * half, 4 * half + 1
    w1, w2 = scr[:2]

    def meta_rows():
        if with_meta:
            om_ref[...] = _ffn_rows(hm_ref[...], g_ref, pre, post, w1, w2)
        else:
            om_ref[...] = hm_ref[...]

    _ffn_steps(lambda rows: h_ref[rows, :], o_ref, g_ref, pre, post, w1_hbm.at[layer, half],
               w2_hbm.at[layer, half], scr, lambda: None, meta_rows)


def _ffn(h, hm, norm_g, ffn_w1, ffn_w2, layer, half, with_meta):
    n = h.shape[0]
    tm = FFN_ROW_TILE
    row = pl.BlockSpec((tm, D_MODEL), lambda i: (i, 0))
    any_spec = pl.BlockSpec(memory_space=pl.ANY)
    return pl.pallas_call(
        functools.partial(_ffn_kernel, layer, half, with_meta),
        grid=(n // tm,),
        in_specs=[row, _const_spec(hm.shape), _slab_spec(norm_g, layer), any_spec, any_spec],
        out_specs=[row, pl.BlockSpec(hm.shape, lambda i: (0, 0))],
        out_shape=[jax.ShapeDtypeStruct(h.shape, h.dtype), jax.ShapeDtypeStruct(hm.shape, hm.dtype)],
        scratch_shapes=_ffn_scratch(),
        compiler_params=_params(1),
        name="ffn",
    )(h, hm, norm_g, ffn_w1, ffn_w2)


def _inproj_kernel(tiles_per_seq, transpose_v, h_ref, g_ref, win_ref, convw_ref, hist_ref,
                   q_ref, k_ref, v_ref, yc_ref, tail_ref, uc_scr):
    i = pl.program_id(0)
    tm = h_ref.shape[0]

    @pl.when(i % tiles_per_seq == 0)
    def _():
        uc_scr[0:CONV_HALO, :] = hist_ref[...]

    sub = min(tm, ROW_SUB)
    for r in range(0, tm, sub):
        rows = slice(r, r + sub)
        u = _rms(h_ref[rows, :], g_ref[2:3, :]).astype(jnp.bfloat16)
        z = _mm(u, win_ref[...])
        q_ref[rows, :] = (z[:, :A_WIDTH] * (HEAD_DIM ** -0.5)).astype(jnp.bfloat16)
        k_ref[rows, :] = z[:, A_WIDTH:A_WIDTH + KV_WIDTH].astype(jnp.bfloat16)
        v = z[:, A_WIDTH + KV_WIDTH:A_WIDTH + 2 * KV_WIDTH]
        if transpose_v:
            v_ref[:, rows] = v.T.astype(jnp.bfloat16)
        else:
            v_ref[rows, :] = v.astype(jnp.bfloat16)
        o = A_WIDTH + 2 * KV_WIDTH
        b_gate = z[:, o:o + B_WIDTH]
        c_gate = z[:, o + B_WIDTH:o + 2 * B_WIDTH]
        x_in = z[:, o + 2 * B_WIDTH:o + 3 * B_WIDTH]
        uc = c_gate * x_in
        uc_scr[CONV_HALO + r:CONV_HALO + r + sub, :] = uc
        y = convw_ref[0:1, :] * uc
        for j in range(1, CONV_WIDTH):
            y = y + convw_ref[j:j + 1, :] * uc_scr[CONV_HALO + r - j:CONV_HALO + r - j + sub, :]
        yc_ref[rows, :] = (b_gate * y).astype(jnp.bfloat16)
    tail = uc_scr[tm:tm + CONV_HALO, :]
    tail_ref[...] = tail
    uc_scr[0:CONV_HALO, :] = tail


def _inproj(h, norm_g, mix_w_in, conv_w, hist, layer, e, rows_per_seq, transpose_v):
    n = h.shape[0]
    tm = min(ROW_TILE, n)
    row = lambda w: pl.BlockSpec((tm, w), lambda i: (i, 0))
    bf = lambda *shape: jax.ShapeDtypeStruct(shape, jnp.bfloat16)
    if transpose_v:
        v_spec, v_shape = pl.BlockSpec((KV_WIDTH, tm), lambda i: (0, i)), bf(KV_WIDTH, n)
    else:
        v_spec, v_shape = row(KV_WIDTH), bf(n, KV_WIDTH)
    return pl.pallas_call(
        functools.partial(_inproj_kernel, rows_per_seq // tm, transpose_v),
        grid=(n // tm,),
        in_specs=[row(D_MODEL), _slab_spec(norm_g, layer), _slab_spec(mix_w_in, e),
                  _slab_spec(conv_w, e), _const_spec((CONV_HALO, B_WIDTH))],
        out_specs=[row(A_WIDTH), row(KV_WIDTH), v_spec, row(B_WIDTH),
                   pl.BlockSpec((CONV_HALO, B_WIDTH), lambda i: (0, 0))],
        out_shape=[bf(n, A_WIDTH), bf(n, KV_WIDTH), v_shape, bf(n, B_WIDTH),
                   jax.ShapeDtypeStruct((CONV_HALO, B_WIDTH), jnp.float32)],
        scratch_shapes=[pltpu.VMEM((CONV_HALO + tm, B_WIDTH), jnp.float32)],
        compiler_params=_params(1),
        name="inproj_conv",
    )(h, norm_g, mix_w_in, conv_w, hist)


def _bias_from_buckets(bucket, rel_ref, head):
    body = lambda b, acc: jnp.where(bucket == b, rel_ref[b, head], acc)
    acc = jax.lax.fori_loop(0, REL_BUCKETS, body, jnp.zeros(bucket.shape, jnp.float32))
    return jnp.where(bucket >= 0, acc, NEG_INF)


def _attn_kernel(e, q_ref, kp_ref, kc_ref, vtp_ref, vtc_ref, km_ref, vtm_ref, bktc_ref, bktm_ref,
                 rel_ref, sink_ref, o_ref, bias_comb_scr, bias_meta_scr, sink_scr, st_scr, p_scr, rden_scr,
                 yt_scr):
    j = pl.program_id(1)
    key_idx = jax.lax.broadcasted_iota(jnp.int32, (BLOCK, GROUP_LANES), 0)
    qry_idx = jax.lax.broadcasted_iota(jnp.int32, (BLOCK, GROUP_LANES), 1) % BLOCK
    from_prev = key_idx > qry_idx

    @pl.when(jnp.logical_and(pl.program_id(0) == 0, j == 0))
    def _():
        bktc = bktc_ref[...]
        bktm = bktm_ref[...]
        for h in range(A_HEADS):
            g, i = divmod(h, A_GROUP)
            lanes = slice(i * BLOCK, (i + 1) * BLOCK)
            bias = _bias_from_buckets(bktc, rel_ref, h)
            bias_comb_scr[0, g, :, lanes] = bias
            bias_comb_scr[1, g, :, lanes] = jnp.where(from_prev[:, :BLOCK], NEG_INF, bias)
            bias_meta_scr[:, h * BLOCK:(h + 1) * BLOCK] = _bias_from_buckets(bktm, rel_ref, h)
            sink_scr[g, :, lanes] = jnp.full((8, BLOCK), sink_ref[e, h], jnp.float32)

    kcat = jnp.concatenate([kp_ref[...], kc_ref[...]], axis=0)
    vtcat = jnp.concatenate([vtp_ref[...], vtc_ref[...]], axis=1)
    km = km_ref[...]
    vtm = vtm_ref[...]
    chains = [(qb, g) for qb in range(ATTN_QBLOCKS) for g in range(A_KV_HEADS)]
    for ci, (qb, g) in enumerate(chains):
        dims = slice(g * HEAD_DIM, (g + 1) * HEAD_DIM)
        q = q_ref[qb * BLOCK:(qb + 1) * BLOCK, :]
        q_stack = jnp.concatenate(
            [q[:, (g * A_GROUP + i) * HEAD_DIM:(g * A_GROUP + i + 1) * HEAD_DIM]
             for i in range(A_GROUP)], axis=0)
        k_all = jnp.concatenate([kcat[qb * BLOCK:(qb + 2) * BLOCK, dims], km[:, dims]], axis=0)
        st_scr[ci] = jax.lax.dot_general(k_all, q_stack, _NT, preferred_element_type=jnp.float32)
    for ci, (qb, g) in enumerate(chains):
        variant = jnp.where(j == 0, 1, 0) if qb == 0 else 0
        meta_row = pl.multiple_of((j * ATTN_QBLOCKS + qb) * N_META, N_META)
        comb = (jnp.where(from_prev, st_scr[ci, 0:BLOCK, :], st_scr[ci, BLOCK:2 * BLOCK, :])
                + bias_comb_scr[variant, g])
        meta = st_scr[ci, 2 * BLOCK:, :] + bias_meta_scr[pl.ds(meta_row, N_META),
                                                         g * GROUP_LANES:(g + 1) * GROUP_LANES]
        sink = sink_scr[g, 0:1, :]
        m = jnp.maximum(jnp.maximum(jnp.max(comb, axis=0, keepdims=True),
                                    jnp.max(meta, axis=0, keepdims=True)), sink)
        pc = jnp.exp(comb - m)
        pm = jnp.exp(meta - m)
        den = (jnp.sum(pc, axis=0, keepdims=True) + jnp.sum(pm, axis=0, keepdims=True)
               + jnp.exp(sink - m))
        p_scr[ci, 0:BLOCK, :] = jnp.where(from_prev, pc, 0.0).astype(jnp.bfloat16)
        p_scr[ci, BLOCK:2 * BLOCK, :] = jnp.where(from_prev, 0.0, pc).astype(jnp.bfloat16)
        p_scr[ci, 2 * BLOCK:, :] = pm.astype(jnp.bfloat16)
        rden_scr[ci] = jnp.broadcast_to(1.0 / den, (8, GROUP_LANES))
    for ci, (qb, g) in enumerate(chains):
        dims = slice(g * HEAD_DIM, (g + 1) * HEAD_DIM)
        ot = jnp.dot(vtcat[dims, qb * BLOCK:(qb + 2) * BLOCK], p_scr[ci, 0:2 * BLOCK, :],
                     preferred_element_type=jnp.float32)
        ot = ot + jnp.dot(vtm[dims, :], p_scr[ci, 2 * BLOCK:, :], preferred_element_type=jnp.float32)
        ot = ot * rden_scr[ci, 0:1, :]
        for i in range(A_GROUP):
            h = g * A_GROUP + i
            yt_scr[qb, h * HEAD_DIM:(h + 1) * HEAD_DIM, :] = ot[:, i * BLOCK:(i + 1) * BLOCK]
    for qb in range(ATTN_QBLOCKS):
        o_ref[qb * BLOCK:(qb + 1) * BLOCK, :] = yt_scr[qb].T.astype(o_ref.dtype)


def _attention(q, k, vt, k_meta, vt_meta, bkt_comb, bkt_meta, rel_bias, sinks, e, batch, seq):
    nb = seq // BLOCK
    qb = ATTN_QBLOCKS
    tiles = nb // qb
    cur_rows = lambda w: pl.BlockSpec((qb * BLOCK, w), lambda b, j: (b * tiles + j, 0))
    prev_blk = lambda b, j: b * nb + jnp.maximum(j * qb - 1, 0)
    smem = pl.BlockSpec(memory_space=pltpu.SMEM)
    return pl.pallas_call(
        functools.partial(_attn_kernel, e),
        grid=(batch, tiles),
        in_specs=[cur_rows(A_WIDTH),
                  pl.BlockSpec((BLOCK, KV_WIDTH), lambda b, j: (prev_blk(b, j), 0)),
                  cur_rows(KV_WIDTH),
                  pl.BlockSpec((KV_WIDTH, BLOCK), lambda b, j: (0, prev_blk(b, j))),
                  pl.BlockSpec((KV_WIDTH, qb * BLOCK), lambda b, j: (0, b * tiles + j)),
                  _const_spec(k_meta.shape), _const_spec(vt_meta.shape),
                  _const_spec(bkt_comb.shape), _const_spec(bkt_meta.shape), smem, smem],
        out_specs=cur_rows(A_WIDTH),
        out_shape=jax.ShapeDtypeStruct(q.shape, jnp.bfloat16),
        scratch_shapes=[pltpu.VMEM((2, A_KV_HEADS, BLOCK, GROUP_LANES), jnp.float32),
                        pltpu.VMEM((nb * N_META, A_HEADS * BLOCK), jnp.float32),
                        pltpu.VMEM((A_KV_HEADS, 8, GROUP_LANES), jnp.float32),
                        pltpu.VMEM((qb * A_KV_HEADS, 2 * BLOCK + N_META, GROUP_LANES), jnp.float32),
                        pltpu.VMEM((qb * A_KV_HEADS, 2 * BLOCK + N_META, GROUP_LANES), jnp.bfloat16),
                        pltpu.VMEM((qb * A_KV_HEADS, 8, GROUP_LANES), jnp.float32),
                        pltpu.VMEM((qb, A_WIDTH, BLOCK), jnp.float32)],
        compiler_params=_params(2),
        name="swa_attention",
    )(q, k, k, vt, vt, k_meta, vt_meta, bkt_comb, bkt_meta, rel_bias, sinks)


def _attn_meta_kernel(e, q_ref, k_ref, v_ref, bkt_ref, rel_ref, sink_ref, o_ref):
    q = q_ref[...]
    k = k_ref[...]
    v = v_ref[...]
    bkt = bkt_ref[...]
    outs = []
    for h in range(A_HEADS):
        dims = slice((h // A_GROUP) * HEAD_DIM, (h // A_GROUP + 1) * HEAD_DIM)
        s = jax.lax.dot_general(q[:, h * HEAD_DIM:(h + 1) * HEAD_DIM], k[:, dims], _NT,
                                preferred_element_type=jnp.float32)
        s = s + _bias_from_buckets(bkt, rel_ref, h)
        sink = sink_ref[e, h]
        m = jnp.maximum(jnp.max(s, axis=-1, keepdims=True), sink)
        p = jnp.exp(s - m)
        den = jnp.sum(p, axis=-1, keepdims=True) + jnp.exp(sink - m)
        o = jnp.dot(p.astype(jnp.bfloat16), v[:, dims], preferred_element_type=jnp.float32)
        outs.append(o * (1.0 / den))
    o_ref[...] = jnp.concatenate(outs, axis=-1).astype(o_ref.dtype)


def _attention_meta(q, k, v, bkt_self, rel_bias, sinks, e):
    vmem = pl.BlockSpec(memory_space=pltpu.VMEM)
    smem = pl.BlockSpec(memory_space=pltpu.SMEM)
    return pl.pallas_call(
        functools.partial(_attn_meta_kernel, e),
        in_specs=[vmem, vmem, vmem, vmem, smem, smem],
        out_specs=vmem,
        out_shape=jax.ShapeDtypeStruct(q.shape, jnp.bfloat16),
        name="meta_attention",
    )(q, k, v, bkt_self, rel_bias, sinks)


def _t5_bucket(dist):
    n = jnp.maximum(dist, 0)
    max_exact = REL_BUCKETS // 2
    nf = jnp.maximum(n, 1).astype(jnp.float32)
    large = max_exact + (jnp.log(nf / max_exact) / math.log(REL_MAX_DIST / max_exact)
                         * (REL_BUCKETS - max_exact)).astype(jnp.int32)
    large = jnp.minimum(large, REL_BUCKETS - 1)
    return jnp.where(n < max_exact, n, large)


def _bucket_tables(seq):
    bucket = lambda dist: jnp.where(dist >= 0, _t5_bucket(dist), -1).astype(jnp.int32)
    idx = jnp.arange(BLOCK)
    comb = bucket((idx[None, :] - idx[:, None]) % BLOCK)
    m = jnp.arange(N_META)
    pos = N_META + jnp.arange(seq).reshape(seq // BLOCK, 1, BLOCK)
    meta = bucket(pos - m[None, :, None]).reshape(seq // BLOCK * N_META, BLOCK)
    self_ = bucket(m[:, None] - m[None, :])
    return comb, meta, self_


def _outproj_rows(x, ya, yc, wo_ref, g_ref):
    mix = _mm(ya, wo_ref[0:A_WIDTH, :]) + _mm(yc, wo_ref[A_WIDTH:, :])
    return x + _rms(mix, g_ref[3:4, :])


def _outproj_ffn_kernel(layer, with_meta, h_ref, ya_ref, yc_ref, hm_ref, yam_ref, ycm_ref, g_ref, wo_ref,
                        w1_hbm, w2_hbm, o_ref, om_ref, *scr):
    w1, w2 = scr[:2]

    def meta_rows():
        if with_meta:
            xm = _outproj_rows(hm_ref[...], yam_ref[...], ycm_ref[...], wo_ref, g_ref)
            om_ref[...] = _ffn_rows(xm, g_ref, 4, 5, w1, w2)
        else:
            om_ref[...] = hm_ref[...]

    def front(rows):
        o_ref[rows, :] = _outproj_rows(h_ref[rows, :], ya_ref[rows, :], yc_ref[rows, :], wo_ref, g_ref)

    make_fronts = lambda: [[functools.partial(front, rows)] for rows in _sub_blocks(o_ref.shape[0])]
    _ffn_steps(lambda rows: o_ref[rows, :], o_ref, g_ref, 4, 5, w1_hbm.at[layer, 1], w2_hbm.at[layer, 1],
               scr, make_fronts, meta_rows)


def _outproj_ffn(h, ya, yc, hm, yam, ycm, mix_w_out, norm_g, ffn_w1, ffn_w2, layer, e, with_meta):
    n = h.shape[0]
    tm = OUTPROJ_FFN_ROW_TILE
    row = lambda w: pl.BlockSpec((tm, w), lambda i: (i, 0))
    any_spec = pl.BlockSpec(memory_space=pl.ANY)
    return pl.pallas_call(
        functools.partial(_outproj_ffn_kernel, layer, with_meta),
        grid=(n // tm,),
        in_specs=[row(D_MODEL), row(A_WIDTH), row(B_WIDTH), _const_spec(hm.shape), _const_spec(yam.shape),
                  _const_spec(ycm.shape), _slab_spec(norm_g, layer), _slab_spec(mix_w_out, e),
                  any_spec, any_spec],
        out_specs=[row(D_MODEL), pl.BlockSpec(hm.shape, lambda i: (0, 0))],
        out_shape=[jax.ShapeDtypeStruct(h.shape, h.dtype), jax.ShapeDtypeStruct(hm.shape, hm.dtype)],
        scratch_shapes=_ffn_scratch(),
        compiler_params=_params(1),
        name="outproj_ffn",
    )(h, ya, yc, hm, yam, ycm, norm_g, mix_w_out, ffn_w1, ffn_w2)


def _pool_fronts(x_ref, hist, t0, g_ref, pw_ref, ps_ref, u_scr, lvl_scr, out_ref):
    tm = x_ref.shape[0]
    gpre = g_ref[2:3, :]
    top = POOL_PAD + POOL_HALO
    sub = min(tm, FFN_SUB_ROWS)

    def prenorm(r):
        if r == 0:
            zero_pad = jnp.zeros((POOL_PAD, D_MODEL), jnp.float32)
            u_scr[0:POOL_PAD, :] = zero_pad
            u_scr[POOL_PAD:top, :] = _rms(hist, gpre)
            for slot in range(2):
                lvl_scr[slot, 0:POOL_PAD, :] = zero_pad[:, :POOL_GROUP_DIM]
        u_scr[top + r:top + r + sub, :] = _rms(x_ref[r:r + sub, :], gpre)

    def group(r, gi):
        w = POOL_SIZES[gi]
        lo, hi = POOL_PAD + r, top + r + sub
        cols = slice(gi * POOL_GROUP_DIM, (gi + 1) * POOL_GROUP_DIM)
        src, src_cols, d = u_scr, cols, 1
        while d < w:
            s = src[lo:hi, src_cols] + src[lo - d:hi - d, src_cols]
            d *= 2
            if d < w:
                dst = lvl_scr.at[(d.bit_length() - 1) % 2]
                dst[lo:hi, :] = s
                src, src_cols = dst, slice(None)
        ug = u_scr[top + r:top + r + sub, cols]
        t = t0 + r + jax.lax.broadcasted_iota(jnp.int32, (sub, 1), 0)
        cnt = jnp.minimum(t + 1, w).astype(jnp.float32)
        diff = (s[POOL_HALO:, :] / cnt - ug).astype(jnp.bfloat16)
        out_ref[r:r + sub, cols] = _mm(diff, pw_ref[gi]) * ps_ref[:, cols]

    def finish(r):
        rows = slice(r, r + sub)
        out_ref[rows, :] = x_ref[rows, :] + _rms(out_ref[rows, :], g_ref[3:4, :])

    return [[functools.partial(prenorm, r)]
            + [functools.partial(group, r, gi) for gi in range(N_POOL_GROUPS)]
            + [functools.partial(finish, r)] for r in range(0, tm, sub)]


def _pool_ffn_kernel(layer, tiles_per_seq, with_meta, h_ref, prev_ref, hm_ref, g_ref, pw_ref, ps_ref,
                     w1_hbm, w2_hbm, o_ref, om_ref, *scr):
    w1, w2 = scr[:2]
    u_scr, lvl_scr = scr[-2:]

    def meta_rows():
        if with_meta:
            no_hist = jnp.zeros((POOL_HALO, D_MODEL), jnp.float32)
            for thunk in _pool_fronts(hm_ref, no_hist, 0, g_ref, pw_ref, ps_ref, u_scr, lvl_scr, om_ref)[0]:
                thunk()
            om_ref[...] = _ffn_rows(om_ref[...], g_ref, 4, 5, w1, w2)
        else:
            om_ref[...] = hm_ref[...]

    def make_fronts():
        j = pl.program_id(0) % tiles_per_seq
        hist = jnp.where(j == 0, hm_ref[...], prev_ref[...])
        return _pool_fronts(h_ref, hist, N_META + j * h_ref.shape[0], g_ref, pw_ref, ps_ref, u_scr, lvl_scr,
                            o_ref)

    _ffn_steps(lambda rows: o_ref[rows, :], o_ref, g_ref, 4, 5, w1_hbm.at[layer, 1], w2_hbm.at[layer, 1],
               scr, make_fronts, meta_rows)


def _pool_ffn(h, hm, norm_g, pool_w, pool_scale, ffn_w1, ffn_w2, layer, o, seq, with_meta):
    n = h.shape[0]
    tm = FFN_ROW_TILE
    halo_blocks = tm // POOL_HALO
    row = pl.BlockSpec((tm, D_MODEL), lambda i: (i, 0))
    prev = pl.BlockSpec((POOL_HALO, D_MODEL), lambda i: (jnp.maximum(i * halo_blocks - 1, 0), 0))
    any_spec = pl.BlockSpec(memory_space=pl.ANY)
    return pl.pallas_call(
        functools.partial(_pool_ffn_kernel, layer, seq // tm, with_meta),
        grid=(n // tm,),
        in_specs=[row, prev, _const_spec(hm.shape), _slab_spec(norm_g, layer), _slab_spec(pool_w, o),
                  pl.BlockSpec((1, D_MODEL), lambda i: (o, 0), pipeline_mode=pl.Buffered(1)),
                  any_spec, any_spec],
        out_specs=[row, pl.BlockSpec(hm.shape, lambda i: (0, 0))],
        out_shape=[jax.ShapeDtypeStruct(h.shape, h.dtype), jax.ShapeDtypeStruct(hm.shape, hm.dtype)],
        scratch_shapes=_ffn_scratch() + [
            pltpu.VMEM((POOL_PAD + POOL_HALO + tm, D_MODEL), jnp.float32),
            pltpu.VMEM((2, POOL_PAD + POOL_HALO + tm, POOL_GROUP_DIM), jnp.float32)],
        compiler_params=_params(1),
        name="pool_ffn",
    )(h, h, hm, norm_g, pool_w, pool_scale, ffn_w1, ffn_w2)


def kernel(x, meta_tokens, rel_bias, norm_g, ffn_w1, ffn_w2, mix_w_in, mix_w_out, attn_sinks,
           conv_w, pool_w, pool_scale):
    batch, seq, _ = x.shape
    depth = norm_g.shape[0]
    h = x.reshape(batch * seq, D_MODEL)
    hm = meta_tokens.astype(x.dtype)
    bkt_comb, bkt_meta, bkt_self = _bucket_tables(seq)
    for layer in range(depth):
        last = layer == depth - 1
        h, hm = _ffn(h, hm, norm_g, ffn_w1, ffn_w2, layer, 0, True)
        if layer % 2 == 0:
            e = layer // 2
            no_hist = jnp.zeros((CONV_HALO, B_WIDTH), jnp.float32)
            qm, km, vm, ycm, tail_m = _inproj(hm, norm_g, mix_w_in, conv_w, no_hist, layer, e, N_META, False)
            q, k, vt, yc, _ = _inproj(h, norm_g, mix_w_in, conv_w, tail_m, layer, e, seq, True)
            ya = _attention(q, k, vt, km, vm.T, bkt_comb, bkt_meta, rel_bias, attn_sinks, e, batch, seq)
            yam = _attention_meta(qm, km, vm, bkt_self, rel_bias, attn_sinks, e) if not last else ycm
            h, hm = _outproj_ffn(h, ya, yc, hm, yam, ycm, mix_w_out, norm_g, ffn_w1, ffn_w2, layer, e,
                                 not last)
        else:
            h, hm = _pool_ffn(h, hm, norm_g, pool_w, pool_scale, ffn_w1, ffn_w2, layer, layer // 2, seq,
                              not last)
    return h.reshape(batch, seq, D_MODEL)
```

```python
import functools

import jax
import jax.numpy as jnp
import numpy as np
from jax.experimental import pallas as pl
from jax.experimental.pallas import tpu as pltpu

D_MODEL = 1024
N_META = 16
A_HEADS = 8
A_KV_HEADS = 2
A_GROUP = A_HEADS // A_KV_HEADS
HEAD_DIM = 64
WINDOW = 128
BLOCK = 128
A_WIDTH = A_HEADS * HEAD_DIM
KV_WIDTH = A_KV_HEADS * HEAD_DIM
B_WIDTH = D_MODEL // 2
CONV_WIDTH = 3
IN_WIDTH = A_WIDTH + 2 * KV_WIDTH + 3 * B_WIDTH
POOL_SIZES = (2, 4, 8, 16)
N_POOL_GROUPS = 4
POOL_GROUP_DIM = D_MODEL // N_POOL_GROUPS
REL_BUCKETS = 32
REL_MAX_DIST = 128
D_FF = 2816
HALF_STEP = 0.5
RMS_EPS = 1e-6

VMEM_LIMIT_BYTES_V7X = 56 * 1024 * 1024
ROW_TILE = 2048
ROW_SUB = 1024
FFN_ROW_TILE = 1024
OUTPROJ_FFN_ROW_TILE = 512
FFN_SUB_ROWS = 512
FF_CHUNK = 256
N_FF_CHUNKS = D_FF // FF_CHUNK
N_WSLOTS = 2
POOL_HALO = 16
POOL_PAD = 8
CONV_HALO = 8
ATTN_QBLOCKS = 16
GROUP_LANES = A_GROUP * BLOCK

NEG_INF = float("-inf")
_NT = (((1,), (1,)), ((), ()))

assert WINDOW == BLOCK and N_META == POOL_HALO
assert REL_MAX_DIST % (REL_BUCKETS // 2) == 0


def _rms(x, g):
    return x * jax.lax.rsqrt(jnp.mean(x * x, axis=-1, keepdims=True) + RMS_EPS) * g


def _const_spec(shape):
    zeros = (0,) * len(shape)
    return pl.BlockSpec(shape, lambda *_: zeros, pipeline_mode=pl.Buffered(1))


def _slab_spec(arr, *lead):
    rest = arr.shape[len(lead):]
    idx = tuple(lead) + (0,) * len(rest)
    return pl.BlockSpec((None,) * len(lead) + rest, lambda *_: idx, pipeline_mode=pl.Buffered(1))


def _mm(a, w):
    return jax.lax.dot_general(a, w, (((1,), (0,)), ((), ())), preferred_element_type=jnp.float32)


def _params(n_axes):
    return pltpu.CompilerParams(dimension_semantics=("arbitrary",) * n_axes,
                                vmem_limit_bytes=VMEM_LIMIT_BYTES_V7X)


def _chunk_loader(w1_src, w2_src, w1, w2, stage1, stage2, sem):
    def copies(c):
        slot = c % N_WSLOTS
        rows = pl.ds(c * FF_CHUNK, FF_CHUNK)
        return [pltpu.make_async_copy(w1_src.at[:, pl.ds(c * FF_CHUNK, FF_CHUNK)], stage1.at[slot, 0],
                                      sem.at[slot, 0]),
                pltpu.make_async_copy(w1_src.at[:, pl.ds(D_FF + c * FF_CHUNK, FF_CHUNK)], stage1.at[slot, 1],
                                      sem.at[slot, 1]),
                pltpu.make_async_copy(w2_src.at[rows, :], stage2.at[slot], sem.at[slot, 2])]

    def start(c):
        for copy in copies(c):
            copy.start()

    def before_chunk(c):
        if c == 0:
            for k in range(min(N_WSLOTS, N_FF_CHUNKS)):
                start(k)
        for copy in copies(c):
            copy.wait()
        slot = c % N_WSLOTS
        cols = slice(c * FF_CHUNK, (c + 1) * FF_CHUNK)
        w1[:, cols] = stage1[slot, 0].astype(w1.dtype)
        w1[:, D_FF + c * FF_CHUNK:D_FF + (c + 1) * FF_CHUNK] = stage1[slot, 1].astype(w1.dtype)
        w2[cols, :] = stage2[slot].astype(w2.dtype)
        if c + N_WSLOTS < N_FF_CHUNKS:
            start(c + N_WSLOTS)

    return before_chunk


def _swiglu(xn, w1, w2, side=(), before_chunk=None):
    side = list(side)
    acts = []
    for c in range(N_FF_CHUNKS):
        if before_chunk is not None:
            before_chunk(c)
        cols = slice(c * FF_CHUNK, (c + 1) * FF_CHUNK)
        gate = _mm(xn, w1[:, cols])
        up = _mm(xn, w1[:, D_FF + c * FF_CHUNK:D_FF + (c + 1) * FF_CHUNK])
        acts.append((gate * (1.0 / (1.0 + jnp.exp(-gate))) * up).astype(jnp.bfloat16))
        if side:
            side.pop(0)()
    for thunk in side:
        thunk()
    return _mm(jnp.concatenate(acts, axis=1), w2[...])


def _ffn_rows(x, g_ref, pre, post, w1, w2):
    xn = _rms(x, g_ref[pre:pre + 1, :]).astype(jnp.bfloat16)
    return x + HALF_STEP * _rms(_swiglu(xn, w1, w2), g_ref[post:post + 1, :])


def _ffn_scratch():
    return [pltpu.VMEM((D_MODEL, 2 * D_FF), jnp.bfloat16),
            pltpu.VMEM((D_FF, D_MODEL), jnp.bfloat16),
            pltpu.VMEM((N_WSLOTS, 2, D_MODEL, FF_CHUNK), jnp.float32),
            pltpu.VMEM((N_WSLOTS, FF_CHUNK, D_MODEL), jnp.float32),
            pltpu.SemaphoreType.DMA((N_WSLOTS, 3))]


def _sub_blocks(n_rows):
    return [slice(r, r + FFN_SUB_ROWS) for r in range(0, n_rows, FFN_SUB_ROWS)]


def _ffn_tile(read_x, o_ref, g_ref, pre, post, w1, w2, fronts=None, before_chunk=None):
    subs = _sub_blocks(o_ref.shape[0])
    fronts = fronts or [[] for _ in subs]
    prenorm = lambda k: _rms(read_x(subs[k]), g_ref[pre:pre + 1, :]).astype(jnp.bfloat16)
    xns, accs = {}, {}

    def finish(k):
        o_ref[subs[k], :] = read_x(subs[k]) + HALF_STEP * _rms(accs[k], g_ref[post:post + 1, :])

    for thunk in fronts[0]:
        thunk()
    xns[0] = prenorm(0)
    for k in range(len(subs)):
        side = []
        if k > 0:
            side.append(functools.partial(finish, k - 1))
        if k + 1 < len(subs):
            side += fronts[k + 1]
            side.append(lambda k=k: xns.__setitem__(k + 1, prenorm(k + 1)))
        accs[k] = _swiglu(xns[k], w1, w2, side, before_chunk if k == 0 else None)
    finish(len(subs) - 1)


def _ffn_steps(read_x, o_ref, g_ref, pre, post, w1_src, w2_src, scr, make_fronts, first_step_tail):
    w1, w2 = scr[:2]

    @pl.when(pl.program_id(0) == 0)
    def _():
        _ffn_tile(read_x, o_ref, g_ref, pre, post, w1, w2, make_fronts(),
                  _chunk_loader(w1_src, w2_src, *scr[:5]))
        first_step_tail()

    @pl.when(pl.program_id(0) > 0)
    def _():
        _ffn_tile(read_x, o_ref, g_ref, pre, post, w1, w2, make_fronts())


def _ffn_kernel(layer, half, with_meta, h_ref, hm_ref, g_ref, w1_hbm, w2_hbm, o_ref, om_ref, *scr):
    pre, post = 4 * half, 4 * half + 1
    w1, w2 = scr[:2]

    def meta_rows():
        if with_meta:
            om_ref[...] = _ffn_rows(hm_ref[...], g_ref, pre, post, w1, w2)
        else:
            om_ref[...] = hm_ref[...]

    _ffn_steps(lambda rows: h_ref[rows, :], o_ref, g_ref, pre, post, w1_hbm.at[layer, half],
               w2_hbm.at[layer, half], scr, lambda: None, meta_rows)


def _ffn(h, hm, norm_g, ffn_w1, ffn_w2, layer, half, with_meta):
    n = h.shape[0]
    tm = FFN_ROW_TILE
    row = pl.BlockSpec((tm, D_MODEL), lambda i: (i, 0))
    any_spec = pl.BlockSpec(memory_space=pl.ANY)
    return pl.pallas_call(
        functools.partial(_ffn_kernel, layer, half, with_meta),
        grid=(n // tm,),
        in_specs=[row, _const_spec(hm.shape), _slab_spec(norm_g, layer), any_spec, any_spec],
        out_specs=[row, pl.BlockSpec(hm.shape, lambda i: (0, 0))],
        out_shape=[jax.ShapeDtypeStruct(h.shape, h.dtype), jax.ShapeDtypeStruct(hm.shape, hm.dtype)],
        scratch_shapes=_ffn_scratch(),
        compiler_params=_params(1),
        name="ffn",
    )(h, hm, norm_g, ffn_w1, ffn_w2)


def _inproj_kernel(tiles_per_seq, transpose_v, h_ref, g_ref, win_ref, convw_ref, hist_ref,
                   q_ref, k_ref, v_ref, yc_ref, tail_ref, uc_scr):
    i = pl.program_id(0)
    tm = h_ref.shape[0]

    @pl.when(i % tiles_per_seq == 0)
    def _():
        uc_scr[0:CONV_HALO, :] = hist_ref[...]

    sub = min(tm, ROW_SUB)
    for r in range(0, tm, sub):
        rows = slice(r, r + sub)
        u = _rms(h_ref[rows, :], g_ref[2:3, :]).astype(jnp.bfloat16)
        z = _mm(u, win_ref[...])
        q_ref[rows, :] = (z[:, :A_WIDTH] * (HEAD_DIM ** -0.5)).astype(jnp.bfloat16)
        k_ref[rows, :] = z[:, A_WIDTH:A_WIDTH + KV_WIDTH].astype(jnp.bfloat16)
        v = z[:, A_WIDTH + KV_WIDTH:A_WIDTH + 2 * KV_WIDTH]
        if transpose_v:
            v_ref[:, rows] = v.T.astype(jnp.bfloat16)
        else:
            v_ref[rows, :] = v.astype(jnp.bfloat16)
        o = A_WIDTH + 2 * KV_WIDTH
        b_gate = z[:, o:o + B_WIDTH]
        c_gate = z[:, o + B_WIDTH:o + 2 * B_WIDTH]
        x_in = z[:, o + 2 * B_WIDTH:o + 3 * B_WIDTH]
        uc = c_gate * x_in
        uc_scr[CONV_HALO + r:CONV_HALO + r + sub, :] = uc
        y = convw_ref[0:1, :] * uc
        for j in range(1, CONV_WIDTH):
            y = y + convw_ref[j:j + 1, :] * uc_scr[CONV_HALO + r - j:CONV_HALO + r - j + sub, :]
        yc_ref[rows, :] = (b_gate * y).astype(jnp.bfloat16)
    tail = uc_scr[tm:tm + CONV_HALO, :]
    tail_ref[...] = tail
    uc_scr[0:CONV_HALO, :] = tail


def _inproj(h, norm_g, mix_w_in, conv_w, hist, layer, e, rows_per_seq, transpose_v):
    n = h.shape[0]
    tm = min(ROW_TILE, n)
    row = lambda w: pl.BlockSpec((tm, w), lambda i: (i, 0))
    bf = lambda *shape: jax.ShapeDtypeStruct(shape, jnp.bfloat16)
    if transpose_v:
        v_spec, v_shape = pl.BlockSpec((KV_WIDTH, tm), lambda i: (0, i)), bf(KV_WIDTH, n)
    else:
        v_spec, v_shape = row(KV_WIDTH), bf(n, KV_WIDTH)
    return pl.pallas_call(
        functools.partial(_inproj_kernel, rows_per_seq // tm, transpose_v),
        grid=(n // tm,),
        in_specs=[row(D_MODEL), _slab_spec(norm_g, layer), _slab_spec(mix_w_in, e),
                  _slab_spec(conv_w, e), _const_spec((CONV_HALO, B_WIDTH))],
        out_specs=[row(A_WIDTH), row(KV_WIDTH), v_spec, row(B_WIDTH),
                   pl.BlockSpec((CONV_HALO, B_WIDTH), lambda i: (0, 0))],
        out_shape=[bf(n, A_WIDTH), bf(n, KV_WIDTH), v_shape, bf(n, B_WIDTH),
                   jax.ShapeDtypeStruct((CONV_HALO, B_WIDTH), jnp.float32)],
        scratch_shapes=[pltpu.VMEM((CONV_HALO + tm, B_WIDTH), jnp.float32)],
        compiler_params=_params(1),
        name="inproj_conv",
    )(h, norm_g, mix_w_in, conv_w, hist)


def _bias_from_buckets(bucket, rel_ref, head):
    body = lambda b, acc: jnp.where(bucket == b, rel_ref[b, head], acc)
    acc = jax.lax.fori_loop(0, REL_BUCKETS, body, jnp.zeros(bucket.shape, jnp.float32))
    return jnp.where(bucket >= 0, acc, NEG_INF)


def _attn_kernel(e, q_ref, kp_ref, kc_ref, vtp_ref, vtc_ref, km_ref, vtm_ref, bktc_ref, bktm_ref,
                 rel_ref, sink_ref, o_ref, bias_comb_scr, bias_meta_scr, sink_scr, st_scr, p_scr, rden_scr,
                 yt_scr):
    j = pl.program_id(1)
    key_idx = jax.lax.broadcasted_iota(jnp.int32, (BLOCK, GROUP_LANES), 0)
    qry_idx = jax.lax.broadcasted_iota(jnp.int32, (BLOCK, GROUP_LANES), 1) % BLOCK
    from_prev = key_idx > qry_idx

    @pl.when(jnp.logical_and(pl.program_id(0) == 0, j == 0))
    def _():
        bktc = bktc_ref[...]
        bktm = bktm_ref[...]
        for h in range(A_HEADS):
            g, i = divmod(h, A_GROUP)
            lanes = slice(i * BLOCK, (i + 1) * BLOCK)
            bias = _bias_from_buckets(bktc, rel_ref, h)
            bias_comb_scr[0, g, :, lanes] = bias
            bias_comb_scr[1, g, :, lanes] = jnp.where(from_prev[:, :BLOCK], NEG_INF, bias)
            bias_meta_scr[:, h * BLOCK:(h + 1) * BLOCK] = _bias_from_buckets(bktm, rel_ref, h)
            sink_scr[g, :, lanes] = jnp.full((8, BLOCK), sink_ref[e, h], jnp.float32)

    kcat = jnp.concatenate([kp_ref[...], kc_ref[...]], axis=0)
    vtcat = jnp.concatenate([vtp_ref[...], vtc_ref[...]], axis=1)
    km = km_ref[...]
    vtm = vtm_ref[...]
    chains = [(qb, g) for qb in range(ATTN_QBLOCKS) for g in range(A_KV_HEADS)]
    for ci, (qb, g) in enumerate(chains):
        dims = slice(g * HEAD_DIM, (g + 1) * HEAD_DIM)
        q = q_ref[qb * BLOCK:(qb + 1) * BLOCK, :]
        q_stack = jnp.concatenate(
            [q[:, (g * A_GROUP + i) * HEAD_DIM:(g * A_GROUP + i + 1) * HEAD_DIM]
             for i in range(A_GROUP)], axis=0)
        k_all = jnp.concatenate([kcat[qb * BLOCK:(qb + 2) * BLOCK, dims], km[:, dims]], axis=0)
        st_scr[ci] = jax.lax.dot_general(k_all, q_stack, _NT, preferred_element_type=jnp.float32)
    for ci, (qb, g) in enumerate(chains):
        variant = jnp.where(j == 0, 1, 0) if qb == 0 else 0
        meta_row = pl.multiple_of((j * ATTN_QBLOCKS + qb) * N_META, N_META)
        comb = (jnp.where(from_prev, st_scr[ci, 0:BLOCK, :], st_scr[ci, BLOCK:2 * BLOCK, :])
                + bias_comb_scr[variant, g])
        meta = st_scr[ci, 2 * BLOCK:, :] + bias_meta_scr[pl.ds(meta_row, N_META),
                                                         g * GROUP_LANES:(g + 1) * GROUP_LANES]
        sink = sink_scr[g, 0:1, :]
        m = jnp.maximum(jnp.maximum(jnp.max(comb, axis=0, keepdims=True),
                                    jnp.max(meta, axis=0, keepdims=True)), sink)
        pc = jnp.exp(comb - m)
        pm = jnp.exp(meta - m)
        den = (jnp.sum(pc, axis=0, keepdims=True) + jnp.sum(pm, axis=0, keepdims=True)
               + jnp.exp(sink - m))
        p_scr[ci, 0:BLOCK, :] = jnp.where(from_prev, pc, 0.0).astype(jnp.bfloat16)
        p_scr[ci, BLOCK:2 * BLOCK, :] = jnp.where(from_prev, 0.0, pc).astype(jnp.bfloat16)
        p_scr[ci, 2 * BLOCK:, :] = pm.astype(jnp.bfloat16)
        rden_scr[ci] = jnp.broadcast_to(1.0 / den, (8, GROUP_LANES))
    for ci, (qb, g) in enumerate(chains):
        dims = slice(g * HEAD_DIM, (g + 1) * HEAD_DIM)
        ot = jnp.dot(vtcat[dims, qb * BLOCK:(qb + 2) * BLOCK], p_scr[ci, 0:2 * BLOCK, :],
                     preferred_element_type=jnp.float32)
        ot = ot + jnp.dot(vtm[dims, :], p_scr[ci, 2 * BLOCK:, :], preferred_element_type=jnp.float32)
        ot = ot * rden_scr[ci, 0:1, :]
        for i in range(A_GROUP):
            h = g * A_GROUP + i
            yt_scr[qb, h * HEAD_DIM:(h + 1) * HEAD_DIM, :] = ot[:, i * BLOCK:(i + 1) * BLOCK]
    for qb in range(ATTN_QBLOCKS):
        o_ref[qb * BLOCK:(qb + 1) * BLOCK, :] = yt_scr[qb].T.astype(o_ref.dtype)


def _attention(q, k, vt, k_meta, vt_meta, bkt_comb, bkt_meta, rel_bias, sinks, e, batch, seq):
    nb = seq // BLOCK
    qb = ATTN_QBLOCKS
    tiles = nb // qb
    cur_rows = lambda w: pl.BlockSpec((qb * BLOCK, w), lambda b, j: (b * tiles + j, 0))
    prev_blk = lambda b, j: b * nb + jnp.maximum(j * qb - 1, 0)
    smem = pl.BlockSpec(memory_space=pltpu.SMEM)
    return pl.pallas_call(
        functools.partial(_attn_kernel, e),
        grid=(batch, tiles),
        in_specs=[cur_rows(A_WIDTH),
                  pl.BlockSpec((BLOCK, KV_WIDTH), lambda b, j: (prev_blk(b, j), 0)),
                  cur_rows(KV_WIDTH),
                  pl.BlockSpec((KV_WIDTH, BLOCK), lambda b, j: (0, prev_blk(b, j))),
                  pl.BlockSpec((KV_WIDTH, qb * BLOCK), lambda b, j: (0, b * tiles + j)),
                  _const_spec(k_meta.shape), _const_spec(vt_meta.shape),
                  _const_spec(bkt_comb.shape), _const_spec(bkt_meta.shape), smem, smem],
        out_specs=cur_rows(A_WIDTH),
        out_shape=jax.ShapeDtypeStruct(q.shape, jnp.bfloat16),
        scratch_shapes=[pltpu.VMEM((2, A_KV_HEADS, BLOCK, GROUP_LANES), jnp.float32),
                        pltpu.VMEM((nb * N_META, A_HEADS * BLOCK), jnp.float32),
                        pltpu.VMEM((A_KV_HEADS, 8, GROUP_LANES), jnp.float32),
                        pltpu.VMEM((qb * A_KV_HEADS, 2 * BLOCK + N_META, GROUP_LANES), jnp.float32),
                        pltpu.VMEM((qb * A_KV_HEADS, 2 * BLOCK + N_META, GROUP_LANES), jnp.bfloat16),
                        pltpu.VMEM((qb * A_KV_HEADS, 8, GROUP_LANES), jnp.float32),
                        pltpu.VMEM((qb, A_WIDTH, BLOCK), jnp.float32)],
        compiler_params=_params(2),
        name="swa_attention",
    )(q, k, k, vt, vt, k_meta, vt_meta, bkt_comb, bkt_meta, rel_bias, sinks)


def _attn_meta_kernel(e, q_ref, k_ref, v_ref, bkt_ref, rel_ref, sink_ref, o_ref):
    q = q_ref[...]
    k = k_ref[...]
    v = v_ref[...]
    bkt = bkt_ref[...]
    outs = []
    for h in range(A_HEADS):
        dims = slice((h // A_GROUP) * HEAD_DIM, (h // A_GROUP + 1) * HEAD_DIM)
        s = jax.lax.dot_general(q[:, h * HEAD_DIM:(h + 1) * HEAD_DIM], k[:, dims], _NT,
                                preferred_element_type=jnp.float32)
        s = s + _bias_from_buckets(bkt, rel_ref, h)
        sink = sink_ref[e, h]
        m = jnp.maximum(jnp.max(s, axis=-1, keepdims=True), sink)
        p = jnp.exp(s - m)
        den = jnp.sum(p, axis=-1, keepdims=True) + jnp.exp(sink - m)
        o = jnp.dot(p.astype(jnp.bfloat16), v[:, dims], preferred_element_type=jnp.float32)
        outs.append(o * (1.0 / den))
    o_ref[...] = jnp.concatenate(outs, axis=-1).astype(o_ref.dtype)


def _attention_meta(q, k, v, bkt_self, rel_bias, sinks, e):
    vmem = pl.BlockSpec(memory_space=pltpu.VMEM)
    smem = pl.BlockSpec(memory_space=pltpu.SMEM)
    return pl.pallas_call(
        functools.partial(_attn_meta_kernel, e),
        in_specs=[vmem, vmem, vmem, vmem, smem, smem],
        out_specs=vmem,
        out_shape=jax.ShapeDtypeStruct(q.shape, jnp.bfloat16),
        name="meta_attention",
    )(q, k, v, bkt_self, rel_bias, sinks)


def _t5_bucket(n):
    max_exact = REL_BUCKETS // 2
    span = REL_BUCKETS - max_exact
    if n < max_exact:
        return n
    k = 0
    while (REL_MAX_DIST // max_exact) ** (k + 1) * max_exact ** span <= n ** span:
        k += 1
    return min(max_exact + k, REL_BUCKETS - 1)


def _bucket_tables(seq):
    by_dist = np.array([_t5_bucket(n) for n in range(N_META + seq)], np.int32)
    bucket = lambda dist: np.where(dist >= 0, by_dist[np.maximum(dist, 0)], -1).astype(np.int32)
    idx = np.arange(BLOCK)
    comb = bucket((idx[None, :] - idx[:, None]) % BLOCK)
    m = np.arange(N_META)
    pos = N_META + np.arange(seq).reshape(seq // BLOCK, 1, BLOCK)
    meta = bucket(pos - m[None, :, None]).reshape(seq // BLOCK * N_META, BLOCK)
    self_ = bucket(m[:, None] - m[None, :])
    return jnp.asarray(comb), jnp.asarray(meta), jnp.asarray(self_)


def _outproj_rows(x, ya, yc, wo_ref, g_ref):
    mix = _mm(ya, wo_ref[0:A_WIDTH, :]) + _mm(yc, wo_ref[A_WIDTH:, :])
    return x + _rms(mix, g_ref[3:4, :])


def _outproj_ffn_kernel(layer, with_meta, h_ref, ya_ref, yc_ref, hm_ref, yam_ref, ycm_ref, g_ref, wo_ref,
                        w1_hbm, w2_hbm, o_ref, om_ref, *scr):
    w1, w2 = scr[:2]

    def meta_rows():
        if with_meta:
            xm = _outproj_rows(hm_ref[...], yam_ref[...], ycm_ref[...], wo_ref, g_ref)
            om_ref[...] = _ffn_rows(xm, g_ref, 4, 5, w1, w2)
        else:
            om_ref[...] = hm_ref[...]

    def front(rows):
        o_ref[rows, :] = _outproj_rows(h_ref[rows, :], ya_ref[rows, :], yc_ref[rows, :], wo_ref, g_ref)

    make_fronts = lambda: [[functools.partial(front, rows)] for rows in _sub_blocks(o_ref.shape[0])]
    _ffn_steps(lambda rows: o_ref[rows, :], o_ref, g_ref, 4, 5, w1_hbm.at[layer, 1], w2_hbm.at[layer, 1],
               scr, make_fronts, meta_rows)


def _outproj_ffn(h, ya, yc, hm, yam, ycm, mix_w_out, norm_g, ffn_w1, ffn_w2, layer, e, with_meta):
    n = h.shape[0]
    tm = OUTPROJ_FFN_ROW_TILE
    row = lambda w: pl.BlockSpec((tm, w), lambda i: (i, 0))
    any_spec = pl.BlockSpec(memory_space=pl.ANY)
    return pl.pallas_call(
        functools.partial(_outproj_ffn_kernel, layer, with_meta),
        grid=(n // tm,),
        in_specs=[row(D_MODEL), row(A_WIDTH), row(B_WIDTH), _const_spec(hm.shape), _const_spec(yam.shape),
                  _const_spec(ycm.shape), _slab_spec(norm_g, layer), _slab_spec(mix_w_out, e),
                  any_spec, any_spec],
        out_specs=[row(D_MODEL), pl.BlockSpec(hm.shape, lambda i: (0, 0))],
        out_shape=[jax.ShapeDtypeStruct(h.shape, h.dtype), jax.ShapeDtypeStruct(hm.shape, hm.dtype)],
        scratch_shapes=_ffn_scratch(),
        compiler_params=_params(1),
        name="outproj_ffn",
    )(h, ya, yc, hm, yam, ycm, norm_g, mix_w_out, ffn_w1, ffn_w2)


def _pool_fronts(x_ref, hist, t0, g_ref, pw_ref, ps_ref, u_scr, lvl_scr, out_ref):
    tm = x_ref.shape[0]
    gpre = g_ref[2:3, :]
    top = POOL_PAD + POOL_HALO
    sub = min(tm, FFN_SUB_ROWS)

    def prenorm(r):
        if r == 0:
            zero_pad = jnp.zeros((POOL_PAD, D_MODEL), jnp.float32)
            u_scr[0:POOL_PAD, :] = zero_pad
            u_scr[POOL_PAD:top, :] = _rms(hist, gpre)
            for slot in range(2):
                lvl_scr[slot, 0:POOL_PAD, :] = zero_pad[:, :POOL_GROUP_DIM]
        u_scr[top + r:top + r + sub, :] = _rms(x_ref[r:r + sub, :], gpre)

    def group(r, gi):
        w = POOL_SIZES[gi]
        lo, hi = POOL_PAD + r, top + r + sub
        cols = slice(gi * POOL_GROUP_DIM, (gi + 1) * POOL_GROUP_DIM)
        src, src_cols, d = u_scr, cols, 1
        while d < w:
            s = src[lo:hi, src_cols] + src[lo - d:hi - d, src_cols]
            d *= 2
            if d < w:
                dst = lvl_scr.at[(d.bit_length() - 1) % 2]
                dst[lo:hi, :] = s
                src, src_cols = dst, slice(None)
        ug = u_scr[top + r:top + r + sub, cols]
        t = t0 + r + jax.lax.broadcasted_iota(jnp.int32, (sub, 1), 0)
        cnt = jnp.minimum(t + 1, w).astype(jnp.float32)
        diff = (s[POOL_HALO:, :] / cnt - ug).astype(jnp.bfloat16)
        out_ref[r:r + sub, cols] = _mm(diff, pw_ref[gi]) * ps_ref[:, cols]

    def finish(r):
        rows = slice(r, r + sub)
        out_ref[rows, :] = x_ref[rows, :] + _rms(out_ref[rows, :], g_ref[3:4, :])

    return [[functools.partial(prenorm, r)]
            + [functools.partial(group, r, gi) for gi in range(N_POOL_GROUPS)]
            + [functools.partial(finish, r)] for r in range(0, tm, sub)]


def _pool_ffn_kernel(layer, tiles_per_seq, with_meta, h_ref, prev_ref, hm_ref, g_ref, pw_ref, ps_ref,
                     w1_hbm, w2_hbm, o_ref, om_ref, *scr):
    w1, w2 = scr[:2]
    u_scr, lvl_scr = scr[-2:]

    def meta_rows():
        if with_meta:
            no_hist = jnp.zeros((POOL_HALO, D_MODEL), jnp.float32)
            for thunk in _pool_fronts(hm_ref, no_hist, 0, g_ref, pw_ref, ps_ref, u_scr, lvl_scr, om_ref)[0]:
                thunk()
            om_ref[...] = _ffn_rows(om_ref[...], g_ref, 4, 5, w1, w2)
        else:
            om_ref[...] = hm_ref[...]

    def make_fronts():
        j = pl.program_id(0) % tiles_per_seq
        hist = jnp.where(j == 0, hm_ref[...], prev_ref[...])
        return _pool_fronts(h_ref, hist, N_META + j * h_ref.shape[0], g_ref, pw_ref, ps_ref, u_scr, lvl_scr,
                            o_ref)

    _ffn_steps(lambda rows: o_ref[rows, :], o_ref, g_ref, 4, 5, w1_hbm.at[layer, 1], w2_hbm.at[layer, 1],
               scr, make_fronts, meta_rows)


def _pool_ffn(h, hm, norm_g, pool_w, pool_scale, ffn_w1, ffn_w2, layer, o, seq, with_meta):
    n = h.shape[0]
    tm = FFN_ROW_TILE
    halo_blocks = tm // POOL_HALO
    row = pl.BlockSpec((tm, D_MODEL), lambda i: (i, 0))
    prev = pl.BlockSpec((POOL_HALO, D_MODEL), lambda i: (jnp.maximum(i * halo_blocks - 1, 0), 0))
    any_spec = pl.BlockSpec(memory_space=pl.ANY)
    return pl.pallas_call(
        functools.partial(_pool_ffn_kernel, layer, seq // tm, with_meta),
        grid=(n // tm,),
        in_specs=[row, prev, _const_spec(hm.shape), _slab_spec(norm_g, layer), _slab_spec(pool_w, o),
                  pl.BlockSpec((1, D_MODEL), lambda i: (o, 0), pipeline_mode=pl.Buffered(1)),
                  any_spec, any_spec],
        out_specs=[row, pl.BlockSpec(hm.shape, lambda i: (0, 0))],
        out_shape=[jax.ShapeDtypeStruct(h.shape, h.dtype), jax.ShapeDtypeStruct(hm.shape, hm.dtype)],
        scratch_shapes=_ffn_scratch() + [
            pltpu.VMEM((POOL_PAD + POOL_HALO + tm, D_MODEL), jnp.float32),
            pltpu.VMEM((2, POOL_PAD + POOL_HALO + tm, POOL_GROUP_DIM), jnp.float32)],
        compiler_params=_params(1),
        name="pool_ffn",
    )(h, h, hm, norm_g, pool_w, pool_scale, ffn_w1, ffn_w2)


def kernel(x, meta_tokens, rel_bias, norm_g, ffn_w1, ffn_w2, mix_w_in, mix_w_out, attn_sinks,
           conv_w, pool_w, pool_scale):
    batch, seq, _ = x.shape
    depth = norm_g.shape[0]
    h = x.reshape(batch * seq, D_MODEL)
    hm = meta_tokens.astype(x.dtype)
    bkt_comb, bkt_meta, bkt_self = _bucket_tables(seq)
    for layer in range(depth):
        last = layer == depth - 1
        h, hm = _ffn(h, hm, norm_g, ffn_w1, ffn_w2, layer, 0, True)
        if layer % 2 == 0:
            e = layer // 2
            no_hist = jnp.zeros((CONV_HALO, B_WIDTH), jnp.float32)
            qm, km, vm, ycm, tail_m = _inproj(hm, norm_g, mix_w_in, conv_w, no_hist, layer, e, N_META, False)
            q, k, vt, yc, _ = _inproj(h, norm_g, mix_w_in, conv_w, tail_m, layer, e, seq, True)
            ya = _attention(q, k, vt, km, vm.T, bkt_comb, bkt_meta, rel_bias, attn_sinks, e, batch, seq)
            yam = _attention_meta(qm, km, vm, bkt_self, rel_bias, attn_sinks, e) if not last else ycm
            h, hm = _outproj_ffn(h, ya, yc, hm, yam, ycm, mix_w_out, norm_g, ffn_w1, ffn_w2, layer, e,
                                 not last)
        else:
            h, hm = _pool_ffn(h, hm, norm_g, pool_w, pool_scale, ffn_w1, ffn_w2, layer, layer // 2, seq,
                              not last)
    return h.reshape(batch, seq, D_MODEL)
```

```python
import functools

import jax
import jax.numpy as jnp
import numpy as np
from jax.experimental import pallas as pl
from jax.experimental.pallas import tpu as pltpu

D_MODEL = 1024
N_META = 16
A_HEADS = 8
A_KV_HEADS = 2
A_GROUP = A_HEADS // A_KV_HEADS
HEAD_DIM = 64
WINDOW = 128
BLOCK = 128
A_WIDTH = A_HEADS * HEAD_DIM
KV_WIDTH = A_KV_HEADS * HEAD_DIM
B_WIDTH = D_MODEL // 2
CONV_WIDTH = 3
IN_WIDTH = A_WIDTH + 2 * KV_WIDTH + 3 * B_WIDTH
POOL_SIZES = (2, 4, 8, 16)
N_POOL_GROUPS = 4
POOL_GROUP_DIM = D_MODEL // N_POOL_GROUPS
REL_BUCKETS = 32
REL_MAX_DIST = 128
D_FF = 2816
HALF_STEP = 0.5
RMS_EPS = 1e-6

VMEM_LIMIT_BYTES_V7X = 56 * 1024 * 1024
ROW_TILE = 2048
ROW_SUB = 1024
FFN_ROW_TILE = 1024
OUTPROJ_FFN_ROW_TILE = 512
FFN_SUB_ROWS = 512
FF_CHUNK = 256
N_FF_CHUNKS = D_FF // FF_CHUNK
N_WSLOTS = 2
POOL_HALO = 16
POOL_PAD = 8
CONV_HALO = 8
ATTN_QBLOCKS = 16
GROUP_LANES = A_GROUP * BLOCK

NEG_INF = float("-inf")
_NT = (((1,), (1,)), ((), ()))

assert WINDOW == BLOCK and N_META == POOL_HALO
assert REL_MAX_DIST % (REL_BUCKETS // 2) == 0


def _rms(x, g):
    return x * jax.lax.rsqrt(jnp.mean(x * x, axis=-1, keepdims=True) + RMS_EPS) * g


def _const_spec(shape):
    zeros = (0,) * len(shape)
    return pl.BlockSpec(shape, lambda *_: zeros, pipeline_mode=pl.Buffered(1))


def _slab_spec(arr, *lead):
    rest = arr.shape[len(lead):]
    idx = tuple(lead) + (0,) * len(rest)
    return pl.BlockSpec((None,) * len(lead) + rest, lambda *_: idx, pipeline_mode=pl.Buffered(1))


def _mm(a, w):
    return jax.lax.dot_general(a, w, (((1,), (0,)), ((), ())), preferred_element_type=jnp.float32)


def _params(n_axes):
    return pltpu.CompilerParams(dimension_semantics=("arbitrary",) * n_axes,
                                vmem_limit_bytes=VMEM_LIMIT_BYTES_V7X)


def _chunk_loader(w1_src, w2_src, w1, w2, stage1, stage2, sem):
    def copies(c):
        slot = c % N_WSLOTS
        rows = pl.ds(c * FF_CHUNK, FF_CHUNK)
        return [pltpu.make_async_copy(w1_src.at[:, pl.ds(c * FF_CHUNK, FF_CHUNK)], stage1.at[slot, 0],
                                      sem.at[slot, 0]),
                pltpu.make_async_copy(w1_src.at[:, pl.ds(D_FF + c * FF_CHUNK, FF_CHUNK)], stage1.at[slot, 1],
                                      sem.at[slot, 1]),
                pltpu.make_async_copy(w2_src.at[rows, :], stage2.at[slot], sem.at[slot, 2])]

    def start(c):
        for copy in copies(c):
            copy.start()

    def before_chunk(c):
        if c == 0:
            for k in range(min(N_WSLOTS, N_FF_CHUNKS)):
                start(k)
        for copy in copies(c):
            copy.wait()
        slot = c % N_WSLOTS
        cols = slice(c * FF_CHUNK, (c + 1) * FF_CHUNK)
        w1[:, cols] = stage1[slot, 0].astype(w1.dtype)
        w1[:, D_FF + c * FF_CHUNK:D_FF + (c + 1) * FF_CHUNK] = stage1[slot, 1].astype(w1.dtype)
        w2[cols, :] = stage2[slot].astype(w2.dtype)
        if c + N_WSLOTS < N_FF_CHUNKS:
            start(c + N_WSLOTS)

    return before_chunk


def _swiglu(xn, w1, w2, side=(), before_chunk=None):
    side = list(side)
    acts = []
    for c in range(N_FF_CHUNKS):
        if before_chunk is not None:
            before_chunk(c)
        cols = slice(c * FF_CHUNK, (c + 1) * FF_CHUNK)
        gate = _mm(xn, w1[:, cols])
        up = _mm(xn, w1[:, D_FF + c * FF_CHUNK:D_FF + (c + 1) * FF_CHUNK])
        acts.append((gate * (1.0 / (1.0 + jnp.exp(-gate))) * up).astype(jnp.bfloat16))
        if side:
            side.pop(0)()
    for thunk in side:
        thunk()
    return _mm(jnp.concatenate(acts, axis=1), w2[...])


def _ffn_rows(x, g_ref, pre, post, w1, w2):
    xn = _rms(x, g_ref[pre:pre + 1, :]).astype(jnp.bfloat16)
    return x + HALF_STEP * _rms(_swiglu(xn, w1, w2), g_ref[post:post + 1, :])


def _ffn_scratch():
    return [pltpu.VMEM((D_MODEL, 2 * D_FF), jnp.bfloat16),
            pltpu.VMEM((D_FF, D_MODEL), jnp.bfloat16),
            pltpu.VMEM((N_WSLOTS, 2, D_MODEL, FF_CHUNK), jnp.float32),
            pltpu.VMEM((N_WSLOTS, FF_CHUNK, D_MODEL), jnp.float32),
            pltpu.SemaphoreType.DMA((N_WSLOTS, 3))]


def _sub_blocks(n_rows):
    return [slice(r, r + FFN_SUB_ROWS) for r in range(0, n_rows, FFN_SUB_ROWS)]


def _ffn_tile(read_x, o_ref, g_ref, pre, post, w1, w2, fronts=None, before_chunk=None):
    subs = _sub_blocks(o_ref.shape[0])
    fronts = fronts or [[] for _ in subs]
    prenorm = lambda k: _rms(read_x(subs[k]), g_ref[pre:pre + 1, :]).astype(jnp.bfloat16)
    xns, accs = {}, {}

    def finish(k):
        o_ref[subs[k], :] = read_x(subs[k]) + HALF_STEP * _rms(accs[k], g_ref[post:post + 1, :])

    for thunk in fronts[0]:
        thunk()
    xns[0] = prenorm(0)
    for k in range(len(subs)):
        side = []
        if k > 0:
            side.append(functools.partial(finish, k - 1))
        if k + 1 < len(subs):
            side += fronts[k + 1]
            side.append(lambda k=k: xns.__setitem__(k + 1, prenorm(k + 1)))
        accs[k] = _swiglu(xns[k], w1, w2, side, before_chunk if k == 0 else None)
    finish(len(subs) - 1)


def _ffn_steps(read_x, o_ref, g_ref, pre, post, w1_src, w2_src, scr, make_fronts, first_step_tail):
    w1, w2 = scr[:2]

    @pl.when(pl.program_id(0) == 0)
    def _():
        _ffn_tile(read_x, o_ref, g_ref, pre, post, w1, w2, make_fronts(),
                  _chunk_loader(w1_src, w2_src, *scr[:5]))
        first_step_tail()

    @pl.when(pl.program_id(0) > 0)
    def _():
        _ffn_tile(read_x, o_ref, g_ref, pre, post, w1, w2, make_fronts())


def _ffn_kernel(layer, half, with_meta, h_ref, hm_ref, g_ref, w1_hbm, w2_hbm, o_ref, om_ref, *scr):
    pre, post = 4 * half, 4 * half + 1
    w1, w2 = scr[:2]

    def meta_rows():
        if with_meta:
            om_ref[...] = _ffn_rows(hm_ref[...], g_ref, pre, post, w1, w2)
        else:
            om_ref[...] = hm_ref[...]

    _ffn_steps(lambda rows: h_ref[rows, :], o_ref, g_ref, pre, post, w1_hbm.at[layer, half],
               w2_hbm.at[layer, half], scr, lambda: None, meta_rows)


def _ffn(h, hm, norm_g, ffn_w1, ffn_w2, layer, half, with_meta):
    n = h.shape[0]
    tm = FFN_ROW_TILE
    row = pl.BlockSpec((tm, D_MODEL), lambda i: (i, 0))
    any_spec = pl.BlockSpec(memory_space=pl.ANY)
    return pl.pallas_call(
        functools.partial(_ffn_kernel, layer, half, with_meta),
        grid=(n // tm,),
        in_specs=[row, _const_spec(hm.shape), _slab_spec(norm_g, layer), any_spec, any_spec],
        out_specs=[row, pl.BlockSpec(hm.shape, lambda i: (0, 0))],
        out_shape=[jax.ShapeDtypeStruct(h.shape, h.dtype), jax.ShapeDtypeStruct(hm.shape, hm.dtype)],
        scratch_shapes=_ffn_scratch(),
        compiler_params=_params(1),
        name="ffn",
    )(h, hm, norm_g, ffn_w1, ffn_w2)


def _inproj_rows(x, r0, g_ref, win_ref, convw_ref, uc_scr):
    n = x.shape[0]
    u = _rms(x, g_ref[2:3, :]).astype(jnp.bfloat16)
    z = _mm(u, win_ref[...])
    q = (z[:, :A_WIDTH] * (HEAD_DIM ** -0.5)).astype(jnp.bfloat16)
    k = z[:, A_WIDTH:A_WIDTH + KV_WIDTH].astype(jnp.bfloat16)
    v = z[:, A_WIDTH + KV_WIDTH:A_WIDTH + 2 * KV_WIDTH]
    o = A_WIDTH + 2 * KV_WIDTH
    b_gate = z[:, o:o + B_WIDTH]
    c_gate = z[:, o + B_WIDTH:o + 2 * B_WIDTH]
    x_in = z[:, o + 2 * B_WIDTH:o + 3 * B_WIDTH]
    uc = c_gate * x_in
    uc_scr[CONV_HALO + r0:CONV_HALO + r0 + n, :] = uc
    y = convw_ref[0:1, :] * uc
    for j in range(1, CONV_WIDTH):
        y = y + convw_ref[j:j + 1, :] * uc_scr[CONV_HALO + r0 - j:CONV_HALO + r0 - j + n, :]
    return q, k, v, (b_gate * y).astype(jnp.bfloat16)


def _inproj_kernel(tiles_per_seq, h_ref, hm_ref, g_ref, win_ref, convw_ref,
                   q_ref, k_ref, vt_ref, yc_ref, qm_ref, km_ref, vm_ref, ycm_ref, uc_scr, tail_scr):
    i = pl.program_id(0)
    tm = h_ref.shape[0]

    @pl.when(i == 0)
    def _():
        uc_scr[0:CONV_HALO, :] = jnp.zeros((CONV_HALO, B_WIDTH), jnp.float32)
        q, k, v, yc = _inproj_rows(hm_ref[...], 0, g_ref, win_ref, convw_ref, uc_scr)
        qm_ref[...] = q
        km_ref[...] = k
        vm_ref[...] = v.astype(jnp.bfloat16)
        ycm_ref[...] = yc
        tail_scr[...] = uc_scr[N_META:N_META + CONV_HALO, :]

    @pl.when(i % tiles_per_seq == 0)
    def _():
        uc_scr[0:CONV_HALO, :] = tail_scr[...]

    sub = min(tm, ROW_SUB)
    for r in range(0, tm, sub):
        rows = slice(r, r + sub)
        q, k, v, yc = _inproj_rows(h_ref[rows, :], r, g_ref, win_ref, convw_ref, uc_scr)
        q_ref[rows, :] = q
        k_ref[rows, :] = k
        vt_ref[:, rows] = v.T.astype(jnp.bfloat16)
        yc_ref[rows, :] = yc
    uc_scr[0:CONV_HALO, :] = uc_scr[tm:tm + CONV_HALO, :]


def _inproj(h, hm, norm_g, mix_w_in, conv_w, layer, e, seq):
    n = h.shape[0]
    tm = ROW_TILE
    row = lambda w: pl.BlockSpec((tm, w), lambda i: (i, 0))
    meta = lambda w: pl.BlockSpec((N_META, w), lambda i: (0, 0))
    bf = lambda *shape: jax.ShapeDtypeStruct(shape, jnp.bfloat16)
    return pl.pallas_call(
        functools.partial(_inproj_kernel, seq // tm),
        grid=(n // tm,),
        in_specs=[row(D_MODEL), _const_spec(hm.shape), _slab_spec(norm_g, layer), _slab_spec(mix_w_in, e),
                  _slab_spec(conv_w, e)],
        out_specs=[row(A_WIDTH), row(KV_WIDTH), pl.BlockSpec((KV_WIDTH, tm), lambda i: (0, i)), row(B_WIDTH),
                   meta(A_WIDTH), meta(KV_WIDTH), meta(KV_WIDTH), meta(B_WIDTH)],
        out_shape=[bf(n, A_WIDTH), bf(n, KV_WIDTH), bf(KV_WIDTH, n), bf(n, B_WIDTH),
                   bf(N_META, A_WIDTH), bf(N_META, KV_WIDTH), bf(N_META, KV_WIDTH), bf(N_META, B_WIDTH)],
        scratch_shapes=[pltpu.VMEM((CONV_HALO + tm, B_WIDTH), jnp.float32),
                        pltpu.VMEM((CONV_HALO, B_WIDTH), jnp.float32)],
        compiler_params=_params(1),
        name="inproj_conv",
    )(h, hm, norm_g, mix_w_in, conv_w)


def _bias_from_buckets(bucket, rel_ref, head):
    body = lambda b, acc: jnp.where(bucket == b, rel_ref[b, head], acc)
    acc = jax.lax.fori_loop(0, REL_BUCKETS, body, jnp.zeros(bucket.shape, jnp.float32))
    return jnp.where(bucket >= 0, acc, NEG_INF)


def _attend_meta_rows(e, q, k, v, bkt, rel_ref, sink_ref):
    outs = []
    for h in range(A_HEADS):
        dims = slice((h // A_GROUP) * HEAD_DIM, (h // A_GROUP + 1) * HEAD_DIM)
        s = jax.lax.dot_general(q[:, h * HEAD_DIM:(h + 1) * HEAD_DIM], k[:, dims], _NT,
                                preferred_element_type=jnp.float32)
        s = s + _bias_from_buckets(bkt, rel_ref, h)
        sink = sink_ref[e, h]
        m = jnp.maximum(jnp.max(s, axis=-1, keepdims=True), sink)
        p = jnp.exp(s - m)
        den = jnp.sum(p, axis=-1, keepdims=True) + jnp.exp(sink - m)
        o = jnp.dot(p.astype(jnp.bfloat16), v[:, dims], preferred_element_type=jnp.float32)
        outs.append(o * (1.0 / den))
    return jnp.concatenate(outs, axis=-1)


def _attn_kernel(e, with_meta, q_ref, kp_ref, kc_ref, vtp_ref, vtc_ref, km_ref, vtm_ref, qm_ref, vm_ref,
                 bktc_ref, bktm_ref, bkts_ref, rel_ref, sink_ref, o_ref, om_ref,
                 bias_comb_scr, bias_meta_scr, sink_scr, st_scr, p_scr, rden_scr, yt_scr):
    j = pl.program_id(1)
    key_idx = jax.lax.broadcasted_iota(jnp.int32, (BLOCK, GROUP_LANES), 0)
    qry_idx = jax.lax.broadcasted_iota(jnp.int32, (BLOCK, GROUP_LANES), 1) % BLOCK
    from_prev = key_idx > qry_idx

    @pl.when(jnp.logical_and(pl.program_id(0) == 0, j == 0))
    def _():
        bktc = bktc_ref[...]
        bktm = bktm_ref[...]
        for h in range(A_HEADS):
            g, i = divmod(h, A_GROUP)
            lanes = slice(i * BLOCK, (i + 1) * BLOCK)
            bias = _bias_from_buckets(bktc, rel_ref, h)
            bias_comb_scr[0, g, :, lanes] = bias
            bias_comb_scr[1, g, :, lanes] = jnp.where(from_prev[:, :BLOCK], NEG_INF, bias)
            bias_meta_scr[:, h * BLOCK:(h + 1) * BLOCK] = _bias_from_buckets(bktm, rel_ref, h)
            sink_scr[g, :, lanes] = jnp.full((8, BLOCK), sink_ref[e, h], jnp.float32)
        if with_meta:
            ym = _attend_meta_rows(e, qm_ref[...], km_ref[...], vm_ref[...], bkts_ref[...], rel_ref, sink_ref)
            om_ref[...] = ym.astype(om_ref.dtype)
        else:
            om_ref[...] = jnp.zeros(om_ref.shape, om_ref.dtype)

    kcat =jnp.concatenate([kp_ref[...], kc_ref[...]], axis=0)
    vtcat = jnp.concatenate([vtp_ref[...], vtc_ref[...]], axis=1)
    km = km_ref[...]
    vtm = vtm_ref[...]
    chains = [(qb, g) for qb in range(ATTN_QBLOCKS) for g in range(A_KV_HEADS)]
    for ci, (qb, g) in enumerate(chains):
        dims = slice(g * HEAD_DIM, (g + 1) * HEAD_DIM)
        q = q_ref[qb * BLOCK:(qb + 1) * BLOCK, :]
        q_stack = jnp.concatenate(
            [q[:, (g * A_GROUP + i) * HEAD_DIM:(g * A_GROUP + i + 1) * HEAD_DIM]
             for i in range(A_GROUP)], axis=0)
        k_all = jnp.concatenate([kcat[qb * BLOCK:(qb + 2) * BLOCK, dims], km[:, dims]], axis=0)
        st_scr[ci] = jax.lax.dot_general(k_all, q_stack, _NT, preferred_element_type=jnp.float32)
    for ci, (qb, g) in enumerate(chains):
        variant = jnp.where(j == 0, 1, 0) if qb == 0 else 0
        meta_row = pl.multiple_of((j * ATTN_QBLOCKS + qb) * N_META, N_META)
        comb = (jnp.where(from_prev, st_scr[ci, 0:BLOCK, :], st_scr[ci, BLOCK:2 * BLOCK, :])
                + bias_comb_scr[variant, g])
        meta = st_scr[ci, 2 * BLOCK:, :] + bias_meta_scr[pl.ds(meta_row, N_META),
                                                         g * GROUP_LANES:(g + 1) * GROUP_LANES]
        sink = sink_scr[g, 0:1, :]
        m = jnp.maximum(jnp.maximum(jnp.max(comb, axis=0, keepdims=True),
                                    jnp.max(meta, axis=0, keepdims=True)), sink)
        pc = jnp.exp(comb - m)
        pm = jnp.exp(meta - m)
        den = (jnp.sum(pc, axis=0, keepdims=True) + jnp.sum(pm, axis=0, keepdims=True)
               + jnp.exp(sink - m))
        p_scr[ci, 0:BLOCK, :] = jnp.where(from_prev, pc, 0.0).astype(jnp.bfloat16)
        p_scr[ci, BLOCK:2 * BLOCK, :] = jnp.where(from_prev, 0.0, pc).astype(jnp.bfloat16)
        p_scr[ci, 2 * BLOCK:, :] = pm.astype(jnp.bfloat16)
        rden_scr[ci] = jnp.broadcast_to(1.0 / den, (8, GROUP_LANES))
    for ci, (qb, g) in enumerate(chains):
        dims = slice(g * HEAD_DIM, (g + 1) * HEAD_DIM)
        ot = jnp.dot(vtcat[dims, qb * BLOCK:(qb + 2) * BLOCK], p_scr[ci, 0:2 * BLOCK, :],
                     preferred_element_type=jnp.float32)
        ot = ot + jnp.dot(vtm[dims, :], p_scr[ci, 2 * BLOCK:, :], preferred_element_type=jnp.float32)
        ot = ot * rden_scr[ci, 0:1, :]
        for i in range(A_GROUP):
            h = g * A_GROUP + i
            yt_scr[qb, h * HEAD_DIM:(h + 1) * HEAD_DIM, :] = ot[:, i * BLOCK:(i + 1) * BLOCK]
    for qb in range(ATTN_QBLOCKS):
        o_ref[qb * BLOCK:(qb + 1) * BLOCK, :] = yt_scr[qb].T.astype(o_ref.dtype)


def _attention(q, k, vt, q_meta, k_meta, v_meta, bkt_comb, bkt_meta, bkt_self, rel_bias, sinks, e, batch, seq,
               with_meta):
    vt_meta = v_meta.T
    nb = seq // BLOCK
    qb = ATTN_QBLOCKS
    tiles = nb // qb
    cur_rows = lambda w: pl.BlockSpec((qb * BLOCK, w), lambda b, j: (b * tiles + j, 0))
    prev_blk = lambda b, j: b * nb + jnp.maximum(j * qb - 1, 0)
    smem = pl.BlockSpec(memory_space=pltpu.SMEM)
    return pl.pallas_call(
        functools.partial(_attn_kernel, e, with_meta),
        grid=(batch, tiles),
        in_specs=[cur_rows(A_WIDTH),
                  pl.BlockSpec((BLOCK, KV_WIDTH), lambda b, j: (prev_blk(b, j), 0)),
                  cur_rows(KV_WIDTH),
                  pl.BlockSpec((KV_WIDTH, BLOCK), lambda b, j: (0, prev_blk(b, j))),
                  pl.BlockSpec((KV_WIDTH, qb * BLOCK), lambda b, j: (0, b * tiles + j)),
                  _const_spec(k_meta.shape), _const_spec(vt_meta.shape), _const_spec(q_meta.shape),
                  _const_spec(v_meta.shape), _const_spec(bkt_comb.shape), _const_spec(bkt_meta.shape),
                  _const_spec(bkt_self.shape), smem, smem],
        out_specs=[cur_rows(A_WIDTH), pl.BlockSpec(q_meta.shape, lambda b, j: (0, 0))],
        out_shape=[jax.ShapeDtypeStruct(q.shape, jnp.bfloat16),
                   jax.ShapeDtypeStruct(q_meta.shape, jnp.bfloat16)],
        scratch_shapes=[pltpu.VMEM((2, A_KV_HEADS, BLOCK, GROUP_LANES), jnp.float32),
                        pltpu.VMEM((nb * N_META, A_HEADS * BLOCK), jnp.float32),
                        pltpu.VMEM((A_KV_HEADS, 8, GROUP_LANES), jnp.float32),
                        pltpu.VMEM((qb * A_KV_HEADS, 2 * BLOCK + N_META, GROUP_LANES), jnp.float32),
                        pltpu.VMEM((qb * A_KV_HEADS, 2 * BLOCK + N_META, GROUP_LANES), jnp.bfloat16),
                        pltpu.VMEM((qb * A_KV_HEADS, 8, GROUP_LANES), jnp.float32),
                        pltpu.VMEM((qb, A_WIDTH, BLOCK), jnp.float32)],
        compiler_params=_params(2),
        name="swa_attention",
    )(q, k, k, vt, vt, k_meta, vt_meta, q_meta, v_meta, bkt_comb, bkt_meta, bkt_self, rel_bias, sinks)


def _t5_bucket(n):
    max_exact = REL_BUCKETS // 2
    span = REL_BUCKETS - max_exact
    if n < max_exact:
        return n
    k = 0
    while (REL_MAX_DIST // max_exact) ** (k + 1) * max_exact ** span <= n ** span:
        k += 1
    return min(max_exact + k, REL_BUCKETS - 1)


def _bucket_tables(seq):
    by_dist = np.array([_t5_bucket(n) for n in range(N_META + seq)], np.int32)
    bucket = lambda dist: np.where(dist >= 0, by_dist[np.maximum(dist, 0)], -1).astype(np.int32)
    idx = np.arange(BLOCK)
    comb = bucket((idx[None, :] - idx[:, None]) % BLOCK)
    m = np.arange(N_META)
    pos = N_META + np.arange(seq).reshape(seq // BLOCK, 1, BLOCK)
    meta = bucket(pos - m[None, :, None]).reshape(seq // BLOCK * N_META, BLOCK)
    self_ = bucket(m[:, None] - m[None, :])
    return jnp.asarray(comb), jnp.asarray(meta), jnp.asarray(self_)


def _outproj_rows(x, ya, yc, wo_ref, g_ref):
    mix = _mm(ya, wo_ref[0:A_WIDTH, :]) + _mm(yc, wo_ref[A_WIDTH:, :])
    return x + _rms(mix, g_ref[3:4, :])


def _outproj_ffn_kernel(layer, with_meta, h_ref, ya_ref, yc_ref, hm_ref, yam_ref, ycm_ref, g_ref, wo_ref,
                        w1_hbm, w2_hbm, o_ref, om_ref, *scr):
    w1, w2 = scr[:2]

    def meta_rows():
        if with_meta:
            xm = _outproj_rows(hm_ref[...], yam_ref[...], ycm_ref[...], wo_ref, g_ref)
            om_ref[...] = _ffn_rows(xm, g_ref, 4, 5, w1, w2)
        else:
            om_ref[...] = hm_ref[...]

    def front(rows):
        o_ref[rows, :] = _outproj_rows(h_ref[rows, :], ya_ref[rows, :], yc_ref[rows, :], wo_ref, g_ref)

    make_fronts = lambda: [[functools.partial(front, rows)] for rows in _sub_blocks(o_ref.shape[0])]
    _ffn_steps(lambda rows: o_ref[rows, :], o_ref, g_ref, 4, 5, w1_hbm.at[layer, 1], w2_hbm.at[layer, 1],
               scr, make_fronts, meta_rows)


def _outproj_ffn(h, ya, yc, hm, yam, ycm, mix_w_out, norm_g, ffn_w1, ffn_w2, layer, e, with_meta):
    n = h.shape[0]
    tm = OUTPROJ_FFN_ROW_TILE
    row = lambda w: pl.BlockSpec((tm, w), lambda i: (i, 0))
    any_spec = pl.BlockSpec(memory_space=pl.ANY)
    return pl.pallas_call(
        functools.partial(_outproj_ffn_kernel, layer, with_meta),
        grid=(n // tm,),
        in_specs=[row(D_MODEL), row(A_WIDTH), row(B_WIDTH), _const_spec(hm.shape), _const_spec(yam.shape),
                  _const_spec(ycm.shape), _slab_spec(norm_g, layer), _slab_spec(mix_w_out, e),
                  any_spec, any_spec],
        out_specs=[row(D_MODEL), pl.BlockSpec(hm.shape, lambda i: (0, 0))],
        out_shape=[jax.ShapeDtypeStruct(h.shape, h.dtype), jax.ShapeDtypeStruct(hm.shape, hm.dtype)],
        scratch_shapes=_ffn_scratch(),
        compiler_params=_params(1),
        name="outproj_ffn",
    )(h, ya, yc, hm, yam, ycm, norm_g, mix_w_out, ffn_w1, ffn_w2)


def _pool_fronts(x_ref, hist, t0, g_ref, pw_ref, ps_ref, u_scr, lvl_scr, out_ref):
    tm = x_ref.shape[0]
    gpre = g_ref[2:3, :]
    top = POOL_PAD + POOL_HALO
    sub = min(tm, FFN_SUB_ROWS)

    def prenorm(r):
        if r == 0:
            zero_pad = jnp.zeros((POOL_PAD, D_MODEL), jnp.float32)
            u_scr[0:POOL_PAD, :] = zero_pad
            u_scr[POOL_PAD:top, :] = _rms(hist, gpre)
            for slot in range(2):
                lvl_scr[slot, 0:POOL_PAD, :] = zero_pad[:, :POOL_GROUP_DIM]
        u_scr[top + r:top + r + sub, :] = _rms(x_ref[r:r + sub, :], gpre)

    def group(r, gi):
        w = POOL_SIZES[gi]
        lo, hi = POOL_PAD + r, top + r + sub
        cols = slice(gi * POOL_GROUP_DIM, (gi + 1) * POOL_GROUP_DIM)
        src, src_cols, d = u_scr, cols, 1
        while d < w:
            s = src[lo:hi, src_cols] + src[lo - d:hi - d, src_cols]
            d *= 2
            if d < w:
                dst = lvl_scr.at[(d.bit_length() - 1) % 2]
                dst[lo:hi, :] = s
                src, src_cols = dst, slice(None)
        ug = u_scr[top + r:top + r + sub, cols]
        t = t0 + r + jax.lax.broadcasted_iota(jnp.int32, (sub, 1), 0)
        cnt = jnp.minimum(t + 1, w).astype(jnp.float32)
        diff = (s[POOL_HALO:, :] / cnt - ug).astype(jnp.bfloat16)
        out_ref[r:r + sub, cols] = _mm(diff, pw_ref[gi]) * ps_ref[:, cols]

    def finish(r):
        rows = slice(r, r + sub)
        out_ref[rows, :] = x_ref[rows, :] + _rms(out_ref[rows, :], g_ref[3:4, :])

    return [[functools.partial(prenorm, r)]
            + [functools.partial(group, r, gi) for gi in range(N_POOL_GROUPS)]
            + [functools.partial(finish, r)] for r in range(0, tm, sub)]


def _pool_ffn_kernel(layer, tiles_per_seq, with_meta, h_ref, prev_ref, hm_ref, g_ref, pw_ref, ps_ref,
                     w1_hbm, w2_hbm, o_ref, om_ref, *scr):
    w1, w2 = scr[:2]
    u_scr, lvl_scr = scr[-2:]

    def meta_rows():
        if with_meta:
            no_hist = jnp.zeros((POOL_HALO, D_MODEL), jnp.float32)
            for thunk in _pool_fronts(hm_ref, no_hist, 0, g_ref, pw_ref, ps_ref, u_scr, lvl_scr, om_ref)[0]:
                thunk()
            om_ref[...] = _ffn_rows(om_ref[...], g_ref, 4, 5, w1, w2)
        else:
            om_ref[...] = hm_ref[...]

    def make_fronts():
        j = pl.program_id(0) % tiles_per_seq
        hist = jnp.where(j == 0, hm_ref[...], prev_ref[...])
        return _pool_fronts(h_ref, hist, N_META + j * h_ref.shape[0], g_ref, pw_ref, ps_ref, u_scr, lvl_scr,
                            o_ref)

    _ffn_steps(lambda rows: o_ref[rows, :], o_ref, g_ref, 4, 5, w1_hbm.at[layer, 1], w2_hbm.at[layer, 1],
               scr, make_fronts, meta_rows)


def _pool_ffn(h, hm, norm_g, pool_w, pool_scale, ffn_w1, ffn_w2, layer, o, seq, with_meta):
    n = h.shape[0]
    tm = FFN_ROW_TILE
    halo_blocks = tm // POOL_HALO
    row = pl.BlockSpec((tm, D_MODEL), lambda i: (i, 0))
    prev = pl.BlockSpec((POOL_HALO, D_MODEL), lambda i: (jnp.maximum(i * halo_blocks - 1, 0), 0))
    any_spec = pl.BlockSpec(memory_space=pl.ANY)
    return pl.pallas_call(
        functools.partial(_pool_ffn_kernel, layer, seq // tm, with_meta),
        grid=(n // tm,),
        in_specs=[row, prev, _const_spec(hm.shape), _slab_spec(norm_g, layer), _slab_spec(pool_w, o),
                  pl.BlockSpec((1, D_MODEL), lambda i: (o, 0), pipeline_mode=pl.Buffered(1)),
                  any_spec, any_spec],
        out_specs=[row, pl.BlockSpec(hm.shape, lambda i: (0, 0))],
        out_shape=[jax.ShapeDtypeStruct(h.shape, h.dtype), jax.ShapeDtypeStruct(hm.shape, hm.dtype)],
        scratch_shapes=_ffn_scratch() + [
            pltpu.VMEM((POOL_PAD + POOL_HALO + tm, D_MODEL), jnp.float32),
            pltpu.VMEM((2, POOL_PAD + POOL_HALO + tm, POOL_GROUP_DIM), jnp.float32)],
        compiler_params=_params(1),
        name="pool_ffn",
    )(h, h, hm, norm_g, pool_w, pool_scale, ffn_w1, ffn_w2)


def kernel(x, meta_tokens, rel_bias, norm_g, ffn_w1, ffn_w2, mix_w_in, mix_w_out, attn_sinks,
           conv_w, pool_w, pool_scale):
    batch, seq, _ = x.shape
    depth = norm_g.shape[0]
    h = x.reshape(batch * seq, D_MODEL)
    hm = meta_tokens.astype(x.dtype)
    bkt_comb, bkt_meta, bkt_self = _bucket_tables(seq)
    for layer in range(depth):
        last = layer == depth - 1
        h, hm = _ffn(h, hm, norm_g, ffn_w1, ffn_w2, layer, 0, True)
        if layer % 2 == 0:
            e = layer // 2
            q, k, vt, yc, qm, km, vm, ycm = _inproj(h, hm, norm_g, mix_w_in, conv_w, layer, e, seq)
            ya, yam = _attention(q, k, vt, qm, km, vm, bkt_comb, bkt_meta, bkt_self, rel_bias, attn_sinks, e,
                                 batch, seq, not last)
            h, hm = _outproj_ffn(h, ya, yc, hm, yam, ycm, mix_w_out, norm_g, ffn_w1, ffn_w2, layer, e,
                                 not last)
        else:
            h, hm = _pool_ffn(h, hm, norm_g, pool_w, pool_scale, ffn_w1, ffn_w2, layer, layer // 2, seq,
                              not last)
    return h.reshape(batch, seq, D_MODEL)
```

```python
import functools

import jax
import jax.numpy as jnp
import numpy as np
from jax.experimental import pallas as pl
from jax.experimental.pallas import tpu as pltpu

D_MODEL = 1024
N_META = 16
A_HEADS = 8
A_KV_HEADS = 2
A_GROUP = A_HEADS // A_KV_HEADS
HEAD_DIM = 64
WINDOW = 128
BLOCK = 128
A_WIDTH = A_HEADS * HEAD_DIM
KV_WIDTH = A_KV_HEADS * HEAD_DIM
B_WIDTH = D_MODEL // 2
CONV_WIDTH = 3
IN_WIDTH = A_WIDTH + 2 * KV_WIDTH + 3 * B_WIDTH
POOL_SIZES = (2, 4, 8, 16)
N_POOL_GROUPS = 4
POOL_GROUP_DIM = D_MODEL // N_POOL_GROUPS
REL_BUCKETS = 32
REL_MAX_DIST = 128
D_FF = 2816
HALF_STEP = 0.5
RMS_EPS = 1e-6

VMEM_LIMIT_BYTES_V7X = 60 * 1024 * 1024
ROW_TILE = 2048
ROW_SUB = 1024
FFN_ROW_TILE = 1024
OUTPROJ_FFN_ROW_TILE = 512
FFN_SUB_ROWS = 512
FF_CHUNK = 256
N_FF_CHUNKS = D_FF // FF_CHUNK
N_WSLOTS = 3
POOL_HALO = 16
POOL_PAD = 8
CONV_HALO = 8
ATTN_QBLOCKS = 16
GROUP_LANES = A_GROUP * BLOCK

NEG_INF = float("-inf")
_NT = (((1,), (1,)), ((), ()))

assert WINDOW == BLOCK and N_META == POOL_HALO
assert REL_MAX_DIST % (REL_BUCKETS // 2) == 0


def _rms(x, g):
    return x * jax.lax.rsqrt(jnp.mean(x * x, axis=-1, keepdims=True) + RMS_EPS) * g


def _const_spec(shape):
    zeros = (0,) * len(shape)
    return pl.BlockSpec(shape, lambda *_: zeros, pipeline_mode=pl.Buffered(1))


def _slab_spec(arr, *lead):
    rest = arr.shape[len(lead):]
    idx = tuple(lead) + (0,) * len(rest)
    return pl.BlockSpec((None,) * len(lead) + rest, lambda *_: idx, pipeline_mode=pl.Buffered(1))


def _mm(a, w):
    return jax.lax.dot_general(a, w, (((1,), (0,)), ((), ())), preferred_element_type=jnp.float32)


def _params(n_axes):
    return pltpu.CompilerParams(dimension_semantics=("arbitrary",) * n_axes,
                                vmem_limit_bytes=VMEM_LIMIT_BYTES_V7X)


def _chunk_loader(w1_src, w2_src, w1, w2, stage1, stage2, sem):
    def copies(c):
        slot = c % N_WSLOTS
        rows = pl.ds(c * FF_CHUNK, FF_CHUNK)
        return [pltpu.make_async_copy(w1_src.at[:, pl.ds(c * FF_CHUNK, FF_CHUNK)], stage1.at[slot, 0],
                                      sem.at[slot, 0]),
                pltpu.make_async_copy(w1_src.at[:, pl.ds(D_FF + c * FF_CHUNK, FF_CHUNK)], stage1.at[slot, 1],
                                      sem.at[slot, 1]),
                pltpu.make_async_copy(w2_src.at[rows, :], stage2.at[slot], sem.at[slot, 2])]

    def start(c):
        for copy in copies(c):
            copy.start()

    def before_chunk(c):
        if c == 0:
            for k in range(min(N_WSLOTS, N_FF_CHUNKS)):
                start(k)
        for copy in copies(c):
            copy.wait()
        slot = c % N_WSLOTS
        cols = slice(c * FF_CHUNK, (c + 1) * FF_CHUNK)
        w1[:, cols] = stage1[slot, 0].astype(w1.dtype)
        w1[:, D_FF + c * FF_CHUNK:D_FF + (c + 1) * FF_CHUNK] = stage1[slot, 1].astype(w1.dtype)
        w2[cols, :] = stage2[slot].astype(w2.dtype)
        if c + N_WSLOTS < N_FF_CHUNKS:
            start(c + N_WSLOTS)

    return before_chunk


def _swiglu(xn, w1, w2, side=(), before_chunk=None):
    side = list(side)
    acts = []
    for c in range(N_FF_CHUNKS):
        if before_chunk is not None:
            before_chunk(c)
        cols = slice(c * FF_CHUNK, (c + 1) * FF_CHUNK)
        gate = _mm(xn, w1[:, cols])
        up = _mm(xn, w1[:, D_FF + c * FF_CHUNK:D_FF + (c + 1) * FF_CHUNK])
        acts.append((gate * (1.0 / (1.0 + jnp.exp(-gate))) * up).astype(jnp.bfloat16))
        if side:
            side.pop(0)()
    for thunk in side:
        thunk()
    return _mm(jnp.concatenate(acts, axis=1), w2[...])


def _ffn_scratch():
    return [pltpu.VMEM((D_MODEL, 2 * D_FF), jnp.bfloat16),
            pltpu.VMEM((D_FF, D_MODEL), jnp.bfloat16),
            pltpu.VMEM((N_WSLOTS, 2, D_MODEL, FF_CHUNK), jnp.float32),
            pltpu.VMEM((N_WSLOTS, FF_CHUNK, D_MODEL), jnp.float32),
            pltpu.SemaphoreType.DMA((N_WSLOTS, 3))]


def _sub_blocks(n_rows):
    return [slice(r, r + FFN_SUB_ROWS) for r in range(0, n_rows, FFN_SUB_ROWS)]


def _ffn_tile(read_x, o_ref, g_ref, pre, post, w1, w2, fronts=None, before_chunk=None, rider_ref=None):
    subs = _sub_blocks(o_ref.shape[0])
    fronts = fronts or [[] for _ in subs]
    xns, accs = {}, {}

    def rows_of(k):
        x = read_x(subs[k])
        return x if (k > 0 or rider_ref is None) else jnp.concatenate([x, rider_ref[...]], axis=0)

    prenorm = lambda k: _rms(rows_of(k), g_ref[pre:pre + 1, :]).astype(jnp.bfloat16)

    def finish(k):
        y = rows_of(k) + HALF_STEP * _rms(accs[k], g_ref[post:post + 1, :])
        n = subs[k].stop - subs[k].start
        o_ref[subs[k], :] = y[:n]
        if y.shape[0] > n:
            rider_ref[...] = y[n:]

    for thunk in fronts[0]:
        thunk()
    xns[0] = prenorm(0)
    for k in range(len(subs)):
        side = []
        if k > 0:
            side.append(functools.partial(finish, k - 1))
        if k + 1 < len(subs):
            side += fronts[k + 1]
            side.append(lambda k=k: xns.__setitem__(k + 1, prenorm(k + 1)))
        accs[k] = _swiglu(xns[k], w1, w2, side, before_chunk if k == 0 else None)
    finish(len(subs) - 1)


def _ffn_steps(read_x, o_ref, g_ref, pre, post, w1_src, w2_src, scr, make_fronts, prepare_meta, rider_ref):
    w1, w2 = scr[:2]

    @pl.when(pl.program_id(0) == 0)
    def _():
        prepare_meta()
        _ffn_tile(read_x, o_ref, g_ref, pre, post, w1, w2, make_fronts(),
                  _chunk_loader(w1_src, w2_src, *scr[:5]), rider_ref)

    @pl.when(pl.program_id(0) > 0)
    def _():
        _ffn_tile(read_x, o_ref, g_ref, pre, post, w1, w2, make_fronts())


def _ffn_kernel(layer, half, with_meta, h_ref, hm_ref, g_ref, w1_hbm, w2_hbm, o_ref, om_ref, *scr):
    pre, post = 4 * half, 4 * half + 1

    def prepare_meta():
        om_ref[...] = hm_ref[...]

    _ffn_steps(lambda rows: h_ref[rows, :], o_ref, g_ref, pre, post, w1_hbm.at[layer, half],
               w2_hbm.at[layer, half], scr, lambda: None, prepare_meta, om_ref if with_meta else None)


def _ffn(h, hm, norm_g, ffn_w1, ffn_w2, layer, half, with_meta):
    n = h.shape[0]
    tm = FFN_ROW_TILE
    row = pl.BlockSpec((tm, D_MODEL), lambda i: (i, 0))
    any_spec = pl.BlockSpec(memory_space=pl.ANY)
    return pl.pallas_call(
        functools.partial(_ffn_kernel, layer, half, with_meta),
        grid=(n // tm,),
        in_specs=[row, _const_spec(hm.shape), _slab_spec(norm_g, layer), any_spec, any_spec],
        out_specs=[row, pl.BlockSpec(hm.shape, lambda i: (0, 0))],
        out_shape=[jax.ShapeDtypeStruct(h.shape, h.dtype), jax.ShapeDtypeStruct(hm.shape, hm.dtype)],
        scratch_shapes=_ffn_scratch(),
        compiler_params=_params(1),
        name="ffn",
    )(h, hm, norm_g, ffn_w1, ffn_w2)


def _inproj_rows(x, r0, g_ref, win_ref, convw_ref, uc_scr):
    n = x.shape[0]
    u = _rms(x, g_ref[2:3, :]).astype(jnp.bfloat16)
    z = _mm(u, win_ref[...])
    q = (z[:, :A_WIDTH] * (HEAD_DIM ** -0.5)).astype(jnp.bfloat16)
    k = z[:, A_WIDTH:A_WIDTH + KV_WIDTH].astype(jnp.bfloat16)
    v = z[:, A_WIDTH + KV_WIDTH:A_WIDTH + 2 * KV_WIDTH]
    o = A_WIDTH + 2 * KV_WIDTH
    b_gate = z[:, o:o + B_WIDTH]
    c_gate = z[:, o + B_WIDTH:o + 2 * B_WIDTH]
    x_in = z[:, o + 2 * B_WIDTH:o + 3 * B_WIDTH]
    uc = c_gate * x_in
    uc_scr[CONV_HALO + r0:CONV_HALO + r0 + n, :] = uc
    y = convw_ref[0:1, :] * uc
    for j in range(1, CONV_WIDTH):
        y = y + convw_ref[j:j + 1, :] * uc_scr[CONV_HALO + r0 - j:CONV_HALO + r0 - j + n, :]
    return q, k, v, (b_gate * y).astype(jnp.bfloat16)


def _inproj_kernel(tiles_per_seq, h_ref, hm_ref, g_ref, win_ref, convw_ref,
                   q_ref, k_ref, vt_ref, yc_ref, qm_ref, km_ref, vm_ref, ycm_ref, uc_scr, tail_scr):
    i = pl.program_id(0)
    tm = h_ref.shape[0]

    @pl.when(i == 0)
    def _():
        uc_scr[0:CONV_HALO, :] = jnp.zeros((CONV_HALO, B_WIDTH), jnp.float32)
        q, k, v, yc = _inproj_rows(hm_ref[...], 0, g_ref, win_ref, convw_ref, uc_scr)
        qm_ref[...] = q
        km_ref[...] = k
        vm_ref[...] = v.astype(jnp.bfloat16)
        ycm_ref[...] = yc
        tail_scr[...] = uc_scr[N_META:N_META + CONV_HALO, :]

    @pl.when(i % tiles_per_seq == 0)
    def _():
        uc_scr[0:CONV_HALO, :] = tail_scr[...]

    sub = min(tm, ROW_SUB)
    for r in range(0, tm, sub):
        rows = slice(r, r + sub)
        q, k, v, yc = _inproj_rows(h_ref[rows, :], r, g_ref, win_ref, convw_ref, uc_scr)
        q_ref[rows, :] = q
        k_ref[rows, :] = k
        vt_ref[:, rows] = v.T.astype(jnp.bfloat16)
        yc_ref[rows, :] = yc
    uc_scr[0:CONV_HALO, :] = uc_scr[tm:tm + CONV_HALO, :]


def _inproj(h, hm, norm_g, mix_w_in, conv_w, layer, e, seq):
    n = h.shape[0]
    tm = ROW_TILE
    row = lambda w: pl.BlockSpec((tm, w), lambda i: (i, 0))
    meta = lambda w: pl.BlockSpec((N_META, w), lambda i: (0, 0))
    bf = lambda *shape: jax.ShapeDtypeStruct(shape, jnp.bfloat16)
    return pl.pallas_call(
        functools.partial(_inproj_kernel, seq // tm),
        grid=(n // tm,),
        in_specs=[row(D_MODEL), _const_spec(hm.shape), _slab_spec(norm_g, layer), _slab_spec(mix_w_in, e),
                  _slab_spec(conv_w, e)],
        out_specs=[row(A_WIDTH), row(KV_WIDTH), pl.BlockSpec((KV_WIDTH, tm), lambda i: (0, i)), row(B_WIDTH),
                   meta(A_WIDTH), meta(KV_WIDTH), meta(KV_WIDTH), meta(B_WIDTH)],
        out_shape=[bf(n, A_WIDTH), bf(n, KV_WIDTH), bf(KV_WIDTH, n), bf(n, B_WIDTH),
                   bf(N_META, A_WIDTH), bf(N_META, KV_WIDTH), bf(N_META, KV_WIDTH), bf(N_META, B_WIDTH)],
        scratch_shapes=[pltpu.VMEM((CONV_HALO + tm, B_WIDTH), jnp.float32),
                        pltpu.VMEM((CONV_HALO, B_WIDTH), jnp.float32)],
        compiler_params=_params(1),
        name="inproj_conv",
    )(h, hm, norm_g, mix_w_in, conv_w)


def _bias_from_buckets(bucket, rel_ref, head):
    body = lambda b, acc: jnp.where(bucket == b, rel_ref[b, head], acc)
    acc = jax.lax.fori_loop(0, REL_BUCKETS, body, jnp.zeros(bucket.shape, jnp.float32))
    return jnp.where(bucket >= 0, acc, NEG_INF)


def _attend_meta_rows(e, q, k, v, bkt, rel_ref, sink_ref):
    outs = []
    for h in range(A_HEADS):
        dims = slice((h // A_GROUP) * HEAD_DIM, (h // A_GROUP + 1) * HEAD_DIM)
        s = jax.lax.dot_general(q[:, h * HEAD_DIM:(h + 1) * HEAD_DIM], k[:, dims], _NT,
                                preferred_element_type=jnp.float32)
        s = s + _bias_from_buckets(bkt, rel_ref, h)
        sink = sink_ref[e, h]
        m = jnp.maximum(jnp.max(s, axis=-1, keepdims=True), sink)
        p = jnp.exp(s - m)
        den = jnp.sum(p, axis=-1, keepdims=True) + jnp.exp(sink - m)
        o = jnp.dot(p.astype(jnp.bfloat16), v[:, dims], preferred_element_type=jnp.float32)
        outs.append(o * (1.0 / den))
    return jnp.concatenate(outs, axis=-1)


def _attn_kernel(e, with_meta, q_ref, kp_ref, kc_ref, vtp_ref, vtc_ref, km_ref, vtm_ref, qm_ref, vm_ref,
                 bktc_ref, bktm_ref, bkts_ref, rel_ref, sink_ref, o_ref, om_ref,
                 bias_comb_scr, bias_meta_scr, sink_scr, st_scr, p_scr, rden_scr, yt_scr):
    j = pl.program_id(1)
    key_idx = jax.lax.broadcasted_iota(jnp.int32, (BLOCK, GROUP_LANES), 0)
    qry_idx = jax.lax.broadcasted_iota(jnp.int32, (BLOCK, GROUP_LANES), 1) % BLOCK
    from_prev = key_idx > qry_idx

    @pl.when(jnp.logical_and(pl.program_id(0) == 0, j == 0))
    def _():
        bktc = bktc_ref[...]
        bktm = bktm_ref[...]
        for h in range(A_HEADS):
            g, i = divmod(h, A_GROUP)
            lanes = slice(i * BLOCK, (i + 1) * BLOCK)
            bias = _bias_from_buckets(bktc, rel_ref, h)
            bias_comb_scr[0, g, :, lanes] = bias
            bias_comb_scr[1, g, :, lanes] = jnp.where(from_prev[:, :BLOCK], NEG_INF, bias)
            bias_meta_scr[:, h * BLOCK:(h + 1) * BLOCK] = _bias_from_buckets(bktm, rel_ref, h)
            sink_scr[g, :, lanes] = jnp.full((8, BLOCK), sink_ref[e, h], jnp.float32)
        if with_meta:
            ym = _attend_meta_rows(e, qm_ref[...], km_ref[...], vm_ref[...], bkts_ref[...], rel_ref, sink_ref)
            om_ref[...] = ym.astype(om_ref.dtype)
        else:
            om_ref[...] = jnp.zeros(om_ref.shape, om_ref.dtype)

    kcat =jnp.concatenate([kp_ref[...], kc_ref[...]], axis=0)
    vtcat = jnp.concatenate([vtp_ref[...], vtc_ref[...]], axis=1)
    km = km_ref[...]
    vtm = vtm_ref[...]
    chains = [(qb, g) for qb in range(ATTN_QBLOCKS) for g in range(A_KV_HEADS)]
    for ci, (qb, g) in enumerate(chains):
        dims = slice(g * HEAD_DIM, (g + 1) * HEAD_DIM)
        q = q_ref[qb * BLOCK:(qb + 1) * BLOCK, :]
        q_stack = jnp.concatenate(
            [q[:, (g * A_GROUP + i) * HEAD_DIM:(g * A_GROUP + i + 1) * HEAD_DIM]
             for i in range(A_GROUP)], axis=0)
        k_all = jnp.concatenate([kcat[qb * BLOCK:(qb + 2) * BLOCK, dims], km[:, dims]], axis=0)
        st_scr[ci] = jax.lax.dot_general(k_all, q_stack, _NT, preferred_element_type=jnp.float32)
    for ci, (qb, g) in enumerate(chains):
        variant = jnp.where(j == 0, 1, 0) if qb == 0 else 0
        meta_row = pl.multiple_of((j * ATTN_QBLOCKS + qb) * N_META, N_META)
        comb = (jnp.where(from_prev, st_scr[ci, 0:BLOCK, :], st_scr[ci, BLOCK:2 * BLOCK, :])
                + bias_comb_scr[variant, g])
        meta = st_scr[ci, 2 * BLOCK:, :] + bias_meta_scr[pl.ds(meta_row, N_META),
                                                         g * GROUP_LANES:(g + 1) * GROUP_LANES]
        sink = sink_scr[g, 0:1, :]
        m = jnp.maximum(jnp.maximum(jnp.max(comb, axis=0, keepdims=True),
                                    jnp.max(meta, axis=0, keepdims=True)), sink)
        pc = jnp.exp(comb - m)
        pm = jnp.exp(meta - m)
        den = (jnp.sum(pc, axis=0, keepdims=True) + jnp.sum(pm, axis=0, keepdims=True)
               + jnp.exp(sink - m))
        p_scr[ci, 0:BLOCK, :] = jnp.where(from_prev, pc, 0.0).astype(jnp.bfloat16)
        p_scr[ci, BLOCK:2 * BLOCK, :] = jnp.where(from_prev, 0.0, pc).astype(jnp.bfloat16)
        p_scr[ci, 2 * BLOCK:, :] = pm.astype(jnp.bfloat16)
        rden_scr[ci] = jnp.broadcast_to(1.0 / den, (8, GROUP_LANES))
    for ci, (qb, g) in enumerate(chains):
        dims = slice(g * HEAD_DIM, (g + 1) * HEAD_DIM)
        ot = jnp.dot(vtcat[dims, qb * BLOCK:(qb + 2) * BLOCK], p_scr[ci, 0:2 * BLOCK, :],
                     preferred_element_type=jnp.float32)
        ot = ot + jnp.dot(vtm[dims, :], p_scr[ci, 2 * BLOCK:, :], preferred_element_type=jnp.float32)
        ot = ot * rden_scr[ci, 0:1, :]
        for i in range(A_GROUP):
            h = g * A_GROUP + i
            yt_scr[qb, h * HEAD_DIM:(h + 1) * HEAD_DIM, :] = ot[:, i * BLOCK:(i + 1) * BLOCK]
    for qb in range(ATTN_QBLOCKS):
        o_ref[qb * BLOCK:(qb + 1) * BLOCK, :] = yt_scr[qb].T.astype(o_ref.dtype)


def _attention(q, k, vt, q_meta, k_meta, v_meta, bkt_comb, bkt_meta, bkt_self, rel_bias, sinks, e, batch, seq,
               with_meta):
    vt_meta = v_meta.T
    nb = seq // BLOCK
    qb = ATTN_QBLOCKS
    tiles = nb // qb
    cur_rows = lambda w: pl.BlockSpec((qb * BLOCK, w), lambda b, j: (b * tiles + j, 0))
    prev_blk = lambda b, j: b * nb + jnp.maximum(j * qb - 1, 0)
    smem = pl.BlockSpec(memory_space=pltpu.SMEM)
    return pl.pallas_call(
        functools.partial(_attn_kernel, e, with_meta),
        grid=(batch, tiles),
        in_specs=[cur_rows(A_WIDTH),
                  pl.BlockSpec((BLOCK, KV_WIDTH), lambda b, j: (prev_blk(b, j), 0)),
                  cur_rows(KV_WIDTH),
                  pl.BlockSpec((KV_WIDTH, BLOCK), lambda b, j: (0, prev_blk(b, j))),
                  pl.BlockSpec((KV_WIDTH, qb * BLOCK), lambda b, j: (0, b * tiles + j)),
                  _const_spec(k_meta.shape), _const_spec(vt_meta.shape), _const_spec(q_meta.shape),
                  _const_spec(v_meta.shape), _const_spec(bkt_comb.shape), _const_spec(bkt_meta.shape),
                  _const_spec(bkt_self.shape), smem, smem],
        out_specs=[cur_rows(A_WIDTH), pl.BlockSpec(q_meta.shape, lambda b, j: (0, 0))],
        out_shape=[jax.ShapeDtypeStruct(q.shape, jnp.bfloat16),
                   jax.ShapeDtypeStruct(q_meta.shape, jnp.bfloat16)],
        scratch_shapes=[pltpu.VMEM((2, A_KV_HEADS, BLOCK, GROUP_LANES), jnp.float32),
                        pltpu.VMEM((nb * N_META, A_HEADS * BLOCK), jnp.float32),
                        pltpu.VMEM((A_KV_HEADS, 8, GROUP_LANES), jnp.float32),
                        pltpu.VMEM((qb * A_KV_HEADS, 2 * BLOCK + N_META, GROUP_LANES), jnp.float32),
                        pltpu.VMEM((qb * A_KV_HEADS, 2 * BLOCK + N_META, GROUP_LANES), jnp.bfloat16),
                        pltpu.VMEM((qb * A_KV_HEADS, 8, GROUP_LANES), jnp.float32),
                        pltpu.VMEM((qb, A_WIDTH, BLOCK), jnp.float32)],
        compiler_params=_params(2),
        name="swa_attention",
    )(q, k, k, vt, vt, k_meta, vt_meta, q_meta, v_meta, bkt_comb, bkt_meta, bkt_self, rel_bias, sinks)


def _t5_bucket(n):
    max_exact = REL_BUCKETS // 2
    span = REL_BUCKETS - max_exact
    if n < max_exact:
        return n
    k = 0
    while (REL_MAX_DIST // max_exact) ** (k + 1) * max_exact ** span <= n ** span:
        k += 1
    return min(max_exact + k, REL_BUCKETS - 1)


def _bucket_tables(seq):
    by_dist = np.array([_t5_bucket(n) for n in range(N_META + seq)], np.int32)
    bucket = lambda dist: np.where(dist >= 0, by_dist[np.maximum(dist, 0)], -1).astype(np.int32)
    idx = np.arange(BLOCK)
    comb = bucket((idx[None, :] - idx[:, None]) % BLOCK)
    m = np.arange(N_META)
    pos = N_META + np.arange(seq).reshape(seq // BLOCK, 1, BLOCK)
    meta = bucket(pos - m[None, :, None]).reshape(seq // BLOCK * N_META, BLOCK)
    self_ = bucket(m[:, None] - m[None, :])
    return jnp.asarray(comb), jnp.asarray(meta), jnp.asarray(self_)


def _outproj_rows(x, ya, yc, wo_ref, g_ref):
    mix = _mm(ya, wo_ref[0:A_WIDTH, :]) + _mm(yc, wo_ref[A_WIDTH:, :])
    return x + _rms(mix, g_ref[3:4, :])


def _outproj_ffn_kernel(layer, with_meta, h_ref, ya_ref, yc_ref, hm_ref, yam_ref, ycm_ref, g_ref, wo_ref,
                        w1_hbm, w2_hbm, o_ref, om_ref, *scr):
    def prepare_meta():
        if with_meta:
            om_ref[...] = _outproj_rows(hm_ref[...], yam_ref[...], ycm_ref[...], wo_ref, g_ref)
        else:
            om_ref[...] = hm_ref[...]

    def front(rows):
        o_ref[rows, :] = _outproj_rows(h_ref[rows, :], ya_ref[rows, :], yc_ref[rows, :], wo_ref, g_ref)

    make_fronts = lambda: [[functools.partial(front, rows)] for rows in _sub_blocks(o_ref.shape[0])]
    _ffn_steps(lambda rows: o_ref[rows, :], o_ref, g_ref, 4, 5, w1_hbm.at[layer, 1], w2_hbm.at[layer, 1],
               scr, make_fronts, prepare_meta, om_ref if with_meta else None)


def _outproj_ffn(h, ya, yc, hm, yam, ycm, mix_w_out, norm_g, ffn_w1, ffn_w2, layer, e, with_meta):
    n = h.shape[0]
    tm = OUTPROJ_FFN_ROW_TILE
    row = lambda w: pl.BlockSpec((tm, w), lambda i: (i, 0))
    any_spec = pl.BlockSpec(memory_space=pl.ANY)
    return pl.pallas_call(
        functools.partial(_outproj_ffn_kernel, layer, with_meta),
        grid=(n // tm,),
        in_specs=[row(D_MODEL), row(A_WIDTH), row(B_WIDTH), _const_spec(hm.shape), _const_spec(yam.shape),
                  _const_spec(ycm.shape), _slab_spec(norm_g, layer), _slab_spec(mix_w_out, e),
                  any_spec, any_spec],
        out_specs=[row(D_MODEL), pl.BlockSpec(hm.shape, lambda i: (0, 0))],
        out_shape=[jax.ShapeDtypeStruct(h.shape, h.dtype), jax.ShapeDtypeStruct(hm.shape, hm.dtype)],
        scratch_shapes=_ffn_scratch(),
        compiler_params=_params(1),
        name="outproj_ffn",
    )(h, ya, yc, hm, yam, ycm, norm_g, mix_w_out, ffn_w1, ffn_w2)


def _pool_fronts(x_ref, hist, t0, g_ref, pw_ref, ps_ref, u_scr, lvl_scr, out_ref):
    tm = x_ref.shape[0]
    gpre = g_ref[2:3, :]
    top = POOL_PAD + POOL_HALO
    sub = min(tm, FFN_SUB_ROWS)

    def prenorm(r):
        if r == 0:
            zero_pad = jnp.zeros((POOL_PAD, D_MODEL), jnp.float32)
            u_scr[0:POOL_PAD, :] = zero_pad
            u_scr[POOL_PAD:top, :] = _rms(hist, gpre)
            for slot in range(2):
                lvl_scr[slot, 0:POOL_PAD, :] = zero_pad[:, :POOL_GROUP_DIM]
        u_scr[top + r:top + r + sub, :] = _rms(x_ref[r:r + sub, :], gpre)

    def group(r, gi):
        w = POOL_SIZES[gi]
        lo, hi = POOL_PAD + r, top + r + sub
        cols = slice(gi * POOL_GROUP_DIM, (gi + 1) * POOL_GROUP_DIM)
        src, src_cols, d = u_scr, cols, 1
        while d < w:
            s = src[lo:hi, src_cols] + src[lo - d:hi - d, src_cols]
            d *= 2
            if d < w:
                dst = lvl_scr.at[(d.bit_length() - 1) % 2]
                dst[lo:hi, :] = s
                src, src_cols = dst, slice(None)
        ug = u_scr[top + r:top + r + sub, cols]
        t = t0 + r + jax.lax.broadcasted_iota(jnp.int32, (sub, 1), 0)
        cnt = jnp.minimum(t + 1, w).astype(jnp.float32)
        diff = (s[POOL_HALO:, :] / cnt - ug).astype(jnp.bfloat16)
        out_ref[r:r + sub, cols] = _mm(diff, pw_ref[gi]) * ps_ref[:, cols]

    def finish(r):
        rows = slice(r, r + sub)
        out_ref[rows, :] = x_ref[rows, :] + _rms(out_ref[rows, :], g_ref[3:4, :])

    return [[functools.partial(prenorm, r)]
            + [functools.partial(group, r, gi) for gi in range(N_POOL_GROUPS)]
            + [functools.partial(finish, r)] for r in range(0, tm, sub)]


def _pool_ffn_kernel(layer, tiles_per_seq, with_meta, h_ref, prev_ref, hm_ref, g_ref, pw_ref, ps_ref,
                     w1_hbm, w2_hbm, o_ref, om_ref, *scr):
    u_scr, lvl_scr = scr[-2:]

    def prepare_meta():
        if with_meta:
            no_hist = jnp.zeros((POOL_HALO, D_MODEL), jnp.float32)
            for thunk in _pool_fronts(hm_ref, no_hist, 0, g_ref, pw_ref, ps_ref, u_scr, lvl_scr, om_ref)[0]:
                thunk()
        else:
            om_ref[...] = hm_ref[...]

    def make_fronts():
        j = pl.program_id(0) % tiles_per_seq
        hist = jnp.where(j == 0, hm_ref[...], prev_ref[...])
        return _pool_fronts(h_ref, hist, N_META + j * h_ref.shape[0], g_ref, pw_ref, ps_ref, u_scr, lvl_scr,
                            o_ref)

    _ffn_steps(lambda rows: o_ref[rows, :], o_ref, g_ref, 4, 5, w1_hbm.at[layer, 1], w2_hbm.at[layer, 1],
               scr, make_fronts, prepare_meta, om_ref if with_meta else None)


def _pool_ffn(h, hm, norm_g, pool_w, pool_scale, ffn_w1, ffn_w2, layer, o, seq, with_meta):
    n = h.shape[0]
    tm = FFN_ROW_TILE
    halo_blocks = tm // POOL_HALO
    row = pl.BlockSpec((tm, D_MODEL), lambda i: (i, 0))
    prev = pl.BlockSpec((POOL_HALO, D_MODEL), lambda i: (jnp.maximum(i * halo_blocks - 1, 0), 0))
    any_spec = pl.BlockSpec(memory_space=pl.ANY)
    return pl.pallas_call(
        functools.partial(_pool_ffn_kernel, layer, seq // tm, with_meta),
        grid=(n // tm,),
        in_specs=[row, prev, _const_spec(hm.shape), _slab_spec(norm_g, layer), _slab_spec(pool_w, o),
                  pl.BlockSpec((1, D_MODEL), lambda i: (o, 0), pipeline_mode=pl.Buffered(1)),
                  any_spec, any_spec],
        out_specs=[row, pl.BlockSpec(hm.shape, lambda i: (0, 0))],
        out_shape=[jax.ShapeDtypeStruct(h.shape, h.dtype), jax.ShapeDtypeStruct(hm.shape, hm.dtype)],
        scratch_shapes=_ffn_scratch() + [
            pltpu.VMEM((POOL_PAD + POOL_HALO + tm, D_MODEL), jnp.float32),
            pltpu.VMEM((2, POOL_PAD + POOL_HALO + tm, POOL_GROUP_DIM), jnp.float32)],
        compiler_params=_params(1),
        name="pool_ffn",
    )(h, h, hm, norm_g, pool_w, pool_scale, ffn_w1, ffn_w2)


def kernel(x, meta_tokens, rel_bias, norm_g, ffn_w1, ffn_w2, mix_w_in, mix_w_out, attn_sinks,
           conv_w, pool_w, pool_scale):
    batch, seq, _ = x.shape
    depth = norm_g.shape[0]
    h = x.reshape(batch * seq, D_MODEL)
    hm = meta_tokens.astype(x.dtype)
    bkt_comb, bkt_meta, bkt_self = _bucket_tables(seq)
    for layer in range(depth):
        last = layer == depth - 1
        h, hm = _ffn(h, hm, norm_g, ffn_w1, ffn_w2, layer, 0, True)
        if layer % 2 == 0:
            e = layer // 2
            q, k, vt, yc, qm, km, vm, ycm = _inproj(h, hm, norm_g, mix_w_in, conv_w, layer, e, seq)
            ya, yam = _attention(q, k, vt, qm, km, vm, bkt_comb, bkt_meta, bkt_self, rel_bias, attn_sinks, e,
                                 batch, seq, not last)
            h, hm = _outproj_ffn(h, ya, yc, hm, yam, ycm, mix_w_out, norm_g, ffn_w1, ffn_w2, layer, e,
                                 not last)
        else:
            h, hm = _pool_ffn(h, hm, norm_g, pool_w, pool_scale, ffn_w1, ffn_w2, layer, layer // 2, seq,
                              not last)
    return h.reshape(batch, seq, D_MODEL)
```
